```python
import math
import jax, jax.numpy as jnp
from jax import lax
import numpy as np

D_MODEL = 1024
BATCH = 4
SEQ = 4096
DEPTH = 2

PLE_DIM = 256
CHUNK = 64
D_RNN = D_MODEL // 2
RNN_BLOCKS = 8
RNN_BLOCK_DIM = D_RNN // RNN_BLOCKS
CONV_WIDTH = 4
LRU_C = 8.0
HEAD_DIM = 64
N_ATT_HEADS = (D_MODEL - D_RNN) // HEAD_DIM
D_ATT = N_ATT_HEADS * HEAD_DIM
D_MIX = D_RNN + D_ATT
D_IN = 2 * D_RNN + 3 * D_ATT + N_ATT_HEADS
Q_BLOCK = 128
N_GROUPS = 4
EXPERTS_PER_GROUP = 8
N_EXPERTS = N_GROUPS * EXPERTS_PER_GROUP
TOP_K = 2
D_EXPERT = D_MODEL // 2
EPS = 1e-6

kernel_name = "hybrid_rglru_fox_hmoe_block"


def rmsnorm(x, g):
    xf = x.astype(jnp.float32)
    y = xf * lax.rsqrt(jnp.mean(xf * xf, axis=-1, keepdims=True) + EPS)
    return (y * g.astype(jnp.float32)).astype(x.dtype)


def causal_depthwise_conv(x, w, b):
    c = x.shape[-1]
    y = lax.conv_general_dilated(
        x, w[:, None, :].astype(x.dtype), window_strides=(1,), padding=[(CONV_WIDTH - 1, 0)],
        dimension_numbers=("NWC", "WIO", "NWC"), feature_group_count=c)
    return y + b.astype(x.dtype)


def rg_lru(xc, w_r, b_r, w_i, b_i, lam):
    B, S, _ = xc.shape
    xb = xc.reshape(B, S, RNN_BLOCKS, RNN_BLOCK_DIM)
    r = jax.nn.sigmoid((jnp.einsum('bshi,hij->bshj', xb, w_r).reshape(B, S, D_RNN) + b_r).astype(jnp.float32))
    i = jax.nn.sigmoid((jnp.einsum('bshi,hij->bshj', xb, w_i).reshape(B, S, D_RNN) + b_i).astype(jnp.float32))
    log_a = -LRU_C * r * jax.nn.softplus(-lam.astype(jnp.float32))
    a = jnp.exp(log_a)
    u = jnp.sqrt(-jnp.expm1(2.0 * log_a)) * (i * xc.astype(jnp.float32))

    def combine(left, right):
        a1, b1 = left
        a2, b2 = right
        return a1 * a2, a2 * b1 + b2

    _, h = lax.associative_scan(combine, (a, u), axis=1)
    return h.astype(xc.dtype)


def forgetting_attention(q, k, v, cum_logf):
    S = q.shape[2]
    scale = 1.0 / math.sqrt(HEAD_DIM)
    outs = []
    for qb in range(S // Q_BLOCK):
        lo, hi = qb * Q_BLOCK, (qb + 1) * Q_BLOCK
        s = jnp.einsum('bhqd,bhkd->bhqk', q[:, :, lo:hi], k[:, :, :hi]).astype(jnp.float32) * scale
        s = s + cum_logf[:, :, lo:hi, None] - cum_logf[:, :, None, :hi]
        mask = jnp.arange(hi)[None, :] <= jnp.arange(lo, hi)[:, None]
        s = jnp.where(mask, s, -jnp.inf)
        pr = jax.nn.softmax(s, axis=-1).astype(v.dtype)
        outs.append(jnp.einsum('bhqk,bhkd->bhqd', pr, v[:, :, :hi]))
    return jnp.concatenate(outs, axis=2)


def hierarchical_moe(n, w_group, b_group, w_router, b_router, w_gate, w_up, w_down):
    B, S, D = n.shape
    T = B * S
    xf = n.reshape(T, D)
    g_prob = jax.nn.softmax((xf @ w_group).astype(jnp.float32) + b_group.astype(jnp.float32), axis=-1)
    g_p, g_idx = lax.top_k(g_prob, 1)
    e_logits = ((xf @ w_router).astype(jnp.float32) + b_router.astype(jnp.float32)).reshape(T, N_GROUPS, EXPERTS_PER_GROUP)
    e_logits = jnp.take_along_axis(e_logits, g_idx[:, :, None], axis=1)[:, 0]
    e_top, e_loc = lax.top_k(e_logits, TOP_K)
    e_w = jax.nn.softmax(e_top, axis=-1) * g_p
    flat_id = (g_idx * EXPERTS_PER_GROUP + e_loc).reshape(-1)
    order = jnp.argsort(flat_id)
    tok = order // TOP_K
    sizes = jnp.bincount(flat_id, length=N_EXPERTS).astype(jnp.int32)
    xs = xf[tok]
    hg = lax.ragged_dot(xs, w_gate, sizes)
    hu = lax.ragged_dot(xs, w_up, sizes)
    ys = lax.ragged_dot(jax.nn.silu(hg) * hu, w_down, sizes)
    ws = e_w.reshape(-1)[order].astype(ys.dtype)
    y = jnp.zeros_like(xf).at[tok].add(ys * ws[:, None])
    return y.reshape(B, S, D)


def setup_inputs(seed: int = 0) -> dict:
    key = jax.random.key(seed)
    ks = iter(jax.random.split(key, 40))
    f32 = jnp.float32

    def nrm(shape, scale):
        return jax.random.normal(next(ks), shape, f32) * scale

    def gain(shape):
        return 1.0 + 0.01 * jax.random.normal(next(ks), shape, f32)

    a8 = jax.random.uniform(next(ks), (DEPTH, D_RNN), f32, 0.9, 0.999)
    a_base = a8 ** (1.0 / LRU_C)
    lru_lambda = jnp.log(a_base) - jnp.log1p(-a_base)

    return {
        "x": jax.random.normal(next(ks), (BATCH, SEQ, D_MODEL), f32),
        "p": jax.random.normal(next(ks), (DEPTH, BATCH, SEQ, PLE_DIM), f32),
        "mix_norm": gain((DEPTH, D_MODEL)),
        "w_in": nrm((DEPTH, D_MODEL, D_IN), D_MODEL ** -0.5),
        "b_forget": jax.random.uniform(next(ks), (DEPTH, N_ATT_HEADS), f32, 1.0, 4.0),
        "conv_w": nrm((DEPTH, CONV_WIDTH, D_RNN), CONV_WIDTH ** -0.5),
        "conv_b": nrm((DEPTH, D_RNN), 0.01),
        "w_rgate": nrm((DEPTH, RNN_BLOCKS, RNN_BLOCK_DIM, RNN_BLOCK_DIM), RNN_BLOCK_DIM ** -0.5),
        "b_rgate": nrm((DEPTH, D_RNN), 0.01),
        "w_igate": nrm((DEPTH, RNN_BLOCKS, RNN_BLOCK_DIM, RNN_BLOCK_DIM), RNN_BLOCK_DIM ** -0.5),
        "b_igate": nrm((DEPTH, D_RNN), 0.01),
        "lru_lambda": lru_lambda,
        "q_norm": gain((DEPTH, HEAD_DIM)),
        "k_norm": gain((DEPTH, HEAD_DIM)),
        "lru_out_norm": gain((DEPTH, D_RNN)),
        "att_out_norm": gain((DEPTH, D_ATT)),
        "w_out": nrm((DEPTH, D_MIX, D_MODEL), D_MIX ** -0.5),
        "ffn_norm": gain((DEPTH, D_MODEL)),
        "w_group": nrm((DEPTH, D_MODEL, N_GROUPS), D_MODEL ** -0.5),
        "b_group": nrm((DEPTH, N_GROUPS), 0.01),
        "w_router": nrm((DEPTH, D_MODEL, N_EXPERTS), D_MODEL ** -0.5),
        "b_router": nrm((DEPTH, N_EXPERTS), 0.01),
        "w_exp_gate": nrm((DEPTH, N_EXPERTS, D_MODEL, D_EXPERT), D_MODEL ** -0.5),
        "w_exp_up": nrm((DEPTH, N_EXPERTS, D_MODEL, D_EXPERT), D_MODEL ** -0.5),
        "w_exp_down": nrm((DEPTH, N_EXPERTS, D_EXPERT, D_MODEL), D_EXPERT ** -0.5),
        "ple_norm": gain((DEPTH, D_MODEL)),
        "w_ple_gate": nrm((DEPTH, D_MODEL, D_MODEL), D_MODEL ** -0.5),
        "b_ple_gate": nrm((DEPTH, D_MODEL), 0.01),
        "w_ple_proj": nrm((DEPTH, PLE_DIM, D_MODEL), PLE_DIM ** -0.5),
    }


def reference(x, p, mix_norm, w_in, b_forget, conv_w, conv_b, w_rgate, b_rgate, w_igate, b_igate,
              lru_lambda, q_norm, k_norm, lru_out_norm, att_out_norm, w_out, ffn_norm,
              w_group, b_group, w_router, b_router, w_exp_gate, w_exp_up, w_exp_down,
              ple_norm, w_ple_gate, b_ple_gate, w_ple_proj):
    B, S, _ = x.shape
    splits = np.cumsum([D_RNN, D_RNN, D_ATT, D_ATT, D_ATT]).tolist()
    for i in range(DEPTH):
        n = rmsnorm(x, mix_norm[i])
        z = n @ w_in[i]
        z_x, z_gate, z_q, z_k, z_v, z_f = jnp.split(z, splits, axis=-1)

        xc = causal_depthwise_conv(z_x, conv_w[i], conv_b[i])
        h = rg_lru(xc, w_rgate[i], b_rgate[i], w_igate[i], b_igate[i], lru_lambda[i])
        y_rnn = h * jax.nn.gelu(z_gate)

        q = rmsnorm(z_q.reshape(B, S, N_ATT_HEADS, HEAD_DIM), q_norm[i]).transpose(0, 2, 1, 3)
        k = rmsnorm(z_k.reshape(B, S, N_ATT_HEADS, HEAD_DIM), k_norm[i]).transpose(0, 2, 1, 3)
        v = z_v.reshape(B, S, N_ATT_HEADS, HEAD_DIM).transpose(0, 2, 1, 3)
        logf = jax.nn.log_sigmoid(z_f.astype(jnp.float32) + b_forget[i].astype(jnp.float32))
        cum_logf = jnp.cumsum(logf, axis=1).transpose(0, 2, 1)
        o = forgetting_attention(q, k, v, cum_logf)
        y_att = o.transpose(0, 2, 1, 3).reshape(B, S, D_ATT)

        y_mix = jnp.concatenate([rmsnorm(y_rnn, lru_out_norm[i]), rmsnorm(y_att, att_out_norm[i])], axis=-1)
        x = x + y_mix @ w_out[i]

        x = x + hierarchical_moe(rmsnorm(x, ffn_norm[i]), w_group[i], b_group[i], w_router[i], b_router[i],
                                 w_exp_gate[i], w_exp_up[i], w_exp_down[i])

        gate = jax.nn.sigmoid(rmsnorm(x, ple_norm[i]) @ w_ple_gate[i] + b_ple_gate[i])
        x = x + (p[i] @ w_ple_proj[i]) * gate
    return x
```

```python
import functools
import math

import jax
import jax.numpy as jnp
from jax import lax
from jax.experimental import pallas as pl
from jax.experimental.pallas import tpu as pltpu

F32 = jnp.float32
BF16 = jnp.bfloat16

EPS = 1e-6
LRU_C = 8.0
HEAD_DIM = 64
CONV_WIDTH = 4
N_GROUPS = 4
EXPERTS_PER_GROUP = 8
N_EXPERTS = N_GROUPS * EXPERTS_PER_GROUP
TOP_K = 2

LANES = 128
SUBLANES = 8
VMEM_LIMIT = 56 * 1024 * 1024

SEQ_BLOCK = 512
Q_BLOCK = 512
FFN_TILE = 256
DISPATCH_BLOCK = 512
COMBINE_BLOCK = 256

R_ID0, R_ID1, R_W0, R_W1, R_RANK0, R_RANK1 = 0, 1, 2, 3, 4, 5


def _sigmoid(x):
    return 1.0 / (1.0 + jnp.exp(-x))


def _softplus(x):
    return jnp.maximum(x, 0.0) + jnp.log1p(jnp.exp(-jnp.abs(x)))


def _gelu_tanh(x):
    c = math.sqrt(2.0 / math.pi)
    return x * (0.5 * (1.0 + jnp.tanh(c * (x + 0.044715 * (x * x * x)))))


def _rms(x):
    return x * lax.rsqrt(jnp.mean(x * x, axis=-1, keepdims=True) + EPS)


def _split_bf16(x):
    hi = x.astype(BF16)
    lo = (x - hi.astype(F32)).astype(BF16)
    return hi, lo


def _params(*sem):
    return pltpu.CompilerParams(dimension_semantics=sem, vmem_limit_bytes=VMEM_LIMIT)


def _in_proj_kernel(x_ref, g_ref, w_ref, gsum_ref, qg_ref, kg_ref, bf_ref, tri_ref,
                    zx_ref, zg_ref, q_ref, k_ref, v_ref, c_ref, carry_ref, *, blocks_per_seq, d_rnn, d_att):
    i = pl.program_id(0)
    n = _rms(x_ref[...]) * g_ref[...]
    z = jnp.dot(n.astype(BF16), w_ref[...], preferred_element_type=F32)
    o = 0
    zx_ref[...] = z[:, o:o + d_rnn]; o += d_rnn
    zg_ref[...] = z[:, o:o + d_rnn]; o += d_rnn
    zq = z[:, o:o + d_att]; o += d_att
    zk = z[:, o:o + d_att]; o += d_att
    v_ref[...] = z[:, o:o + d_att].astype(BF16); o += d_att
    zf = z[:, o:o + LANES]

    def head_norm(t, gain):
        ss = jnp.dot((t * t).astype(BF16), gsum_ref[...], preferred_element_type=F32)
        return t * lax.rsqrt(ss * (1.0 / HEAD_DIM) + EPS) * gain

    q_ref[...] = head_norm(zq, qg_ref[...]).astype(BF16)
    k_ref[...] = head_norm(zk, kg_ref[...]).astype(BF16)

    zft = zf.T[:SUBLANES, :] + bf_ref[...]
    logf = jnp.minimum(zft, 0.0) - jnp.log1p(jnp.exp(-jnp.abs(zft)))
    hi, lo = _split_bf16(logf)
    cs = (jnp.dot(hi, tri_ref[...], preferred_element_type=F32)
          + jnp.dot(lo, tri_ref[...], preferred_element_type=F32))

    @pl.when(i % blocks_per_seq == 0)
    def _():
        carry_ref[...] = jnp.zeros_like(carry_ref)

    c = cs + carry_ref[:, 0:1]
    c_ref[0, 0] = c
    carry_ref[...] = jnp.broadcast_to(c[:, SEQ_BLOCK - 1:SEQ_BLOCK], carry_ref.shape)


def _in_proj(x2, g_mix, w_pad, gsum, qg, kg, bf, tri, *, batch, seq, d_rnn, d_att):
    t, d = x2.shape
    tm = SEQ_BLOCK
    bps = seq // tm
    n_pad = w_pad.shape[1]
    row = lambda i: (i, 0)
    const = lambda i: (0, 0)
    out_shape = (
        jax.ShapeDtypeStruct((t, d_rnn), F32),
        jax.ShapeDtypeStruct((t, d_rnn), F32),
        jax.ShapeDtypeStruct((t, d_att), BF16),
        jax.ShapeDtypeStruct((t, d_att), BF16),
        jax.ShapeDtypeStruct((t, d_att), BF16),
        jax.ShapeDtypeStruct((batch, bps, SUBLANES, tm), F32),
    )
    return pl.pallas_call(
        functools.partial(_in_proj_kernel, blocks_per_seq=bps, d_rnn=d_rnn, d_att=d_att),
        grid=(t // tm,),
        in_specs=[
            pl.BlockSpec((tm, d), row),
            pl.BlockSpec((1, d), const),
            pl.BlockSpec((d, n_pad), const),
            pl.BlockSpec((d_att, d_att), const),
            pl.BlockSpec((1, d_att), const),
            pl.BlockSpec((1, d_att), const),
            pl.BlockSpec((SUBLANES, 1), const),
            pl.BlockSpec((tm, tm), const),
        ],
        out_specs=(
            pl.BlockSpec((tm, d_rnn), row),
            pl.BlockSpec((tm, d_rnn), row),
            pl.BlockSpec((tm, d_att), row),
            pl.BlockSpec((tm, d_att), row),
            pl.BlockSpec((tm, d_att), row),
            pl.BlockSpec((1, 1, SUBLANES, tm), lambda i: (i // bps, i % bps, 0, 0)),
        ),
        out_shape=out_shape,
        scratch_shapes=[pltpu.VMEM((SUBLANES, LANES), F32)],
        compiler_params=_params("arbitrary"),
        name="in_proj",
    )(x2, g_mix, w_pad, gsum, qg, kg, bf, tri)


def _rglru_kernel(zx_ref, zg_ref, cw_ref, cb_ref, wg_ref, br_ref, bi_ref, lam_ref, go_ref,
                  y_ref, tail_ref, h_ref, a_s, u_s, hs_s, *, d_rnn):
    j = pl.program_id(1)
    ts = zx_ref.shape[1]

    @pl.when(j == 0)
    def _():
        tail_ref[...] = jnp.zeros_like(tail_ref)
        h_ref[...] = jnp.zeros_like(h_ref)

    x = zx_ref[0]
    tail = tail_ref[...]
    row8 = lax.broadcasted_iota(jnp.int32, (SUBLANES, d_rnn), 0)

    xc = x * cw_ref[CONV_WIDTH - 1:CONV_WIDTH, :] + cb_ref[...]
    for d in range(1, CONV_WIDTH):
        xr = pltpu.roll(x, d, 0)
        top = jnp.where(row8 < d, pltpu.roll(tail, d, 0), xr[:SUBLANES])
        xs = jnp.concatenate([top, xr[SUBLANES:]], axis=0)
        xc = xc + xs * cw_ref[CONV_WIDTH - 1 - d:CONV_WIDTH - d, :]
    tail_ref[...] = x[ts - SUBLANES:, :]

    gl = jnp.dot(xc.astype(BF16), wg_ref[...], preferred_element_type=F32)
    r = _sigmoid(gl[:, :d_rnn] + br_ref[...])
    ig = _sigmoid(gl[:, d_rnn:] + bi_ref[...])
    log_a = (-LRU_C) * r * _softplus(-lam_ref[...])
    a = jnp.exp(log_a)
    a_s[...] = a
    u_s[...] = jnp.sqrt(1.0 - a * a) * (ig * xc)

    def group(gidx, h):
        r0 = pl.multiple_of(gidx * SUBLANES, SUBLANES)
        a8 = a_s[pl.ds(r0, SUBLANES), :]
        b8 = u_s[pl.ds(r0, SUBLANES), :]
        for d in (1, 2, 4):
            keep = row8 >= d
            a_sh = jnp.where(keep, pltpu.roll(a8, d, 0), 1.0)
            b_sh = jnp.where(keep, pltpu.roll(b8, d, 0), 0.0)
            b8 = a8 * b_sh + b8
            a8 = a8 * a_sh
        h8 = a8 * h + b8
        hs_s[pl.ds(r0, SUBLANES), :] = h8
        return h8[SUBLANES - 1:SUBLANES, :]

    h_last = lax.fori_loop(0, ts // SUBLANES, group, h_ref[0:1, :], unroll=4)
    h_ref[...] = jnp.broadcast_to(h_last, h_ref.shape)

    y = hs_s[...] * _gelu_tanh(zg_ref[0])
    y_ref[0] = (_rms(y) * go_ref[...]).astype(BF16)


def _rglru(zx, zg, conv_w, conv_b, w_gates, b_r, b_i, lam, g_out):
    b, s, d_rnn = zx.shape
    ts = SEQ_BLOCK
    blk = lambda bi, j: (bi, j, 0)
    const = lambda bi, j: (0, 0)
    return pl.pallas_call(
        functools.partial(_rglru_kernel, d_rnn=d_rnn),
        grid=(b, s // ts),
        in_specs=[
            pl.BlockSpec((1, ts, d_rnn), blk),
            pl.BlockSpec((1, ts, d_rnn), blk),
            pl.BlockSpec((CONV_WIDTH, d_rnn), const),
            pl.BlockSpec((1, d_rnn), const),
            pl.BlockSpec((d_rnn, 2 * d_rnn), const),
            pl.BlockSpec((1, d_rnn), const),
            pl.BlockSpec((1, d_rnn), const),
            pl.BlockSpec((1, d_rnn), const),
            pl.BlockSpec((1, d_rnn), const),
        ],
        out_specs=pl.BlockSpec((1, ts, d_rnn), blk),
        out_shape=jax.ShapeDtypeStruct((b, s, d_rnn), BF16),
        scratch_shapes=[
            pltpu.VMEM((SUBLANES, d_rnn), F32),
            pltpu.VMEM((SUBLANES, d_rnn), F32),
            pltpu.VMEM((ts, d_rnn), F32),
            pltpu.VMEM((ts, d_rnn), F32),
            pltpu.VMEM((ts, d_rnn), F32),
        ],
        compiler_params=_params("arbitrary", "arbitrary"),
        name="rglru",
    )(zx, zg, conv_w, conv_b, w_gates, b_r, b_i, lam, g_out)


def _attn_kernel(q_ref, k_ref, v_ref, c_ref, o_ref, *, bq, bk):
    hp = pl.program_id(1)
    qi = pl.program_id(2)
    q2 = q_ref[0]
    lane = lax.broadcasted_iota(jnp.int32, (bq, LANES), 1)
    row = qi * bq + lax.broadcasted_iota(jnp.int32, (bq, bk), 0)
    col0 = lax.broadcasted_iota(jnp.int32, (bq, bk), 1)
    n_full = qi * (bq // bk)
    outs = []
    for j in range(LANES // HEAD_DIM):
        qj = jnp.where(lane // HEAD_DIM == j, q2, jnp.zeros_like(q2))
        crow = hp * (LANES // HEAD_DIM) + j

        def step(kb, carry, masked):
            m, l, acc = carry
            k0 = pl.multiple_of(kb * bk, bk)
            kblk = k_ref[0, pl.ds(k0, bk), :]
            vblk = v_ref[0, pl.ds(k0, bk), :]
            s = lax.dot_general(qj, kblk, (((1,), (1,)), ((), ())), preferred_element_type=F32)
            s = s - c_ref[0, kb, pl.ds(crow, 1), :]
            if masked:
                s = jnp.where(col0 + k0 <= row, s, -jnp.inf)
            m_new = jnp.maximum(m, jnp.max(s, axis=-1, keepdims=True))
            alpha = jnp.exp(m - m_new)
            p = jnp.exp(s - m_new)
            l = alpha * l + jnp.sum(p, axis=-1, keepdims=True)
            acc = alpha * acc + jnp.dot(p.astype(BF16), vblk, preferred_element_type=F32)
            return m_new, l, acc

        carry = (jnp.full((bq, 1), -jnp.inf, F32), jnp.zeros((bq, 1), F32), jnp.zeros((bq, LANES), F32))
        carry = lax.fori_loop(0, n_full, functools.partial(step, masked=False), carry)
        for dd in range(bq // bk):
            carry = step(n_full + dd, carry, True)
        _, l, acc = carry
        outs.append(acc / l)
    o = outs[0]
    for j in range(1, len(outs)):
        o = jnp.where(lane // HEAD_DIM == j, outs[j], o)
    o_ref[0] = o.astype(o_ref.dtype)


def _attention(q, k, v, c):
    b, s, d_att = q.shape
    bq, bk = Q_BLOCK, SEQ_BLOCK
    return pl.pallas_call(
        functools.partial(_attn_kernel, bq=bq, bk=bk),
        grid=(b, d_att // LANES, s // bq),
        in_specs=[
            pl.BlockSpec((1, bq, LANES), lambda bi, hp, qi: (bi, qi, hp)),
            pl.BlockSpec((1, s, LANES), lambda bi, hp, qi: (bi, 0, hp)),
            pl.BlockSpec((1, s, LANES), lambda bi, hp, qi: (bi, 0, hp)),
            pl.BlockSpec((1, s // bk, SUBLANES, bk), lambda bi, hp, qi: (bi, 0, 0, 0)),
        ],
        out_specs=pl.BlockSpec((1, bq, LANES), lambda bi, hp, qi: (bi, qi, hp)),
        out_shape=jax.ShapeDtypeStruct((b, s, d_att), BF16),
        compiler_params=_params("arbitrary", "arbitrary", "arbitrary"),
        name="attention",
    )(q, k, v, c)


def _out_route_kernel(x_ref, yr_ref, ya_ref, ga_ref, wo_ref, gf_ref, wrh_ref, wrl_ref, br_ref, ltri_ref,
                      x1_ref, xn_ref, route_ref, cnt_ref, carry_ref, *, d_rnn):
    i = pl.program_id(0)
    tm = x_ref.shape[0]

    @pl.when(i == 0)
    def _():
        carry_ref[...] = jnp.zeros_like(carry_ref)

    ya = _rms(ya_ref[...].astype(F32)) * ga_ref[...]
    x1 = (x_ref[...]
          + jnp.dot(yr_ref[...], wo_ref[:d_rnn, :], preferred_element_type=F32)
          + jnp.dot(ya.astype(BF16), wo_ref[d_rnn:, :], preferred_element_type=F32))
    x1_ref[...] = x1
    xn = _rms(x1) * gf_ref[...]
    xn_ref[...] = xn

    nh, nl = _split_bf16(xn)
    logits = (jnp.dot(nh, wrh_ref[...], preferred_element_type=F32)
              + jnp.dot(nl, wrh_ref[...], preferred_element_type=F32)
              + jnp.dot(nh, wrl_ref[...], preferred_element_type=F32)) + br_ref[...]

    lane = lax.broadcasted_iota(jnp.int32, (tm, LANES), 1)
    lane_f = lane.astype(F32)
    big = float(LANES)

    def top1(vals):
        m = jnp.max(vals, axis=-1, keepdims=True)
        idx = jnp.min(jnp.where(vals == m, lane_f, big), axis=-1, keepdims=True)
        return m, idx

    is_group = (lane >= N_EXPERTS) & (lane < N_EXPERTS + N_GROUPS)
    gl = jnp.where(is_group, logits, -jnp.inf)
    g_max, g_lane = top1(gl)
    g_p = 1.0 / jnp.sum(jnp.exp(gl - g_max), axis=-1, keepdims=True)
    g_idx = g_lane - float(N_EXPERTS)

    in_group = (lane < N_EXPERTS) & ((lane // EXPERTS_PER_GROUP).astype(F32) == g_idx)
    el = jnp.where(in_group, logits, -jnp.inf)
    m1, i1 = top1(el)
    el2 = jnp.where(lane_f == i1, -jnp.inf, el)
    m2, i2 = top1(el2)
    e2 = jnp.exp(m2 - m1)
    w1 = g_p / (1.0 + e2)
    w2 = g_p * e2 / (1.0 + e2)

    onehot = jnp.where((lane_f == i1) | (lane_f == i2), 1.0, 0.0)
    before = jnp.dot(ltri_ref[...], onehot.astype(BF16), preferred_element_type=F32) + carry_ref[0:1, :]
    rank1 = jnp.sum(jnp.where(lane_f == i1, before, 0.0), axis=-1, keepdims=True)
    rank2 = jnp.sum(jnp.where(lane_f == i2, before, 0.0), axis=-1, keepdims=True)
    carry = carry_ref[0:1, :] + jnp.sum(onehot, axis=0, keepdims=True)
    carry_ref[...] = jnp.broadcast_to(carry, carry_ref.shape)
    cnt_ref[...] = jnp.broadcast_to(carry, cnt_ref.shape)

    rec = jnp.zeros((tm, LANES), F32)
    for ln, val in ((R_ID0, i1), (R_ID1, i2), (R_W0, w1), (R_W1, w2), (R_RANK0, rank1), (R_RANK1, rank2)):
        rec = jnp.where(lane == ln, val, rec)
    route_ref[...] = rec


def _out_route(x2, yr, ya, g_att, w_out, g_ffn, wr_hi, wr_lo, b_rt, ltri, *, d_rnn):
    t, d = x2.shape
    tm = SEQ_BLOCK
    d_att = ya.shape[1]
    row = lambda i: (i, 0)
    const = lambda i: (0, 0)
    return pl.pallas_call(
        functools.partial(_out_route_kernel, d_rnn=d_rnn),
        grid=(t // tm,),
        in_specs=[
            pl.BlockSpec((tm, d), row),
            pl.BlockSpec((tm, d_rnn), row),
            pl.BlockSpec((tm, d_att), row),
            pl.BlockSpec((1, d_att), const),
            pl.BlockSpec((d_rnn + d_att, d), const),
            pl.BlockSpec((1, d), const),
            pl.BlockSpec((d, LANES), const),
            pl.BlockSpec((d, LANES), const),
            pl.BlockSpec((1, LANES), const),
            pl.BlockSpec((tm, tm), const),
        ],
        out_specs=(
            pl.BlockSpec((tm, d), row),
            pl.BlockSpec((tm, d), row),
            pl.BlockSpec((tm, LANES), row),
            pl.BlockSpec((SUBLANES, LANES), const),
        ),
        out_shape=(
            jax.ShapeDtypeStruct((t, d), F32),
            jax.ShapeDtypeStruct((t, d), F32),
            jax.ShapeDtypeStruct((t, LANES), F32),
            jax.ShapeDtypeStruct((SUBLANES, LANES), F32),
        ),
        scratch_shapes=[pltpu.VMEM((SUBLANES, LANES), F32)],
        compiler_params=_params("arbitrary"),
        name="out_route",
    )(x2, yr, ya, g_att, w_out, g_ffn, wr_hi, wr_lo, b_rt, ltri)


def _row_copy(src_ref, src_row, dst_ref, dst_row, sem):
    return pltpu.make_async_copy(src_ref.at[pl.ds(src_row, 1), :], dst_ref.at[pl.ds(dst_row, 1), :], sem)


def _dispatch_kernel(pos_ref, xn_ref, xs_ref, sem):
    tm = xn_ref.shape[0]
    base = pl.program_id(0) * tm

    def issue(r, carry):
        for k in range(TOP_K):
            _row_copy(xn_ref, r, xs_ref, pos_ref[TOP_K * (base + r) + k], sem).start()
        return carry

    def drain(r, carry):
        for k in range(TOP_K):
            _row_copy(xn_ref, r, xs_ref, pos_ref[TOP_K * (base + r) + k], sem).wait()
        return carry

    lax.fori_loop(0, tm, issue, 0)
    lax.fori_loop(0, tm, drain, 0)


def _dispatch(pos, xn):
    t, d = xn.shape
    tm = DISPATCH_BLOCK
    return pl.pallas_call(
        _dispatch_kernel,
        grid_spec=pltpu.PrefetchScalarGridSpec(
            num_scalar_prefetch=1,
            grid=(t // tm,),
            in_specs=[pl.BlockSpec((tm, d), lambda i, pos_ref: (i, 0))],
            out_specs=pl.BlockSpec(memory_space=pl.ANY),
            scratch_shapes=[pltpu.SemaphoreType.DMA(())],
        ),
        out_shape=jax.ShapeDtypeStruct((TOP_K * t, d), F32),
        compiler_params=_params("arbitrary"),
        name="dispatch",
    )(pos, xn)


def _ffn_kernel(tile_ref, exp_ref, lo_ref, hi_ref, first_ref, init_ref,
                xs_ref, wg_ref, wu_ref, wd_ref, ys_ref, wg_s, wu_s, wd_s):
    g = pl.program_id(0)
    tm = xs_ref.shape[0]

    @pl.when(first_ref[g] == 1)
    def _():
        wg_s[...] = wg_ref[...].astype(BF16)
        wu_s[...] = wu_ref[...].astype(BF16)
        wd_s[...] = wd_ref[...].astype(BF16)

    lo = lo_ref[g]
    hi = hi_ref[g]

    @pl.when(hi > lo)
    def _():
        x = xs_ref[...].astype(BF16)
        hg = jnp.dot(x, wg_s[...], preferred_element_type=F32)
        hu = jnp.dot(x, wu_s[...], preferred_element_type=F32)
        h = (hg * _sigmoid(hg)) * hu
        y = jnp.dot(h.astype(BF16), wd_s[...], preferred_element_type=F32)
        row = lax.broadcasted_iota(jnp.int32, y.shape, 0)
        mine = (row >= lo) & (row < hi)

        @pl.when(init_ref[g] == 1)
        def _():
            ys_ref[...] = jnp.where(mine, y, 0.0)

        @pl.when(init_ref[g] == 0)
        def _():
            ys_ref[...] = jnp.where(mine, y, ys_ref[...])


def _expert_ffn(meta, xs, w_gate, w_up, w_down, layer):
    rows, d = xs.shape
    d_e = w_gate.shape[-1]
    tm = FFN_TILE
    steps = meta[0].shape[0]
    x_map = lambda g, tile, exp, lo, hi, first, init: (tile[g], 0)
    w_map = lambda g, tile, exp, lo, hi, first, init: (layer, exp[g], 0, 0)
    return pl.pallas_call(
        _ffn_kernel,
        grid_spec=pltpu.PrefetchScalarGridSpec(
            num_scalar_prefetch=6,
            grid=(steps,),
            in_specs=[
                pl.BlockSpec((tm, d), x_map),
                pl.BlockSpec((None, None, d, d_e), w_map),
                pl.BlockSpec((None, None, d, d_e), w_map),
                pl.BlockSpec((None, None, d_e, d), w_map),
            ],
            out_specs=pl.BlockSpec((tm, d), x_map),
            scratch_shapes=[
                pltpu.VMEM((d, d_e), BF16),
                pltpu.VMEM((d, d_e), BF16),
                pltpu.VMEM((d_e, d), BF16),
            ],
        ),
        out_shape=jax.ShapeDtypeStruct((rows, d), F32),
        compiler_params=_params("arbitrary"),
        name="expert_ffn",
    )(*meta, xs, w_gate, w_up, w_down)


def _ffn_schedule(cnt, off, rows):
    tm = FFN_TILE
    n_tiles = rows // tm
    steps = n_tiles + N_EXPERTS - 1
    end = off + cnt
    first_tile = off // tm
    last_tile = jnp.where(cnt > 0, (end - 1) // tm, first_tile - 1)
    visits = last_tile - first_tile + 1
    vis_end = jnp.cumsum(visits)
    vis_start = vis_end - visits
    total = vis_end[-1]
    g = jnp.arange(steps, dtype=jnp.int32)
    gc = jnp.minimum(g, total - 1)
    exp = jnp.searchsorted(vis_end, gc, side="right").astype(jnp.int32)
    tile = first_tile[exp] + (gc - vis_start[exp])
    valid = g < total
    lo = jnp.where(valid, jnp.maximum(off[exp], tile * tm) - tile * tm, 0)
    hi = jnp.where(valid, jnp.minimum(end[exp], (tile + 1) * tm) - tile * tm, 0)
    prev_exp = jnp.concatenate([jnp.full((1,), -1, jnp.int32), exp[:-1]])
    prev_tile = jnp.concatenate([jnp.full((1,), -1, jnp.int32), tile[:-1]])
    first = (valid & (exp != prev_exp)).astype(jnp.int32)
    init = (valid & (tile != prev_tile)).astype(jnp.int32)
    i32 = lambda a: a.astype(jnp.int32)
    return i32(tile), i32(exp), i32(lo), i32(hi), first, init


def _combine_kernel(pos_ref, x1_ref, route_ref, p_ref, ys_ref, gp_ref, wg_ref, bg_ref, wp_ref,
                    o_ref, buf0, buf1, sem):
    tm = x1_ref.shape[0]
    base = pl.program_id(0) * tm
    bufs = (buf0, buf1)

    def issue(r, carry):
        for k in range(TOP_K):
            _row_copy(ys_ref, pos_ref[TOP_K * (base + r) + k], bufs[k], r, sem).start()
        return carry

    def drain(r, carry):
        for k in range(TOP_K):
            _row_copy(ys_ref, pos_ref[TOP_K * (base + r) + k], bufs[k], r, sem).wait()
        return carry

    lax.fori_loop(0, tm, issue, 0)
    lax.fori_loop(0, tm, drain, 0)

    rec = route_ref[...]
    w0 = rec[:, R_W0:R_W0 + 1]
    w1 = rec[:, R_W1:R_W1 + 1]
    x2 = x1_ref[...] + (buf0[...] * w0 + buf1[...] * w1)
    n = _rms(x2) * gp_ref[...]
    gate = _sigmoid(jnp.dot(n.astype(BF16), wg_ref[...], preferred_element_type=F32) + bg_ref[...])
    pe = jnp.dot(p_ref[...].astype(BF16), wp_ref[...], preferred_element_type=F32)
    o_ref[...] = x2 + pe * gate


def _combine_ple(pos, x1, route, p3, ys, g_ple, w_gate, b_gate, w_proj, layer):
    t, d = x1.shape
    dp = p3.shape[-1]
    tm = COMBINE_BLOCK
    row = lambda i, pos_ref: (i, 0)
    const = lambda i, pos_ref: (0, 0)
    return pl.pallas_call(
        _combine_kernel,
        grid_spec=pltpu.PrefetchScalarGridSpec(
            num_scalar_prefetch=1,
            grid=(t // tm,),
            in_specs=[
                pl.BlockSpec((tm, d), row),
                pl.BlockSpec((tm, LANES), row),
                pl.BlockSpec((None, tm, dp), lambda i, pos_ref: (layer, i, 0)),
                pl.BlockSpec(memory_space=pl.ANY),
                pl.BlockSpec((1, d), const),
                pl.BlockSpec((d, d), const),
                pl.BlockSpec((1, d), const),
                pl.BlockSpec((dp, d), const),
            ],
            out_specs=pl.BlockSpec((tm, d), row),
            scratch_shapes=[
                pltpu.VMEM((tm, d), F32),
                pltpu.VMEM((tm, d), F32),
                pltpu.SemaphoreType.DMA(()),
            ],
        ),
        out_shape=jax.ShapeDtypeStruct((t, d), F32),
        compiler_params=_params("arbitrary"),
        name="combine_ple",
    )(pos, x1, route, p3, ys, g_ple, w_gate, b_gate, w_proj)


def _block_diag(w):
    nb, n, _ = w.shape
    eye = jnp.eye(nb, dtype=w.dtype)
    return jnp.einsum("hij,hg->higj", w, eye).reshape(nb * n, nb * n)


def kernel(x, p, mix_norm, w_in, b_forget, conv_w, conv_b, w_rgate, b_rgate, w_igate, b_igate, lru_lambda,
           q_norm, k_norm, lru_out_norm, att_out_norm, w_out, ffn_norm, w_group, b_group, w_router, b_router,
           w_exp_gate, w_exp_up, w_exp_down, ple_norm, w_ple_gate, b_ple_gate, w_ple_proj):
    batch, seq, d = x.shape
    depth = w_in.shape[0]
    d_rnn = conv_w.shape[-1]
    n_heads = b_forget.shape[-1]
    d_att = n_heads * HEAD_DIM
    t = batch * seq
    assert n_heads == SUBLANES and d_att % LANES == 0 and seq % SEQ_BLOCK == 0 and seq % Q_BLOCK == 0
    assert Q_BLOCK % SEQ_BLOCK == 0 and (TOP_K * t) % FFN_TILE == 0
    assert t % DISPATCH_BLOCK == 0 and t % COMBINE_BLOCK == 0
    assert w_router.shape[-1] == N_EXPERTS and w_group.shape[-1] == N_GROUPS

    idx = jnp.arange(SEQ_BLOCK)
    tri_incl = (idx[:, None] <= idx[None, :]).astype(BF16)
    tri_strict = (idx[None, :] < idx[:, None]).astype(BF16)
    hid = jnp.arange(d_att) // HEAD_DIM
    gsum = (hid[:, None] == hid[None, :]).astype(BF16)
    row2 = lambda a: a.reshape(1, -1).astype(F32)
    n_in_pad = 2 * d_rnn + 3 * d_att + LANES
    p3 = p.reshape(depth, t, p.shape[-1])

    x2 = x.reshape(t, d)
    for i in range(depth):
        w_pad = jnp.pad(w_in[i], ((0, 0), (0, n_in_pad - w_in.shape[-1]))).astype(BF16)
        qg = row2(jnp.tile(q_norm[i], n_heads) * (1.0 / math.sqrt(HEAD_DIM)))
        kg = row2(jnp.tile(k_norm[i], n_heads))
        zx, zg, q, k, v, c = _in_proj(
            x2, row2(mix_norm[i]), w_pad, gsum, qg, kg, b_forget[i].reshape(n_heads, 1).astype(F32), tri_incl,
            batch=batch, seq=seq, d_rnn=d_rnn, d_att=d_att)

        w_gates = jnp.concatenate([_block_diag(w_rgate[i]), _block_diag(w_igate[i])], axis=1).astype(BF16)
        yr = _rglru(zx.reshape(batch, seq, d_rnn), zg.reshape(batch, seq, d_rnn), conv_w[i].astype(F32),
                    row2(conv_b[i]), w_gates, row2(b_rgate[i]), row2(b_igate[i]), row2(lru_lambda[i]),
                    row2(lru_out_norm[i]))
        ya = _attention(q.reshape(batch, seq, d_att), k.reshape(batch, seq, d_att),
                        v.reshape(batch, seq, d_att), c)

        w_rt = jnp.pad(jnp.concatenate([w_router[i], w_group[i]], axis=1),
                       ((0, 0), (0, LANES - N_EXPERTS - N_GROUPS))).astype(F32)
        wr_hi, wr_lo = _split_bf16(w_rt)
        b_rt = jnp.pad(jnp.concatenate([b_router[i], b_group[i]]), (0, LANES - N_EXPERTS - N_GROUPS))
        x1, xn, route, cnt = _out_route(
            x2, yr.reshape(t, d_rnn), ya.reshape(t, d_att), row2(att_out_norm[i]), w_out[i].astype(BF16),
            row2(ffn_norm[i]), wr_hi, wr_lo, row2(b_rt), tri_strict, d_rnn=d_rnn)

        cnt_e = cnt[0, :N_EXPERTS].astype(jnp.int32)
        off_e = jnp.cumsum(cnt_e) - cnt_e
        ids = route[:, R_ID0:R_ID1 + 1].astype(jnp.int32)
        ranks = route[:, R_RANK0:R_RANK1 + 1].astype(jnp.int32)
        pos = (off_e[ids] + ranks).reshape(-1)

        xs = _dispatch(pos, xn)
        ys = _expert_ffn(_ffn_schedule(cnt_e, off_e, TOP_K * t), xs, w_exp_gate, w_exp_up, w_exp_down, i)
        x2 = _combine_ple(pos, x1, route, p3, ys, row2(ple_norm[i]), w_ple_gate[i].astype(BF16),
                          row2(b_ple_gate[i]), w_ple_proj[i].astype(BF16), i)
    return x2.reshape(batch, seq, d)
```

```python
import functools
import math

import jax
import jax.numpy as jnp
from jax import lax
from jax.experimental import pallas as pl
from jax.experimental.pallas import tpu as pltpu

F32 = jnp.float32
BF16 = jnp.bfloat16

EPS = 1e-6
LRU_C = 8.0
HEAD_DIM = 64
CONV_WIDTH = 4
N_GROUPS = 4
EXPERTS_PER_GROUP = 8
N_EXPERTS = N_GROUPS * EXPERTS_PER_GROUP
TOP_K = 2

LOG2E = math.log2(math.e)

LANES = 128
SUBLANES = 8
VMEM_LIMIT = 56 * 1024 * 1024
HEADS_PER_BLOCK = LANES // HEAD_DIM

SEQ_BLOCK = 512
Q_BLOCK = 512
FFN_TILE = 256
DISPATCH_BLOCK = 512
COMBINE_BLOCK = 256

R_ID0, R_ID1, R_W0, R_W1, R_RANK0, R_RANK1 = 0, 1, 2, 3, 4, 5


def _sigmoid(x):
    return 1.0 / (1.0 + jnp.exp(-x))


def _softplus(x):
    return jnp.maximum(x, 0.0) + jnp.log1p(jnp.exp(-jnp.abs(x)))


def _gelu_tanh(x):
    c = math.sqrt(2.0 / math.pi)
    return x * (0.5 * (1.0 + jnp.tanh(c * (x + 0.044715 * (x * x * x)))))


def _rms(x):
    return x * lax.rsqrt(jnp.mean(x * x, axis=-1, keepdims=True) + EPS)


def _split_bf16(x):
    hi = x.astype(BF16)
    lo = (x - hi.astype(F32)).astype(BF16)
    return hi, lo


def _params(*sem):
    return pltpu.CompilerParams(dimension_semantics=sem, vmem_limit_bytes=VMEM_LIMIT)


def _in_proj_kernel(x_ref, g_ref, w_ref, gsum_ref, qg_ref, kg_ref, bf_ref, tri_ref,
                    zx_ref, zg_ref, q_ref, k_ref, v_ref, c_ref, carry_ref, *, blocks_per_seq, d_rnn, d_att):
    i = pl.program_id(0)
    n = _rms(x_ref[...]) * g_ref[...]
    z = jnp.dot(n.astype(BF16), w_ref[...], preferred_element_type=F32)
    o = 0
    zx_ref[...] = z[:, o:o + d_rnn]; o += d_rnn
    zg_ref[...] = z[:, o:o + d_rnn]; o += d_rnn
    zq = z[:, o:o + d_att]; o += d_att
    zk = z[:, o:o + d_att]; o += d_att
    zv = z[:, o:o + d_att]; o += d_att
    zf = z[:, o:o + LANES]

    lane = lax.broadcasted_iota(jnp.int32, (zv.shape[0], LANES), 1)
    v_parts = []
    for blk in range(d_att // LANES):
        vb = zv[:, blk * LANES:(blk + 1) * LANES]
        v_parts += [jnp.where(lane // HEAD_DIM == j, vb, 1.0) for j in range(HEADS_PER_BLOCK)]
    v_ref[...] = jnp.concatenate(v_parts, axis=1).astype(BF16)

    def head_norm(t, gain):
        ss = jnp.dot((t * t).astype(BF16), gsum_ref[...], preferred_element_type=F32)
        return t * lax.rsqrt(ss * (1.0 / HEAD_DIM) + EPS) * gain

    q_ref[...] = head_norm(zq, qg_ref[...]).astype(BF16)
    k_ref[...] = head_norm(zk, kg_ref[...]).astype(BF16)

    zft = zf.T[:SUBLANES, :] + bf_ref[...]
    logf = jnp.minimum(zft, 0.0) - jnp.log1p(jnp.exp(-jnp.abs(zft)))
    hi, lo = _split_bf16(logf)
    cs = (jnp.dot(hi, tri_ref[...], preferred_element_type=F32)
          + jnp.dot(lo, tri_ref[...], preferred_element_type=F32))

    @pl.when(i % blocks_per_seq == 0)
    def _():
        carry_ref[...] = jnp.zeros_like(carry_ref)

    c = cs + carry_ref[:, 0:1]
    c_ref[0, 0] = c * LOG2E
    carry_ref[...] = jnp.broadcast_to(c[:, SEQ_BLOCK - 1:SEQ_BLOCK], carry_ref.shape)


def _in_proj(x2, g_mix, w_pad, gsum, qg, kg, bf, tri, *, batch, seq, d_rnn, d_att):
    t, d = x2.shape
    tm = SEQ_BLOCK
    bps = seq // tm
    n_pad = w_pad.shape[1]
    row = lambda i: (i, 0)
    const = lambda i: (0, 0)
    out_shape = (
        jax.ShapeDtypeStruct((t, d_rnn), F32),
        jax.ShapeDtypeStruct((t, d_rnn), F32),
        jax.ShapeDtypeStruct((t, d_att), BF16),
        jax.ShapeDtypeStruct((t, d_att), BF16),
        jax.ShapeDtypeStruct((t, HEADS_PER_BLOCK * d_att), BF16),
        jax.ShapeDtypeStruct((batch, bps, SUBLANES, tm), F32),
    )
    return pl.pallas_call(
        functools.partial(_in_proj_kernel, blocks_per_seq=bps, d_rnn=d_rnn, d_att=d_att),
        grid=(t // tm,),
        in_specs=[
            pl.BlockSpec((tm, d), row),
            pl.BlockSpec((1, d), const),
            pl.BlockSpec((d, n_pad), const),
            pl.BlockSpec((d_att, d_att), const),
            pl.BlockSpec((1, d_att), const),
            pl.BlockSpec((1, d_att), const),
            pl.BlockSpec((SUBLANES, 1), const),
            pl.BlockSpec((tm, tm), const),
        ],
        out_specs=(
            pl.BlockSpec((tm, d_rnn), row),
            pl.BlockSpec((tm, d_rnn), row),
            pl.BlockSpec((tm, d_att), row),
            pl.BlockSpec((tm, d_att), row),
            pl.BlockSpec((tm, HEADS_PER_BLOCK * d_att), row),
            pl.BlockSpec((1, 1, SUBLANES, tm), lambda i: (i // bps, i % bps, 0, 0)),
        ),
        out_shape=out_shape,
        scratch_shapes=[pltpu.VMEM((SUBLANES, LANES), F32)],
        compiler_params=_params("arbitrary"),
        name="in_proj",
    )(x2, g_mix, w_pad, gsum, qg, kg, bf, tri)


def _rglru_kernel(zx_ref, zg_ref, cw_ref, cb_ref, wg_ref, br_ref, bi_ref, lam_ref, go_ref,
                  y_ref, tail_ref, h_ref, a_s, u_s, hs_s, *, d_rnn):
    j = pl.program_id(1)
    ts = zx_ref.shape[1]

    @pl.when(j == 0)
    def _():
        tail_ref[...] = jnp.zeros_like(tail_ref)
        h_ref[...] = jnp.zeros_like(h_ref)

    x = zx_ref[0]
    tail = tail_ref[...]
    row8 = lax.broadcasted_iota(jnp.int32, (SUBLANES, d_rnn), 0)

    xc = x * cw_ref[CONV_WIDTH - 1:CONV_WIDTH, :] + cb_ref[...]
    for d in range(1, CONV_WIDTH):
        xr = pltpu.roll(x, d, 0)
        top = jnp.where(row8 < d, pltpu.roll(tail, d, 0), xr[:SUBLANES])
        xs = jnp.concatenate([top, xr[SUBLANES:]], axis=0)
        xc = xc + xs * cw_ref[CONV_WIDTH - 1 - d:CONV_WIDTH - d, :]
    tail_ref[...] = x[ts - SUBLANES:, :]

    gl = jnp.dot(xc.astype(BF16), wg_ref[...], preferred_element_type=F32)
    r = _sigmoid(gl[:, :d_rnn] + br_ref[...])
    ig = _sigmoid(gl[:, d_rnn:] + bi_ref[...])
    log_a = (-LRU_C) * r * _softplus(-lam_ref[...])
    a = jnp.exp(log_a)
    a_s[...] = a
    u_s[...] = jnp.sqrt(1.0 - a * a) * (ig * xc)

    def group(gidx, h):
        r0 = pl.multiple_of(gidx * SUBLANES, SUBLANES)
        a8 = a_s[pl.ds(r0, SUBLANES), :]
        b8 = u_s[pl.ds(r0, SUBLANES), :]
        for d in (1, 2, 4):
            keep = row8 >= d
            a_sh = jnp.where(keep, pltpu.roll(a8, d, 0), 1.0)
            b_sh = jnp.where(keep, pltpu.roll(b8, d, 0), 0.0)
            b8 = a8 * b_sh + b8
            a8 = a8 * a_sh
        h8 = a8 * h + b8
        hs_s[pl.ds(r0, SUBLANES), :] = h8
        return h8[SUBLANES - 1:SUBLANES, :]

    h_last = lax.fori_loop(0, ts // SUBLANES, group, h_ref[0:1, :], unroll=4)
    h_ref[...] = jnp.broadcast_to(h_last, h_ref.shape)

    y = hs_s[...] * _gelu_tanh(zg_ref[0])
    y_ref[0] = (_rms(y) * go_ref[...]).astype(BF16)


def _rglru(zx, zg, conv_w, conv_b, w_gates, b_r, b_i, lam, g_out):
    b, s, d_rnn = zx.shape
    ts = SEQ_BLOCK
    blk = lambda bi, j: (bi, j, 0)
    const = lambda bi, j: (0, 0)
    return pl.pallas_call(
        functools.partial(_rglru_kernel, d_rnn=d_rnn),
        grid=(b, s // ts),
        in_specs=[
            pl.BlockSpec((1, ts, d_rnn), blk),
            pl.BlockSpec((1, ts, d_rnn), blk),
            pl.BlockSpec((CONV_WIDTH, d_rnn), const),
            pl.BlockSpec((1, d_rnn), const),
            pl.BlockSpec((d_rnn, 2 * d_rnn), const),
            pl.BlockSpec((1, d_rnn), const),
            pl.BlockSpec((1, d_rnn), const),
            pl.BlockSpec((1, d_rnn), const),
            pl.BlockSpec((1, d_rnn), const),
        ],
        out_specs=pl.BlockSpec((1, ts, d_rnn), blk),
        out_shape=jax.ShapeDtypeStruct((b, s, d_rnn), BF16),
        scratch_shapes=[
            pltpu.VMEM((SUBLANES, d_rnn), F32),
            pltpu.VMEM((SUBLANES, d_rnn), F32),
            pltpu.VMEM((ts, d_rnn), F32),
            pltpu.VMEM((ts, d_rnn), F32),
            pltpu.VMEM((ts, d_rnn), F32),
        ],
        compiler_params=_params("arbitrary", "arbitrary"),
        name="rglru",
    )(zx, zg, conv_w, conv_b, w_gates, b_r, b_i, lam, g_out)


def _attn_kernel(q_ref, k_ref, v_ref, c_ref, o_ref, sa_ref, sb_ref, *, bq, bk):
    hp = pl.program_id(1)
    qi = pl.program_id(2)
    q2 = q_ref[0]
    lane = lax.broadcasted_iota(jnp.int32, (bq, LANES), 1)
    row = qi * bq + lax.broadcasted_iota(jnp.int32, (bq, bk), 0)
    col0 = lax.broadcasted_iota(jnp.int32, (bq, bk), 1)
    n_full = qi * (bq // bk)
    qs = [jnp.where(lane // HEAD_DIM == j, q2, jnp.zeros_like(q2)) for j in range(HEADS_PER_BLOCK)]

    def scores(kb, s_ref):
        k0 = pl.multiple_of(kb * bk, bk)
        kblk = k_ref[0, pl.ds(k0, bk), :]
        for j in range(HEADS_PER_BLOCK):
            s = lax.dot_general(qs[j], kblk, (((1,), (1,)), ((), ())), preferred_element_type=F32)
            s_ref[j] = s - c_ref[0, kb, pl.ds(hp * HEADS_PER_BLOCK + j, 1), :]

    def update(kb, s_ref, stats, masked):
        k0 = pl.multiple_of(kb * bk, bk)
        new = []
        for j in range(HEADS_PER_BLOCK):
            m, acc = stats[j]
            s = s_ref[j]
            if masked:
                s = jnp.where(col0 + k0 <= row, s, -jnp.inf)
            m_new = jnp.maximum(m, jnp.max(s, axis=-1, keepdims=True))
            p = jnp.exp2(s - m_new)
            vj = v_ref[0, pl.ds(k0, bk), j * LANES:(j + 1) * LANES]
            acc = jnp.exp2(m - m_new) * acc + jnp.dot(p.astype(BF16), vj, preferred_element_type=F32)
            new.append((m_new, acc))
        return tuple(new)

    def finish(stats):
        o = None
        for j in range(HEADS_PER_BLOCK):
            acc = stats[j][1]
            oj = acc / pltpu.roll(acc, HEAD_DIM, 1)
            o = oj if o is None else jnp.where(lane // HEAD_DIM == j, oj, o)
        o_ref[0] = o.astype(o_ref.dtype)

    def pair(i, stats):
        scores(2 * i + 1, sb_ref)
        stats = update(2 * i, sa_ref, stats, False)
        scores(2 * i + 2, sa_ref)
        return update(2 * i + 1, sb_ref, stats, False)

    stats = tuple((jnp.full((bq, 1), -jnp.inf, F32), jnp.zeros((bq, LANES), F32)) for _ in range(HEADS_PER_BLOCK))
    scores(0, sa_ref)
    stats = lax.fori_loop(0, n_full // 2, pair, stats)

    @pl.when(n_full % 2 == 0)
    def _():
        finish(update(n_full, sa_ref, stats, True))

    @pl.when(n_full % 2 == 1)
    def _():
        scores(n_full, sb_ref)
        st = update(n_full - 1, sa_ref, stats, False)
        finish(update(n_full, sb_ref, st, True))


def _attention(q, k, v, c):
    b, s, d_att = q.shape
    bq, bk = Q_BLOCK, SEQ_BLOCK
    return pl.pallas_call(
        functools.partial(_attn_kernel, bq=bq, bk=bk),
        grid=(b, d_att // LANES, s // bq),
        in_specs=[
            pl.BlockSpec((1, bq, LANES), lambda bi, hp, qi: (bi, qi, hp)),
            pl.BlockSpec((1, s, LANES), lambda bi, hp, qi: (bi, 0, hp)),
            pl.BlockSpec((1, s, HEADS_PER_BLOCK * LANES), lambda bi, hp, qi: (bi, 0, hp)),
            pl.BlockSpec((1, s // bk, SUBLANES, bk), lambda bi, hp, qi: (bi, 0, 0, 0)),
        ],
        out_specs=pl.BlockSpec((1, bq, LANES), lambda bi, hp, qi: (bi, qi, hp)),
        out_shape=jax.ShapeDtypeStruct((b, s, d_att), BF16),
        scratch_shapes=[pltpu.VMEM((HEADS_PER_BLOCK, bq, bk), F32), pltpu.VMEM((HEADS_PER_BLOCK, bq, bk), F32)],
        compiler_params=_params("arbitrary", "arbitrary", "arbitrary"),
        name="attention",
    )(q, k, v, c)


def _out_route_kernel(x_ref, yr_ref, ya_ref, ga_ref, wo_ref, gf_ref, wrh_ref, wrl_ref, br_ref, ltri_ref,
                      x1_ref, xn_ref, route_ref, cnt_ref, carry_ref, *, d_rnn):
    i = pl.program_id(0)
    tm = x_ref.shape[0]

    @pl.when(i == 0)
    def _():
        carry_ref[...] = jnp.zeros_like(carry_ref)

    ya = _rms(ya_ref[...].astype(F32)) * ga_ref[...]
    x1 = (x_ref[...]
          + jnp.dot(yr_ref[...], wo_ref[:d_rnn, :], preferred_element_type=F32)
          + jnp.dot(ya.astype(BF16), wo_ref[d_rnn:, :], preferred_element_type=F32))
    x1_ref[...] = x1
    xn = _rms(x1) * gf_ref[...]
    xn_ref[...] = xn

    nh, nl = _split_bf16(xn)
    logits = (jnp.dot(nh, wrh_ref[...], preferred_element_type=F32)
              + jnp.dot(nl, wrh_ref[...], preferred_element_type=F32)
              + jnp.dot(nh, wrl_ref[...], preferred_element_type=F32)) + br_ref[...]

    lane = lax.broadcasted_iota(jnp.int32, (tm, LANES), 1)
    lane_f = lane.astype(F32)
    big = float(LANES)

    def top1(vals):
        m = jnp.max(vals, axis=-1, keepdims=True)
        idx = jnp.min(jnp.where(vals == m, lane_f, big), axis=-1, keepdims=True)
        return m, idx

    is_group = (lane >= N_EXPERTS) & (lane < N_EXPERTS + N_GROUPS)
    gl = jnp.where(is_group, logits, -jnp.inf)
    g_max, g_lane = top1(gl)
    g_p = 1.0 / jnp.sum(jnp.exp(gl - g_max), axis=-1, keepdims=True)
    g_idx = g_lane - float(N_EXPERTS)

    in_group = (lane < N_EXPERTS) & ((lane // EXPERTS_PER_GROUP).astype(F32) == g_idx)
    el = jnp.where(in_group, logits, -jnp.inf)
    m1, i1 = top1(el)
    el2 = jnp.where(lane_f == i1, -jnp.inf, el)
    m2, i2 = top1(el2)
    e2 = jnp.exp(m2 - m1)
    w1 = g_p / (1.0 + e2)
    w2 = g_p * e2 / (1.0 + e2)

    onehot = jnp.where((lane_f == i1) | (lane_f == i2), 1.0, 0.0)
    before = jnp.dot(ltri_ref[...], onehot.astype(BF16), preferred_element_type=F32) + carry_ref[0:1, :]
    rank1 = jnp.sum(jnp.where(lane_f == i1, before, 0.0), axis=-1, keepdims=True)
    rank2 = jnp.sum(jnp.where(lane_f == i2, before, 0.0), axis=-1, keepdims=True)
    carry = carry_ref[0:1, :] + jnp.sum(onehot, axis=0, keepdims=True)
    carry_ref[...] = jnp.broadcast_to(carry, carry_ref.shape)
    cnt_ref[...] = jnp.broadcast_to(carry, cnt_ref.shape)

    rec = jnp.zeros((tm, LANES), F32)
    for ln, val in ((R_ID0, i1), (R_ID1, i2), (R_W0, w1), (R_W1, w2), (R_RANK0, rank1), (R_RANK1, rank2)):
        rec = jnp.where(lane == ln, val, rec)
    route_ref[...] = rec


def _out_route(x2, yr, ya, g_att, w_out, g_ffn, wr_hi, wr_lo, b_rt, ltri, *, d_rnn):
    t, d = x2.shape
    tm = SEQ_BLOCK
    d_att = ya.shape[1]
    row = lambda i: (i, 0)
    const = lambda i: (0, 0)
    return pl.pallas_call(
        functools.partial(_out_route_kernel, d_rnn=d_rnn),
        grid=(t // tm,),
        in_specs=[
            pl.BlockSpec((tm, d), row),
            pl.BlockSpec((tm, d_rnn), row),
            pl.BlockSpec((tm, d_att), row),
            pl.BlockSpec((1, d_att), const),
            pl.BlockSpec((d_rnn + d_att, d), const),
            pl.BlockSpec((1, d), const),
            pl.BlockSpec((d, LANES), const),
            pl.BlockSpec((d, LANES), const),
            pl.BlockSpec((1, LANES), const),
            pl.BlockSpec((tm, tm), const),
        ],
        out_specs=(
            pl.BlockSpec((tm, d), row),
            pl.BlockSpec((tm, d), row),
            pl.BlockSpec((tm, LANES), row),
            pl.BlockSpec((SUBLANES, LANES), const),
        ),
        out_shape=(
            jax.ShapeDtypeStruct((t, d), F32),
            jax.ShapeDtypeStruct((t, d), F32),
            jax.ShapeDtypeStruct((t, LANES), F32),
            jax.ShapeDtypeStruct((SUBLANES, LANES), F32),
        ),
        scratch_shapes=[pltpu.VMEM((SUBLANES, LANES), F32)],
        compiler_params=_params("arbitrary"),
        name="out_route",
    )(x2, yr, ya, g_att, w_out, g_ffn, wr_hi, wr_lo, b_rt, ltri)


def _row_copy(src_ref, src_row, dst_ref, dst_row, sem):
    return pltpu.make_async_copy(src_ref.at[pl.ds(src_row, 1), :], dst_ref.at[pl.ds(dst_row, 1), :], sem)


def _dispatch_kernel(pos_ref, xn_ref, xs_ref, sem):
    tm = xn_ref.shape[0]
    base = pl.program_id(0) * tm

    def issue(r, carry):
        for k in range(TOP_K):
            _row_copy(xn_ref, r, xs_ref, pos_ref[TOP_K * (base + r) + k], sem).start()
        return carry

    def drain(r, carry):
        for k in range(TOP_K):
            _row_copy(xn_ref, r, xs_ref, pos_ref[TOP_K * (base + r) + k], sem).wait()
        return carry

    lax.fori_loop(0, tm, issue, 0)
    lax.fori_loop(0, tm, drain, 0)


def _dispatch(pos, xn):
    t, d = xn.shape
    tm = DISPATCH_BLOCK
    return pl.pallas_call(
        _dispatch_kernel,
        grid_spec=pltpu.PrefetchScalarGridSpec(
            num_scalar_prefetch=1,
            grid=(t // tm,),
            in_specs=[pl.BlockSpec((tm, d), lambda i, pos_ref: (i, 0))],
            out_specs=pl.BlockSpec(memory_space=pl.ANY),
            scratch_shapes=[pltpu.SemaphoreType.DMA(())],
        ),
        out_shape=jax.ShapeDtypeStruct((TOP_K * t, d), F32),
        compiler_params=_params("arbitrary"),
        name="dispatch",
    )(pos, xn)


def _ffn_kernel(tile_ref, exp_ref, lo_ref, hi_ref, first_ref, init_ref,
                xs_ref, wg_ref, wu_ref, wd_ref, ys_ref, wg_s, wu_s, wd_s):
    g = pl.program_id(0)
    tm = xs_ref.shape[0]

    @pl.when(first_ref[g] == 1)
    def _():
        wg_s[...] = wg_ref[...].astype(BF16)
        wu_s[...] = wu_ref[...].astype(BF16)
        wd_s[...] = wd_ref[...].astype(BF16)

    lo = lo_ref[g]
    hi = hi_ref[g]

    @pl.when(hi > lo)
    def _():
        x = xs_ref[...].astype(BF16)
        hg = jnp.dot(x, wg_s[...], preferred_element_type=F32)
        hu = jnp.dot(x, wu_s[...], preferred_element_type=F32)
        h = (hg * _sigmoid(hg)) * hu
        y = jnp.dot(h.astype(BF16), wd_s[...], preferred_element_type=F32)
        row = lax.broadcasted_iota(jnp.int32, y.shape, 0)
        mine = (row >= lo) & (row < hi)

        @pl.when(init_ref[g] == 1)
        def _():
            ys_ref[...] = jnp.where(mine, y, 0.0)

        @pl.when(init_ref[g] == 0)
        def _():
            ys_ref[...] = jnp.where(mine, y, ys_ref[...])


def _expert_ffn(meta, xs, w_gate, w_up, w_down, layer):
    rows, d = xs.shape
    d_e = w_gate.shape[-1]
    tm = FFN_TILE
    steps = meta[0].shape[0]
    x_map = lambda g, tile, exp, lo, hi, first, init: (tile[g], 0)
    w_map = lambda g, tile, exp, lo, hi, first, init: (layer, exp[g], 0, 0)
    return pl.pallas_call(
        _ffn_kernel,
        grid_spec=pltpu.PrefetchScalarGridSpec(
            num_scalar_prefetch=6,
            grid=(steps,),
            in_specs=[
                pl.BlockSpec((tm, d), x_map),
                pl.BlockSpec((None, None, d, d_e), w_map),
                pl.BlockSpec((None, None, d, d_e), w_map),
                pl.BlockSpec((None, None, d_e, d), w_map),
            ],
            out_specs=pl.BlockSpec((tm, d), x_map),
            scratch_shapes=[
                pltpu.VMEM((d, d_e), BF16),
                pltpu.VMEM((d, d_e), BF16),
                pltpu.VMEM((d_e, d), BF16),
            ],
        ),
        out_shape=jax.ShapeDtypeStruct((rows, d), F32),
        compiler_params=_params("arbitrary"),
        name="expert_ffn",
    )(*meta, xs, w_gate, w_up, w_down)


def _ffn_schedule(cnt, off, rows):
    tm = FFN_TILE
    n_tiles = rows // tm
    steps = n_tiles + N_EXPERTS - 1
    end = off + cnt
    first_tile = off // tm
    last_tile = jnp.where(cnt > 0, (end - 1) // tm, first_tile - 1)
    visits = last_tile - first_tile + 1
    vis_end = jnp.cumsum(visits)
    vis_start = vis_end - visits
    total = vis_end[-1]
    g = jnp.arange(steps, dtype=jnp.int32)
    gc = jnp.minimum(g, total - 1)
    exp = jnp.searchsorted(vis_end, gc, side="right").astype(jnp.int32)
    tile = first_tile[exp] + (gc - vis_start[exp])
    valid = g < total
    lo = jnp.where(valid, jnp.maximum(off[exp], tile * tm) - tile * tm, 0)
    hi = jnp.where(valid, jnp.minimum(end[exp], (tile + 1) * tm) - tile * tm, 0)
    prev_exp = jnp.concatenate([jnp.full((1,), -1, jnp.int32), exp[:-1]])
    prev_tile = jnp.concatenate([jnp.full((1,), -1, jnp.int32), tile[:-1]])
    first = (valid & (exp != prev_exp)).astype(jnp.int32)
    init = (valid & (tile != prev_tile)).astype(jnp.int32)
    i32 = lambda a: a.astype(jnp.int32)
    return i32(tile), i32(exp), i32(lo), i32(hi), first, init


def _combine_kernel(pos_ref, x1_ref, route_ref, p_ref, ys_ref, gp_ref, wg_ref, bg_ref, wp_ref,
                    o_ref, buf0, buf1, sem):
    tm = x1_ref.shape[0]
    base = pl.program_id(0) * tm
    bufs = (buf0, buf1)

    def issue(r, carry):
        for k in range(TOP_K):
            _row_copy(ys_ref, pos_ref[TOP_K * (base + r) + k], bufs[k], r, sem).start()
        return carry

    def drain(r, carry):
        for k in range(TOP_K):
            _row_copy(ys_ref, pos_ref[TOP_K * (base + r) + k], bufs[k], r, sem).wait()
        return carry

    lax.fori_loop(0, tm, issue, 0)
    lax.fori_loop(0, tm, drain, 0)

    rec = route_ref[...]
    w0 = rec[:, R_W0:R_W0 + 1]
    w1 = rec[:, R_W1:R_W1 + 1]
    x2 = x1_ref[...] + (buf0[...] * w0 + buf1[...] * w1)
    n = _rms(x2) * gp_ref[...]
    gate = _sigmoid(jnp.dot(n.astype(BF16), wg_ref[...], preferred_element_type=F32) + bg_ref[...])
    pe = jnp.dot(p_ref[...].astype(BF16), wp_ref[...], preferred_element_type=F32)
    o_ref[...] = x2 + pe * gate


def _combine_ple(pos, x1, route, p3, ys, g_ple, w_gate, b_gate, w_proj, layer):
    t, d = x1.shape
    dp = p3.shape[-1]
    tm = COMBINE_BLOCK
    row = lambda i, pos_ref: (i, 0)
    const = lambda i, pos_ref: (0, 0)
    return pl.pallas_call(
        _combine_kernel,
        grid_spec=pltpu.PrefetchScalarGridSpec(
            num_scalar_prefetch=1,
            grid=(t // tm,),
            in_specs=[
                pl.BlockSpec((tm, d), row),
                pl.BlockSpec((tm, LANES), row),
                pl.BlockSpec((None, tm, dp), lambda i, pos_ref: (layer, i, 0)),
                pl.BlockSpec(memory_space=pl.ANY),
                pl.BlockSpec((1, d), const),
                pl.BlockSpec((d, d), const),
                pl.BlockSpec((1, d), const),
                pl.BlockSpec((dp, d), const),
            ],
            out_specs=pl.BlockSpec((tm, d), row),
            scratch_shapes=[
                pltpu.VMEM((tm, d), F32),
                pltpu.VMEM((tm, d), F32),
                pltpu.SemaphoreType.DMA(()),
            ],
        ),
        out_shape=jax.ShapeDtypeStruct((t, d), F32),
        compiler_params=_params("arbitrary"),
        name="combine_ple",
    )(pos, x1, route, p3, ys, g_ple, w_gate, b_gate, w_proj)


def _block_diag(w):
    nb, n, _ = w.shape
    eye = jnp.eye(nb, dtype=w.dtype)
    return jnp.einsum("hij,hg->higj", w, eye).reshape(nb * n, nb * n)


def kernel(x, p, mix_norm, w_in, b_forget, conv_w, conv_b, w_rgate, b_rgate, w_igate, b_igate, lru_lambda,
           q_norm, k_norm, lru_out_norm, att_out_norm, w_out, ffn_norm, w_group, b_group, w_router, b_router,
           w_exp_gate, w_exp_up, w_exp_down, ple_norm, w_ple_gate, b_ple_gate, w_ple_proj):
    batch, seq, d = x.shape
    depth = w_in.shape[0]
    d_rnn = conv_w.shape[-1]
    n_heads = b_forget.shape[-1]
    d_att = n_heads * HEAD_DIM
    t = batch * seq
    assert n_heads == SUBLANES and d_att % LANES == 0 and seq % SEQ_BLOCK == 0 and seq % Q_BLOCK == 0
    assert HEADS_PER_BLOCK == 2
    assert Q_BLOCK == SEQ_BLOCK and (TOP_K * t) % FFN_TILE == 0
    assert t % DISPATCH_BLOCK == 0 and t % COMBINE_BLOCK == 0
    assert w_router.shape[-1] == N_EXPERTS and w_group.shape[-1] == N_GROUPS

    idx = jnp.arange(SEQ_BLOCK)
    tri_incl = (idx[:, None] <= idx[None, :]).astype(BF16)
    tri_strict = (idx[None, :] < idx[:, None]).astype(BF16)
    hid = jnp.arange(d_att) // HEAD_DIM
    gsum = (hid[:, None] == hid[None, :]).astype(BF16)
    row2 = lambda a: a.reshape(1, -1).astype(F32)
    n_in_pad = 2 * d_rnn + 3 * d_att + LANES
    p3 = p.reshape(depth, t, p.shape[-1])

    x2 = x.reshape(t, d)
    for i in range(depth):
        w_pad = jnp.pad(w_in[i], ((0, 0), (0, n_in_pad - w_in.shape[-1]))).astype(BF16)
        qg = row2(jnp.tile(q_norm[i], n_heads) * (LOG2E / math.sqrt(HEAD_DIM)))
        kg = row2(jnp.tile(k_norm[i], n_heads))
        zx, zg, q, k, v, c = _in_proj(
            x2, row2(mix_norm[i]), w_pad, gsum, qg, kg, b_forget[i].reshape(n_heads, 1).astype(F32), tri_incl,
            batch=batch, seq=seq, d_rnn=d_rnn, d_att=d_att)

        w_gates = jnp.concatenate([_block_diag(w_rgate[i]), _block_diag(w_igate[i])], axis=1).astype(BF16)
        yr = _rglru(zx.reshape(batch, seq, d_rnn), zg.reshape(batch, seq, d_rnn), conv_w[i].astype(F32),
                    row2(conv_b[i]), w_gates, row2(b_rgate[i]), row2(b_igate[i]), row2(lru_lambda[i]),
                    row2(lru_out_norm[i]))
        ya = _attention(q.reshape(batch, seq, d_att), k.reshape(batch, seq, d_att),
                        v.reshape(batch, seq, HEADS_PER_BLOCK * d_att), c)

        w_rt = jnp.pad(jnp.concatenate([w_router[i], w_group[i]], axis=1),
                       ((0, 0), (0, LANES - N_EXPERTS - N_GROUPS))).astype(F32)
        wr_hi, wr_lo = _split_bf16(w_rt)
        b_rt = jnp.pad(jnp.concatenate([b_router[i], b_group[i]]), (0, LANES - N_EXPERTS - N_GROUPS))
        x1, xn, route, cnt = _out_route(
            x2, yr.reshape(t, d_rnn), ya.reshape(t, d_att), row2(att_out_norm[i]), w_out[i].astype(BF16),
            row2(ffn_norm[i]), wr_hi, wr_lo, row2(b_rt), tri_strict, d_rnn=d_rnn)

        cnt_e = cnt[0, :N_EXPERTS].astype(jnp.int32)
        off_e = jnp.cumsum(cnt_e) - cnt_e
        ids = route[:, R_ID0:R_ID1 + 1].astype(jnp.int32)
        ranks = route[:, R_RANK0:R_RANK1 + 1].astype(jnp.int32)
        pos = (off_e[ids] + ranks).reshape(-1)

        xs = _dispatch(pos, xn)
        ys = _expert_ffn(_ffn_schedule(cnt_e, off_e, TOP_K * t), xs, w_exp_gate, w_exp_up, w_exp_down, i)
        x2 = _combine_ple(pos, x1, route, p3, ys, row2(ple_norm[i]), w_ple_gate[i].astype(BF16),
                          row2(b_ple_gate[i]), w_ple_proj[i].astype(BF16), i)
    return x2.reshape(batch, seq, d)
```

```python
import functools
import math

import jax
import jax.numpy as jnp
from jax import lax
from jax.experimental import pallas as pl
from jax.experimental.pallas import tpu as pltpu

F32 = jnp.float32
BF16 = jnp.bfloat16

EPS = 1e-6
LRU_C = 8.0
HEAD_DIM = 64
CONV_WIDTH = 4
N_GROUPS = 4
EXPERTS_PER_GROUP = 8
N_EXPERTS = N_GROUPS * EXPERTS_PER_GROUP
TOP_K = 2

LOG2E = math.log2(math.e)

LANES = 128
SUBLANES = 8
VMEM_LIMIT = 56 * 1024 * 1024
HEADS_PER_BLOCK = LANES // HEAD_DIM

SEQ_BLOCK = 512
Q_BLOCK = 512
FFN_TILE = 256
DISPATCH_BLOCK = 1024
COMBINE_BLOCK = 256
DMA_UNROLL = 8

R_ID0, R_ID1, R_W0, R_W1, R_RANK0, R_RANK1 = 0, 1, 2, 3, 4, 5


def _sigmoid(x):
    return 1.0 / (1.0 + jnp.exp(-x))


def _softplus(x):
    return jnp.maximum(x, 0.0) + jnp.log1p(jnp.exp(-jnp.abs(x)))


def _gelu_tanh(x):
    c = math.sqrt(2.0 / math.pi)
    return x * (0.5 * (1.0 + jnp.tanh(c * (x + 0.044715 * (x * x * x)))))


def _rms(x):
    return x * lax.rsqrt(jnp.mean(x * x, axis=-1, keepdims=True) + EPS)


def _split_bf16(x):
    hi = x.astype(BF16)
    lo = (x - hi.astype(F32)).astype(BF16)
    return hi, lo


def _params(*sem):
    return pltpu.CompilerParams(dimension_semantics=sem, vmem_limit_bytes=VMEM_LIMIT)


def _in_proj_kernel(x_ref, g_ref, w_ref, gsum_ref, qg_ref, kg_ref, bf_ref, tri_ref,
                    zx_ref, zg_ref, q_ref, k_ref, v_ref, c_ref, carry_ref, *, blocks_per_seq, d_rnn, d_att):
    i = pl.program_id(0)
    n = _rms(x_ref[...]) * g_ref[...]
    z = jnp.dot(n.astype(BF16), w_ref[...], preferred_element_type=F32)
    o = 0
    zx_ref[...] = z[:, o:o + d_rnn]; o += d_rnn
    zg_ref[...] = z[:, o:o + d_rnn]; o += d_rnn
    zq = z[:, o:o + d_att]; o += d_att
    zk = z[:, o:o + d_att]; o += d_att
    zv = z[:, o:o + d_att]; o += d_att
    zf = z[:, o:o + LANES]

    lane = lax.broadcasted_iota(jnp.int32, (zv.shape[0], LANES), 1)
    v_parts = []
    for blk in range(d_att // LANES):
        vb = zv[:, blk * LANES:(blk + 1) * LANES]
        v_parts += [jnp.where(lane // HEAD_DIM == j, vb, 1.0) for j in range(HEADS_PER_BLOCK)]
    v_ref[...] = jnp.concatenate(v_parts, axis=1).astype(BF16)

    def head_norm(t, gain):
        ss = jnp.dot((t * t).astype(BF16), gsum_ref[...], preferred_element_type=F32)
        return t * lax.rsqrt(ss * (1.0 / HEAD_DIM) + EPS) * gain

    q_ref[...] = head_norm(zq, qg_ref[...]).astype(BF16)
    k_ref[...] = head_norm(zk, kg_ref[...]).astype(BF16)

    zft = zf.T[:SUBLANES, :] + bf_ref[...]
    logf = jnp.minimum(zft, 0.0) - jnp.log1p(jnp.exp(-jnp.abs(zft)))
    hi, lo = _split_bf16(logf)
    cs = (jnp.dot(hi, tri_ref[...], preferred_element_type=F32)
          + jnp.dot(lo, tri_ref[...], preferred_element_type=F32))

    @pl.when(i % blocks_per_seq == 0)
    def _():
        carry_ref[...] = jnp.zeros_like(carry_ref)

    c = cs + carry_ref[:, 0:1]
    c_ref[0, 0] = c * LOG2E
    carry_ref[...] = jnp.broadcast_to(c[:, SEQ_BLOCK - 1:SEQ_BLOCK], carry_ref.shape)


def _in_proj(x2, g_mix, w_pad, gsum, qg, kg, bf, tri, *, batch, seq, d_rnn, d_att):
    t, d = x2.shape
    tm = SEQ_BLOCK
    bps = seq // tm
    n_pad = w_pad.shape[1]
    row = lambda i: (i, 0)
    const = lambda i: (0, 0)
    out_shape = (
        jax.ShapeDtypeStruct((t, d_rnn), F32),
        jax.ShapeDtypeStruct((t, d_rnn), F32),
        jax.ShapeDtypeStruct((t, d_att), BF16),
        jax.ShapeDtypeStruct((t, d_att), BF16),
        jax.ShapeDtypeStruct((t, HEADS_PER_BLOCK * d_att), BF16),
        jax.ShapeDtypeStruct((batch, bps, SUBLANES, tm), F32),
    )
    return pl.pallas_call(
        functools.partial(_in_proj_kernel, blocks_per_seq=bps, d_rnn=d_rnn, d_att=d_att),
        grid=(t // tm,),
        in_specs=[
            pl.BlockSpec((tm, d), row),
            pl.BlockSpec((1, d), const),
            pl.BlockSpec((d, n_pad), const),
            pl.BlockSpec((d_att, d_att), const),
            pl.BlockSpec((1, d_att), const),
            pl.BlockSpec((1, d_att), const),
            pl.BlockSpec((SUBLANES, 1), const),
            pl.BlockSpec((tm, tm), const),
        ],
        out_specs=(
            pl.BlockSpec((tm, d_rnn), row),
            pl.BlockSpec((tm, d_rnn), row),
            pl.BlockSpec((tm, d_att), row),
            pl.BlockSpec((tm, d_att), row),
            pl.BlockSpec((tm, HEADS_PER_BLOCK * d_att), row),
            pl.BlockSpec((1, 1, SUBLANES, tm), lambda i: (i // bps, i % bps, 0, 0)),
        ),
        out_shape=out_shape,
        scratch_shapes=[pltpu.VMEM((SUBLANES, LANES), F32)],
        compiler_params=_params("arbitrary"),
        name="in_proj",
    )(x2, g_mix, w_pad, gsum, qg, kg, bf, tri)


def _rglru_kernel(zx_ref, zg_ref, cw_ref, cb_ref, wg_ref, br_ref, bi_ref, lam_ref, go_ref,
                  y_ref, tail_ref, h_ref, a_s, u_s, hs_s, *, d_rnn):
    j = pl.program_id(1)
    ts = zx_ref.shape[1]

    @pl.when(j == 0)
    def _():
        tail_ref[...] = jnp.zeros_like(tail_ref)
        h_ref[...] = jnp.zeros_like(h_ref)

    x = zx_ref[0]
    tail = tail_ref[...]
    row8 = lax.broadcasted_iota(jnp.int32, (SUBLANES, d_rnn), 0)

    xc = x * cw_ref[CONV_WIDTH - 1:CONV_WIDTH, :] + cb_ref[...]
    for d in range(1, CONV_WIDTH):
        xr = pltpu.roll(x, d, 0)
        top = jnp.where(row8 < d, pltpu.roll(tail, d, 0), xr[:SUBLANES])
        xs = jnp.concatenate([top, xr[SUBLANES:]], axis=0)
        xc = xc + xs * cw_ref[CONV_WIDTH - 1 - d:CONV_WIDTH - d, :]
    tail_ref[...] = x[ts - SUBLANES:, :]

    gl = jnp.dot(xc.astype(BF16), wg_ref[...], preferred_element_type=F32)
    r = _sigmoid(gl[:, :d_rnn] + br_ref[...])
    ig = _sigmoid(gl[:, d_rnn:] + bi_ref[...])
    log_a = (-LRU_C) * r * _softplus(-lam_ref[...])
    a = jnp.exp(log_a)
    a_s[...] = a
    u_s[...] = jnp.sqrt(1.0 - a * a) * (ig * xc)

    def group(gidx, h):
        r0 = pl.multiple_of(gidx * SUBLANES, SUBLANES)
        a8 = a_s[pl.ds(r0, SUBLANES), :]
        b8 = u_s[pl.ds(r0, SUBLANES), :]
        for d in (1, 2, 4):
            keep = row8 >= d
            a_sh = jnp.where(keep, pltpu.roll(a8, d, 0), 1.0)
            b_sh = jnp.where(keep, pltpu.roll(b8, d, 0), 0.0)
            b8 = a8 * b_sh + b8
            a8 = a8 * a_sh
        h8 = a8 * h + b8
        hs_s[pl.ds(r0, SUBLANES), :] = h8
        return h8[SUBLANES - 1:SUBLANES, :]

    h_last = lax.fori_loop(0, ts // SUBLANES, group, h_ref[0:1, :], unroll=4)
    h_ref[...] = jnp.broadcast_to(h_last, h_ref.shape)

    y = hs_s[...] * _gelu_tanh(zg_ref[0])
    y_ref[0] = (_rms(y) * go_ref[...]).astype(BF16)


def _rglru(zx, zg, conv_w, conv_b, w_gates, b_r, b_i, lam, g_out):
    b, s, d_rnn = zx.shape
    ts = SEQ_BLOCK
    blk = lambda bi, j: (bi, j, 0)
    const = lambda bi, j: (0, 0)
    return pl.pallas_call(
        functools.partial(_rglru_kernel, d_rnn=d_rnn),
        grid=(b, s // ts),
        in_specs=[
            pl.BlockSpec((1, ts, d_rnn), blk),
            pl.BlockSpec((1, ts, d_rnn), blk),
            pl.BlockSpec((CONV_WIDTH, d_rnn), const),
            pl.BlockSpec((1, d_rnn), const),
            pl.BlockSpec((d_rnn, 2 * d_rnn), const),
            pl.BlockSpec((1, d_rnn), const),
            pl.BlockSpec((1, d_rnn), const),
            pl.BlockSpec((1, d_rnn), const),
            pl.BlockSpec((1, d_rnn), const),
        ],
        out_specs=pl.BlockSpec((1, ts, d_rnn), blk),
        out_shape=jax.ShapeDtypeStruct((b, s, d_rnn), BF16),
        scratch_shapes=[
            pltpu.VMEM((SUBLANES, d_rnn), F32),
            pltpu.VMEM((SUBLANES, d_rnn), F32),
            pltpu.VMEM((ts, d_rnn), F32),
            pltpu.VMEM((ts, d_rnn), F32),
            pltpu.VMEM((ts, d_rnn), F32),
        ],
        compiler_params=_params("arbitrary", "arbitrary"),
        name="rglru",
    )(zx, zg, conv_w, conv_b, w_gates, b_r, b_i, lam, g_out)


def _attn_kernel(q_ref, k_ref, v_ref, c_ref, o_ref, sa_ref, sb_ref, *, bq, bk):
    hp = pl.program_id(1)
    qi = pl.program_id(2)
    q2 = q_ref[0]
    lane = lax.broadcasted_iota(jnp.int32, (bq, LANES), 1)
    row = qi * bq + lax.broadcasted_iota(jnp.int32, (bq, bk), 0)
    col0 = lax.broadcasted_iota(jnp.int32, (bq, bk), 1)
    n_full = qi * (bq // bk)
    qs = [jnp.where(lane // HEAD_DIM == j, q2, jnp.zeros_like(q2)) for j in range(HEADS_PER_BLOCK)]

    def scores(kb, s_ref):
        k0 = pl.multiple_of(kb * bk, bk)
        kblk = k_ref[0, pl.ds(k0, bk), :]
        for j in range(HEADS_PER_BLOCK):
            s = lax.dot_general(qs[j], kblk, (((1,), (1,)), ((), ())), preferred_element_type=F32)
            s_ref[j] = s - c_ref[0, kb, pl.ds(hp * HEADS_PER_BLOCK + j, 1), :]

    def update(kb, s_ref, stats, masked):
        k0 = pl.multiple_of(kb * bk, bk)
        new = []
        for j in range(HEADS_PER_BLOCK):
            m, acc = stats[j]
            s = s_ref[j]
            if masked:
                s = jnp.where(col0 + k0 <= row, s, -jnp.inf)
            m_new = jnp.maximum(m, jnp.max(s, axis=-1, keepdims=True))
            p = jnp.exp2(s - m_new)
            vj = v_ref[0, pl.ds(k0, bk), j * LANES:(j + 1) * LANES]
            acc = jnp.exp2(m - m_new) * acc + jnp.dot(p.astype(BF16), vj, preferred_element_type=F32)
            new.append((m_new, acc))
        return tuple(new)

    def finish(stats):
        o = None
        for j in range(HEADS_PER_BLOCK):
            acc = stats[j][1]
            oj = acc / pltpu.roll(acc, HEAD_DIM, 1)
            o = oj if o is None else jnp.where(lane // HEAD_DIM == j, oj, o)
        o_ref[0] = o.astype(o_ref.dtype)

    def pair(i, stats):
        scores(2 * i + 1, sb_ref)
        stats = update(2 * i, sa_ref, stats, False)
        scores(2 * i + 2, sa_ref)
        return update(2 * i + 1, sb_ref, stats, False)

    stats = tuple((jnp.full((bq, 1), -jnp.inf, F32), jnp.zeros((bq, LANES), F32)) for _ in range(HEADS_PER_BLOCK))
    scores(0, sa_ref)
    stats = lax.fori_loop(0, n_full // 2, pair, stats)

    @pl.when(n_full % 2 == 0)
    def _():
        finish(update(n_full, sa_ref, stats, True))

    @pl.when(n_full % 2 == 1)
    def _():
        scores(n_full, sb_ref)
        st = update(n_full - 1, sa_ref, stats, False)
        finish(update(n_full, sb_ref, st, True))


def _attention(q, k, v, c):
    b, s, d_att = q.shape
    bq, bk = Q_BLOCK, SEQ_BLOCK
    return pl.pallas_call(
        functools.partial(_attn_kernel, bq=bq, bk=bk),
        grid=(b, d_att // LANES, s // bq),
        in_specs=[
            pl.BlockSpec((1, bq, LANES), lambda bi, hp, qi: (bi, qi, hp)),
            pl.BlockSpec((1, s, LANES), lambda bi, hp, qi: (bi, 0, hp)),
            pl.BlockSpec((1, s, HEADS_PER_BLOCK * LANES), lambda bi, hp, qi: (bi, 0, hp)),
            pl.BlockSpec((1, s // bk, SUBLANES, bk), lambda bi, hp, qi: (bi, 0, 0, 0)),
        ],
        out_specs=pl.BlockSpec((1, bq, LANES), lambda bi, hp, qi: (bi, qi, hp)),
        out_shape=jax.ShapeDtypeStruct((b, s, d_att), BF16),
        scratch_shapes=[pltpu.VMEM((HEADS_PER_BLOCK, bq, bk), F32), pltpu.VMEM((HEADS_PER_BLOCK, bq, bk), F32)],
        compiler_params=_params("arbitrary", "arbitrary", "arbitrary"),
        name="attention",
    )(q, k, v, c)


def _out_route_kernel(x_ref, yr_ref, ya_ref, ga_ref, wo_ref, gf_ref, wrh_ref, wrl_ref, br_ref, ltri_ref,
                      x1_ref, xn_ref, route_ref, meta_ref, cnt_ref, carry_ref, *, d_rnn):
    i = pl.program_id(0)
    tm = x_ref.shape[0]

    @pl.when(i == 0)
    def _():
        carry_ref[...] = jnp.zeros_like(carry_ref)

    ya = _rms(ya_ref[...].astype(F32)) * ga_ref[...]
    x1 = (x_ref[...]
          + jnp.dot(yr_ref[...], wo_ref[:d_rnn, :], preferred_element_type=F32)
          + jnp.dot(ya.astype(BF16), wo_ref[d_rnn:, :], preferred_element_type=F32))
    x1_ref[...] = x1
    xn = _rms(x1) * gf_ref[...]
    xn_ref[...] = xn

    nh, nl = _split_bf16(xn)
    logits = (jnp.dot(nh, wrh_ref[...], preferred_element_type=F32)
              + jnp.dot(nl, wrh_ref[...], preferred_element_type=F32)
              + jnp.dot(nh, wrl_ref[...], preferred_element_type=F32)) + br_ref[...]

    lane = lax.broadcasted_iota(jnp.int32, (tm, LANES), 1)
    lane_f = lane.astype(F32)
    big = float(LANES)

    def top1(vals):
        m = jnp.max(vals, axis=-1, keepdims=True)
        idx = jnp.min(jnp.where(vals == m, lane_f, big), axis=-1, keepdims=True)
        return m, idx

    is_group = (lane >= N_EXPERTS) & (lane < N_EXPERTS + N_GROUPS)
    gl = jnp.where(is_group, logits, -jnp.inf)
    g_max, g_lane = top1(gl)
    g_p = 1.0 / jnp.sum(jnp.exp(gl - g_max), axis=-1, keepdims=True)
    g_idx = g_lane - float(N_EXPERTS)

    in_group = (lane < N_EXPERTS) & ((lane // EXPERTS_PER_GROUP).astype(F32) == g_idx)
    el = jnp.where(in_group, logits, -jnp.inf)
    m1, i1 = top1(el)
    el2 = jnp.where(lane_f == i1, -jnp.inf, el)
    m2, i2 = top1(el2)
    e2 = jnp.exp(m2 - m1)
    w1 = g_p / (1.0 + e2)
    w2 = g_p * e2 / (1.0 + e2)

    onehot = jnp.where((lane_f == i1) | (lane_f == i2), 1.0, 0.0)
    before = jnp.dot(ltri_ref[...], onehot.astype(BF16), preferred_element_type=F32) + carry_ref[0:1, :]
    rank1 = jnp.sum(jnp.where(lane_f == i1, before, 0.0), axis=-1, keepdims=True)
    rank2 = jnp.sum(jnp.where(lane_f == i2, before, 0.0), axis=-1, keepdims=True)
    carry = carry_ref[0:1, :] + jnp.sum(onehot, axis=0, keepdims=True)
    carry_ref[...] = jnp.broadcast_to(carry, carry_ref.shape)
    cnt_ref[...] = jnp.broadcast_to(carry, cnt_ref.shape)

    rec = jnp.zeros((tm, LANES), F32)
    for ln, val in ((R_ID0, i1), (R_ID1, i2), (R_W0, w1), (R_W1, w2), (R_RANK0, rank1), (R_RANK1, rank2)):
        rec = jnp.where(lane == ln, val, rec)
    route_ref[...] = rec
    meta_ref[...] = rec.T[:SUBLANES, :]


def _out_route(x2, yr, ya, g_att, w_out, g_ffn, wr_hi, wr_lo, b_rt, ltri, *, d_rnn):
    t, d = x2.shape
    tm = SEQ_BLOCK
    d_att = ya.shape[1]
    row = lambda i: (i, 0)
    const = lambda i: (0, 0)
    return pl.pallas_call(
        functools.partial(_out_route_kernel, d_rnn=d_rnn),
        grid=(t // tm,),
        in_specs=[
            pl.BlockSpec((tm, d), row),
            pl.BlockSpec((tm, d_rnn), row),
            pl.BlockSpec((tm, d_att), row),
            pl.BlockSpec((1, d_att), const),
            pl.BlockSpec((d_rnn + d_att, d), const),
            pl.BlockSpec((1, d), const),
            pl.BlockSpec((d, LANES), const),
            pl.BlockSpec((d, LANES), const),
            pl.BlockSpec((1, LANES), const),
            pl.BlockSpec((tm, tm), const),
        ],
        out_specs=(
            pl.BlockSpec((tm, d), row),
            pl.BlockSpec((tm, d), row),
            pl.BlockSpec((tm, LANES), row),
            pl.BlockSpec((SUBLANES, tm), lambda i: (0, i)),
            pl.BlockSpec((SUBLANES, LANES), const),
        ),
        out_shape=(
            jax.ShapeDtypeStruct((t, d), F32),
            jax.ShapeDtypeStruct((t, d), F32),
            jax.ShapeDtypeStruct((t, LANES), F32),
            jax.ShapeDtypeStruct((SUBLANES, t), F32),
            jax.ShapeDtypeStruct((SUBLANES, LANES), F32),
        ),
        scratch_shapes=[pltpu.VMEM((SUBLANES, LANES), F32)],
        compiler_params=_params("arbitrary"),
        name="out_route",
    )(x2, yr, ya, g_att, w_out, g_ffn, wr_hi, wr_lo, b_rt, ltri)


def _row_copy(src_ref, src_row, dst_ref, dst_row, sem):
    return pltpu.make_async_copy(src_ref.at[pl.ds(src_row, 1), :], dst_ref.at[pl.ds(dst_row, 1), :], sem)


def _dispatch_kernel(pos_ref, xn_ref, xs_ref, sem, *, n_tok):
    tm = xn_ref.shape[0]
    base = pl.program_id(0) * tm

    def issue(r, carry):
        for k in range(TOP_K):
            _row_copy(xn_ref, r, xs_ref, pos_ref[k * n_tok + base + r], sem).start()
        return carry

    def drain(r, carry):
        for k in range(TOP_K):
            _row_copy(xn_ref, 0, xs_ref, 0, sem).wait()
        return carry

    lax.fori_loop(0, tm, issue, 0, unroll=DMA_UNROLL)
    lax.fori_loop(0, tm, drain, 0, unroll=DMA_UNROLL)


def _dispatch(pos, xn):
    t, d = xn.shape
    tm = DISPATCH_BLOCK
    return pl.pallas_call(
        functools.partial(_dispatch_kernel, n_tok=t),
        grid_spec=pltpu.PrefetchScalarGridSpec(
            num_scalar_prefetch=1,
            grid=(t // tm,),
            in_specs=[pl.BlockSpec((tm, d), lambda i, pos_ref: (i, 0))],
            out_specs=pl.BlockSpec(memory_space=pl.ANY),
            scratch_shapes=[pltpu.SemaphoreType.DMA(())],
        ),
        out_shape=jax.ShapeDtypeStruct((TOP_K * t, d), F32),
        compiler_params=_params("arbitrary"),
        name="dispatch",
    )(pos, xn)


def _ffn_kernel(tile_ref, exp_ref, lo_ref, hi_ref, first_ref, init_ref,
                xs_ref, wg_ref, wu_ref, wd_ref, ys_ref, wg_s, wu_s, wd_s):
    g = pl.program_id(0)
    tm = xs_ref.shape[0]

    @pl.when(first_ref[g] == 1)
    def _():
        wg_s[...] = wg_ref[...].astype(BF16)
        wu_s[...] = wu_ref[...].astype(BF16)
        wd_s[...] = wd_ref[...].astype(BF16)

    lo = lo_ref[g]
    hi = hi_ref[g]

    @pl.when(hi > lo)
    def _():
        x = xs_ref[...].astype(BF16)
        hg = jnp.dot(x, wg_s[...], preferred_element_type=F32)
        hu = jnp.dot(x, wu_s[...], preferred_element_type=F32)
        h = (hg * _sigmoid(hg)) * hu
        y = jnp.dot(h.astype(BF16), wd_s[...], preferred_element_type=F32)
        row = lax.broadcasted_iota(jnp.int32, y.shape, 0)
        mine = (row >= lo) & (row < hi)

        @pl.when(init_ref[g] == 1)
        def _():
            ys_ref[...] = jnp.where(mine, y, 0.0)

        @pl.when(init_ref[g] == 0)
        def _():
            ys_ref[...] = jnp.where(mine, y, ys_ref[...])


def _expert_ffn(meta, xs, w_gate, w_up, w_down, layer):
    rows, d = xs.shape
    d_e = w_gate.shape[-1]
    tm = FFN_TILE
    steps = meta[0].shape[0]
    x_map = lambda g, tile, exp, lo, hi, first, init: (tile[g], 0)
    w_map = lambda g, tile, exp, lo, hi, first, init: (layer, exp[g], 0, 0)
    return pl.pallas_call(
        _ffn_kernel,
        grid_spec=pltpu.PrefetchScalarGridSpec(
            num_scalar_prefetch=6,
            grid=(steps,),
            in_specs=[
                pl.BlockSpec((tm, d), x_map),
                pl.BlockSpec((None, None, d, d_e), w_map),
                pl.BlockSpec((None, None, d, d_e), w_map),
                pl.BlockSpec((None, None, d_e, d), w_map),
            ],
            out_specs=pl.BlockSpec((tm, d), x_map),
            scratch_shapes=[
                pltpu.VMEM((d, d_e), BF16),
                pltpu.VMEM((d, d_e), BF16),
                pltpu.VMEM((d_e, d), BF16),
            ],
        ),
        out_shape=jax.ShapeDtypeStruct((rows, d), F32),
        compiler_params=_params("arbitrary"),
        name="expert_ffn",
    )(*meta, xs, w_gate, w_up, w_down)


def _ffn_schedule(cnt, off, rows):
    tm = FFN_TILE
    n_tiles = rows // tm
    steps = n_tiles + N_EXPERTS - 1
    end = off + cnt
    first_tile = off // tm
    last_tile = jnp.where(cnt > 0, (end - 1) // tm, first_tile - 1)
    visits = last_tile - first_tile + 1
    vis_end = jnp.cumsum(visits)
    vis_start = vis_end - visits
    total = vis_end[-1]
    g = jnp.arange(steps, dtype=jnp.int32)
    gc = jnp.minimum(g, total - 1)
    exp = jnp.searchsorted(vis_end, gc, side="right").astype(jnp.int32)
    tile = first_tile[exp] + (gc - vis_start[exp])
    valid = g < total
    lo = jnp.where(valid, jnp.maximum(off[exp], tile * tm) - tile * tm, 0)
    hi = jnp.where(valid, jnp.minimum(end[exp], (tile + 1) * tm) - tile * tm, 0)
    prev_exp = jnp.concatenate([jnp.full((1,), -1, jnp.int32), exp[:-1]])
    prev_tile = jnp.concatenate([jnp.full((1,), -1, jnp.int32), tile[:-1]])
    first = (valid & (exp != prev_exp)).astype(jnp.int32)
    init = (valid & (tile != prev_tile)).astype(jnp.int32)
    i32 = lambda a: a.astype(jnp.int32)
    return i32(tile), i32(exp), i32(lo), i32(hi), first, init


def _combine_kernel(pos_ref, x1_ref, route_ref, p_ref, ys_ref, gp_ref, wg_ref, bg_ref, wp_ref,
                    o_ref, buf, sem, *, n_tok):
    tm = x1_ref.shape[0]
    i = pl.program_id(0)
    slot = i % 2

    def gather(block, slot_):
        def issue(r, carry):
            for k in range(TOP_K):
                src = pos_ref[k * n_tok + block * tm + r]
                _row_copy(ys_ref, src, buf.at[slot_, k], r, sem.at[slot_]).start()
            return carry
        lax.fori_loop(0, tm, issue, 0, unroll=DMA_UNROLL)

    @pl.when(i == 0)
    def _():
        gather(0, 0)

    @pl.when(i + 1 < pl.num_programs(0))
    def _():
        gather(i + 1, 1 - slot)

    def drain(r, carry):
        for k in range(TOP_K):
            _row_copy(ys_ref, 0, buf.at[slot, k], 0, sem.at[slot]).wait()
        return carry

    lax.fori_loop(0, tm, drain, 0, unroll=DMA_UNROLL)

    rec = route_ref[...]
    w0 = rec[:, R_W0:R_W0 + 1]
    w1 = rec[:, R_W1:R_W1 + 1]
    x2 = x1_ref[...] + (buf[slot, 0] * w0 + buf[slot, 1] * w1)
    n = _rms(x2) * gp_ref[...]
    gate = _sigmoid(jnp.dot(n.astype(BF16), wg_ref[...], preferred_element_type=F32) + bg_ref[...])
    pe = jnp.dot(p_ref[...].astype(BF16), wp_ref[...], preferred_element_type=F32)
    o_ref[...] = x2 + pe * gate


def _combine_ple(pos, x1, route, p3, ys, g_ple, w_gate, b_gate, w_proj, layer):
    t, d = x1.shape
    dp = p3.shape[-1]
    tm = COMBINE_BLOCK
    row = lambda i, pos_ref: (i, 0)
    const = lambda i, pos_ref: (0, 0)
    return pl.pallas_call(
        functools.partial(_combine_kernel, n_tok=t),
        grid_spec=pltpu.PrefetchScalarGridSpec(
            num_scalar_prefetch=1,
            grid=(t // tm,),
            in_specs=[
                pl.BlockSpec((tm, d), row),
                pl.BlockSpec((tm, LANES), row),
                pl.BlockSpec((None, tm, dp), lambda i, pos_ref: (layer, i, 0)),
                pl.BlockSpec(memory_space=pl.ANY),
                pl.BlockSpec((1, d), const),
                pl.BlockSpec((d, d), const),
                pl.BlockSpec((1, d), const),
                pl.BlockSpec((dp, d), const),
            ],
            out_specs=pl.BlockSpec((tm, d), row),
            scratch_shapes=[
                pltpu.VMEM((2, TOP_K, tm, d), F32),
                pltpu.SemaphoreType.DMA((2,)),
            ],
        ),
        out_shape=jax.ShapeDtypeStruct((t, d), F32),
        compiler_params=_params("arbitrary"),
        name="combine_ple",
    )(pos, x1, route, p3, ys, g_ple, w_gate, b_gate, w_proj)


def _block_diag(w):
    nb, n, _ = w.shape
    eye = jnp.eye(nb, dtype=w.dtype)
    return jnp.einsum("hij,hg->higj", w, eye).reshape(nb * n, nb * n)


def kernel(x, p, mix_norm, w_in, b_forget, conv_w, conv_b, w_rgate, b_rgate, w_igate, b_igate, lru_lambda,
           q_norm, k_norm, lru_out_norm, att_out_norm, w_out, ffn_norm, w_group, b_group, w_router, b_router,
           w_exp_gate, w_exp_up, w_exp_down, ple_norm, w_ple_gate, b_ple_gate, w_ple_proj):
    batch, seq, d = x.shape
    depth = w_in.shape[0]
    d_rnn = conv_w.shape[-1]
    n_heads = b_forget.shape[-1]
    d_att = n_heads * HEAD_DIM
    t = batch * seq
    assert n_heads == SUBLANES and d_att % LANES == 0 and seq % SEQ_BLOCK == 0 and seq % Q_BLOCK == 0
    assert HEADS_PER_BLOCK == 2
    assert Q_BLOCK == SEQ_BLOCK and (TOP_K * t) % FFN_TILE == 0
    assert t % DISPATCH_BLOCK == 0 and t % COMBINE_BLOCK == 0
    assert w_router.shape[-1] == N_EXPERTS and w_group.shape[-1] == N_GROUPS

    idx = jnp.arange(SEQ_BLOCK)
    tri_incl = (idx[:, None] <= idx[None, :]).astype(BF16)
    tri_strict = (idx[None, :] < idx[:, None]).astype(BF16)
    hid = jnp.arange(d_att) // HEAD_DIM
    gsum = (hid[:, None] == hid[None, :]).astype(BF16)
    row2 = lambda a: a.reshape(1, -1).astype(F32)
    n_in_pad = 2 * d_rnn + 3 * d_att + LANES
    p3 = p.reshape(depth, t, p.shape[-1])

    x2 = x.reshape(t, d)
    for i in range(depth):
        w_pad = jnp.pad(w_in[i], ((0, 0), (0, n_in_pad - w_in.shape[-1]))).astype(BF16)
        qg = row2(jnp.tile(q_norm[i], n_heads) * (LOG2E / math.sqrt(HEAD_DIM)))
        kg = row2(jnp.tile(k_norm[i], n_heads))
        zx, zg, q, k, v, c = _in_proj(
            x2, row2(mix_norm[i]), w_pad, gsum, qg, kg, b_forget[i].reshape(n_heads, 1).astype(F32), tri_incl,
            batch=batch, seq=seq, d_rnn=d_rnn, d_att=d_att)

        w_gates = jnp.concatenate([_block_diag(w_rgate[i]), _block_diag(w_igate[i])], axis=1).astype(BF16)
        yr = _rglru(zx.reshape(batch, seq, d_rnn), zg.reshape(batch, seq, d_rnn), conv_w[i].astype(F32),
                    row2(conv_b[i]), w_gates, row2(b_rgate[i]), row2(b_igate[i]), row2(lru_lambda[i]),
                    row2(lru_out_norm[i]))
        ya = _attention(q.reshape(batch, seq, d_att), k.reshape(batch, seq, d_att),
                        v.reshape(batch, seq, HEADS_PER_BLOCK * d_att), c)

        w_rt = jnp.pad(jnp.concatenate([w_router[i], w_group[i]], axis=1),
                       ((0, 0), (0, LANES - N_EXPERTS - N_GROUPS))).astype(F32)
        wr_hi, wr_lo = _split_bf16(w_rt)
        b_rt = jnp.pad(jnp.concatenate([b_router[i], b_group[i]]), (0, LANES - N_EXPERTS - N_GROUPS))
        x1, xn, route, meta, cnt = _out_route(
            x2, yr.reshape(t, d_rnn), ya.reshape(t, d_att), row2(att_out_norm[i]), w_out[i].astype(BF16),
            row2(ffn_norm[i]), wr_hi, wr_lo, row2(b_rt), tri_strict, d_rnn=d_rnn)

        cnt_e = cnt[0, :N_EXPERTS].astype(jnp.int32)
        off_e = jnp.cumsum(cnt_e) - cnt_e
        ids = meta[R_ID0:R_ID1 + 1].astype(jnp.int32)
        ranks = meta[R_RANK0:R_RANK1 + 1].astype(jnp.int32)
        pos = (off_e[ids] + ranks).reshape(-1)

        xs = _dispatch(pos, xn)
        ys = _expert_ffn(_ffn_schedule(cnt_e, off_e, TOP_K * t), xs, w_exp_gate, w_exp_up, w_exp_down, i)
        x2 = _combine_ple(pos, x1, route, p3, ys, row2(ple_norm[i]), w_ple_gate[i].astype(BF16),
                          row2(b_ple_gate[i]), w_ple_proj[i].astype(BF16), i)
    return x2.reshape(batch, seq, d)
```

```python
import functools
import math

import jax
import jax.numpy as jnp
from jax import lax
from jax.experimental import pallas as pl
from jax.experimental.pallas import tpu as pltpu

F32 = jnp.float32
BF16 = jnp.bfloat16

EPS = 1e-6
LRU_C = 8.0
HEAD_DIM = 64
CONV_WIDTH = 4
N_GROUPS = 4
EXPERTS_PER_GROUP = 8
N_EXPERTS = N_GROUPS * EXPERTS_PER_GROUP
TOP_K = 2

LOG2E = math.log2(math.e)

LANES = 128
SUBLANES = 8
VMEM_LIMIT = 56 * 1024 * 1024
HEADS_PER_BLOCK = LANES // HEAD_DIM

SEQ_BLOCK = 512
Q_BLOCK = 512
FFN_TILE = 256
DISPATCH_BLOCK = 1024
COMBINE_BLOCK = 256
DMA_UNROLL = 8

R_ID0, R_ID1, R_W0, R_W1, R_RANK0, R_RANK1 = 0, 1, 2, 3, 4, 5


def _sigmoid(x):
    return 1.0 / (1.0 + jnp.exp(-x))


def _softplus(x):
    return jnp.maximum(x, 0.0) + jnp.log1p(jnp.exp(-jnp.abs(x)))


def _gelu_tanh(x):
    c = math.sqrt(2.0 / math.pi)
    return x * (0.5 * (1.0 + jnp.tanh(c * (x + 0.044715 * (x * x * x)))))


def _rms(x):
    return x * lax.rsqrt(jnp.mean(x * x, axis=-1, keepdims=True) + EPS)


def _split_bf16(x):
    hi = x.astype(BF16)
    lo = (x - hi.astype(F32)).astype(BF16)
    return hi, lo


def _params(*sem):
    return pltpu.CompilerParams(dimension_semantics=sem, vmem_limit_bytes=VMEM_LIMIT)


def _in_proj_kernel(x_ref, g_ref, w_ref, gsum_ref, qg_ref, kg_ref, bf_ref, tri_ref,
                    zx_ref, zg_ref, q_ref, k_ref, v_ref, c_ref, carry_ref, *, blocks_per_seq, d_rnn, d_att):
    i = pl.program_id(0)
    n = _rms(x_ref[...]) * g_ref[...]
    z = jnp.dot(n.astype(BF16), w_ref[...], preferred_element_type=F32)
    o = 0
    zx_ref[...] = z[:, o:o + d_rnn]; o += d_rnn
    zg_ref[...] = z[:, o:o + d_rnn]; o += d_rnn
    zq = z[:, o:o + d_att]; o += d_att
    zk = z[:, o:o + d_att]; o += d_att
    zv = z[:, o:o + d_att]; o += d_att
    zf = z[:, o:o + LANES]

    lane = lax.broadcasted_iota(jnp.int32, (zv.shape[0], LANES), 1)
    v_parts = []
    for blk in range(d_att // LANES):
        vb = zv[:, blk * LANES:(blk + 1) * LANES]
        v_parts += [jnp.where(lane // HEAD_DIM == j, vb, 1.0) for j in range(HEADS_PER_BLOCK)]
    v_ref[...] = jnp.concatenate(v_parts, axis=1).astype(BF16)

    def head_norm(t, gain):
        ss = jnp.dot((t * t).astype(BF16), gsum_ref[...], preferred_element_type=F32)
        return t * lax.rsqrt(ss * (1.0 / HEAD_DIM) + EPS) * gain

    q_ref[...] = head_norm(zq, qg_ref[...]).astype(BF16)
    k_ref[...] = head_norm(zk, kg_ref[...]).astype(BF16)

    zft = zf.T[:SUBLANES, :] + bf_ref[...]
    logf = jnp.minimum(zft, 0.0) - jnp.log1p(jnp.exp(-jnp.abs(zft)))
    hi, lo = _split_bf16(logf)
    cs = (jnp.dot(hi, tri_ref[...], preferred_element_type=F32)
          + jnp.dot(lo, tri_ref[...], preferred_element_type=F32))

    @pl.when(i % blocks_per_seq == 0)
    def _():
        carry_ref[...] = jnp.zeros_like(carry_ref)

    c = cs + carry_ref[:, 0:1]
    c_ref[0, 0] = c * LOG2E
    carry_ref[...] = jnp.broadcast_to(c[:, SEQ_BLOCK - 1:SEQ_BLOCK], carry_ref.shape)


def _in_proj(x2, g_mix, w_pad, gsum, qg, kg, bf, tri, *, batch, seq, d_rnn, d_att):
    t, d = x2.shape
    tm = SEQ_BLOCK
    bps = seq // tm
    n_pad = w_pad.shape[1]
    row = lambda i: (i, 0)
    const = lambda i: (0, 0)
    out_shape = (
        jax.ShapeDtypeStruct((t, d_rnn), F32),
        jax.ShapeDtypeStruct((t, d_rnn), F32),
        jax.ShapeDtypeStruct((t, d_att), BF16),
        jax.ShapeDtypeStruct((t, d_att), BF16),
        jax.ShapeDtypeStruct((t, HEADS_PER_BLOCK * d_att), BF16),
        jax.ShapeDtypeStruct((batch, bps, SUBLANES, tm), F32),
    )
    return pl.pallas_call(
        functools.partial(_in_proj_kernel, blocks_per_seq=bps, d_rnn=d_rnn, d_att=d_att),
        grid=(t // tm,),
        in_specs=[
            pl.BlockSpec((tm, d), row),
            pl.BlockSpec((1, d), const),
            pl.BlockSpec((d, n_pad), const),
            pl.BlockSpec((d_att, d_att), const),
            pl.BlockSpec((1, d_att), const),
            pl.BlockSpec((1, d_att), const),
            pl.BlockSpec((SUBLANES, 1), const),
            pl.BlockSpec((tm, tm), const),
        ],
        out_specs=(
            pl.BlockSpec((tm, d_rnn), row),
            pl.BlockSpec((tm, d_rnn), row),
            pl.BlockSpec((tm, d_att), row),
            pl.BlockSpec((tm, d_att), row),
            pl.BlockSpec((tm, HEADS_PER_BLOCK * d_att), row),
            pl.BlockSpec((1, 1, SUBLANES, tm), lambda i: (i // bps, i % bps, 0, 0)),
        ),
        out_shape=out_shape,
        scratch_shapes=[pltpu.VMEM((SUBLANES, LANES), F32)],
        compiler_params=_params("arbitrary"),
        name="in_proj",
    )(x2, g_mix, w_pad, gsum, qg, kg, bf, tri)


def _rglru_kernel(zx_ref, zg_ref, cw_ref, cb_ref, wg_ref, br_ref, bi_ref, lam_ref, go_ref,
                  y_ref, tail_ref, h_ref, a_s, u_s, hs_s, *, d_rnn):
    j = pl.program_id(1)
    ts = zx_ref.shape[1]

    @pl.when(j == 0)
    def _():
        tail_ref[...] = jnp.zeros_like(tail_ref)
        h_ref[...] = jnp.zeros_like(h_ref)

    x = zx_ref[0]
    tail = tail_ref[...]
    row8 = lax.broadcasted_iota(jnp.int32, (SUBLANES, d_rnn), 0)

    xc = x * cw_ref[CONV_WIDTH - 1:CONV_WIDTH, :] + cb_ref[...]
    for d in range(1, CONV_WIDTH):
        xr = pltpu.roll(x, d, 0)
        top = jnp.where(row8 < d, pltpu.roll(tail, d, 0), xr[:SUBLANES])
        xs = jnp.concatenate([top, xr[SUBLANES:]], axis=0)
        xc = xc + xs * cw_ref[CONV_WIDTH - 1 - d:CONV_WIDTH - d, :]
    tail_ref[...] = x[ts - SUBLANES:, :]

    gl = jnp.dot(xc.astype(BF16), wg_ref[...], preferred_element_type=F32)
    r = _sigmoid(gl[:, :d_rnn] + br_ref[...])
    ig = _sigmoid(gl[:, d_rnn:] + bi_ref[...])
    log_a = (-LRU_C) * r * _softplus(-lam_ref[...])
    a = jnp.exp(log_a)
    a_s[...] = a
    u_s[...] = jnp.sqrt(1.0 - a * a) * (ig * xc)

    def group(gidx, h):
        r0 = pl.multiple_of(gidx * SUBLANES, SUBLANES)
        a8 = a_s[pl.ds(r0, SUBLANES), :]
        b8 = u_s[pl.ds(r0, SUBLANES), :]
        for d in (1, 2, 4):
            keep = row8 >= d
            a_sh = jnp.where(keep, pltpu.roll(a8, d, 0), 1.0)
            b_sh = jnp.where(keep, pltpu.roll(b8, d, 0), 0.0)
            b8 = a8 * b_sh + b8
            a8 = a8 * a_sh
        h8 = a8 * h + b8
        hs_s[pl.ds(r0, SUBLANES), :] = h8
        return h8[SUBLANES - 1:SUBLANES, :]

    h_last = lax.fori_loop(0, ts // SUBLANES, group, h_ref[0:1, :], unroll=4)
    h_ref[...] = jnp.broadcast_to(h_last, h_ref.shape)

    y = hs_s[...] * _gelu_tanh(zg_ref[0])
    y_ref[0] = (_rms(y) * go_ref[...]).astype(BF16)


def _rglru(zx, zg, conv_w, conv_b, w_gates, b_r, b_i, lam, g_out):
    b, s, d_rnn = zx.shape
    ts = SEQ_BLOCK
    blk = lambda bi, j: (bi, j, 0)
    const = lambda bi, j: (0, 0)
    return pl.pallas_call(
        functools.partial(_rglru_kernel, d_rnn=d_rnn),
        grid=(b, s // ts),
        in_specs=[
            pl.BlockSpec((1, ts, d_rnn), blk),
            pl.BlockSpec((1, ts, d_rnn), blk),
            pl.BlockSpec((CONV_WIDTH, d_rnn), const),
            pl.BlockSpec((1, d_rnn), const),
            pl.BlockSpec((d_rnn, 2 * d_rnn), const),
            pl.BlockSpec((1, d_rnn), const),
            pl.BlockSpec((1, d_rnn), const),
            pl.BlockSpec((1, d_rnn), const),
            pl.BlockSpec((1, d_rnn), const),
        ],
        out_specs=pl.BlockSpec((1, ts, d_rnn), blk),
        out_shape=jax.ShapeDtypeStruct((b, s, d_rnn), BF16),
        scratch_shapes=[
            pltpu.VMEM((SUBLANES, d_rnn), F32),
            pltpu.VMEM((SUBLANES, d_rnn), F32),
            pltpu.VMEM((ts, d_rnn), F32),
            pltpu.VMEM((ts, d_rnn), F32),
            pltpu.VMEM((ts, d_rnn), F32),
        ],
        compiler_params=_params("arbitrary", "arbitrary"),
        name="rglru",
    )(zx, zg, conv_w, conv_b, w_gates, b_r, b_i, lam, g_out)


def _attn_kernel(q_ref, k_ref, v_ref, c_ref, o_ref, sa_ref, sb_ref, *, bq, bk):
    hp = pl.program_id(1)
    qi = pl.program_id(2)
    q2 = q_ref[0]
    lane = lax.broadcasted_iota(jnp.int32, (bq, LANES), 1)
    row = qi * bq + lax.broadcasted_iota(jnp.int32, (bq, bk), 0)
    col0 = lax.broadcasted_iota(jnp.int32, (bq, bk), 1)
    n_full = qi * (bq // bk)
    qs = [jnp.where(lane // HEAD_DIM == j, q2, jnp.zeros_like(q2)) for j in range(HEADS_PER_BLOCK)]

    def scores(kb, s_ref):
        k0 = pl.multiple_of(kb * bk, bk)
        kblk = k_ref[0, pl.ds(k0, bk), :]
        for j in range(HEADS_PER_BLOCK):
            s = lax.dot_general(qs[j], kblk, (((1,), (1,)), ((), ())), preferred_element_type=F32)
            s_ref[j] = s - c_ref[0, kb, pl.ds(hp * HEADS_PER_BLOCK + j, 1), :]

    def update(kb, s_ref, stats, masked):
        k0 = pl.multiple_of(kb * bk, bk)
        new = []
        for j in range(HEADS_PER_BLOCK):
            m, acc = stats[j]
            s = s_ref[j]
            if masked:
                s = jnp.where(col0 + k0 <= row, s, -jnp.inf)
            m_new = jnp.maximum(m, jnp.max(s, axis=-1, keepdims=True))
            p = jnp.exp2(s - m_new)
            vj = v_ref[0, pl.ds(k0, bk), j * LANES:(j + 1) * LANES]
            acc = jnp.exp2(m - m_new) * acc + jnp.dot(p.astype(BF16), vj, preferred_element_type=F32)
            new.append((m_new, acc))
        return tuple(new)

    def finish(stats):
        o = None
        for j in range(HEADS_PER_BLOCK):
            acc = stats[j][1]
            oj = acc / pltpu.roll(acc, HEAD_DIM, 1)
            o = oj if o is None else jnp.where(lane // HEAD_DIM == j, oj, o)
        o_ref[0] = o.astype(o_ref.dtype)

    def pair(i, stats):
        scores(2 * i + 1, sb_ref)
        stats = update(2 * i, sa_ref, stats, False)
        scores(2 * i + 2, sa_ref)
        return update(2 * i + 1, sb_ref, stats, False)

    stats = tuple((jnp.full((bq, 1), -jnp.inf, F32), jnp.zeros((bq, LANES), F32)) for _ in range(HEADS_PER_BLOCK))
    scores(0, sa_ref)
    stats = lax.fori_loop(0, n_full // 2, pair, stats)

    @pl.when(n_full % 2 == 0)
    def _():
        finish(update(n_full, sa_ref, stats, True))

    @pl.when(n_full % 2 == 1)
    def _():
        scores(n_full, sb_ref)
        st = update(n_full - 1, sa_ref, stats, False)
        finish(update(n_full, sb_ref, st, True))


def _attention(q, k, v, c):
    b, s, d_att = q.shape
    bq, bk = Q_BLOCK, SEQ_BLOCK
    return pl.pallas_call(
        functools.partial(_attn_kernel, bq=bq, bk=bk),
        grid=(b, d_att // LANES, s // bq),
        in_specs=[
            pl.BlockSpec((1, bq, LANES), lambda bi, hp, qi: (bi, qi, hp)),
            pl.BlockSpec((1, s, LANES), lambda bi, hp, qi: (bi, 0, hp)),
            pl.BlockSpec((1, s, HEADS_PER_BLOCK * LANES), lambda bi, hp, qi: (bi, 0, hp)),
            pl.BlockSpec((1, s // bk, SUBLANES, bk), lambda bi, hp, qi: (bi, 0, 0, 0)),
        ],
        out_specs=pl.BlockSpec((1, bq, LANES), lambda bi, hp, qi: (bi, qi, hp)),
        out_shape=jax.ShapeDtypeStruct((b, s, d_att), BF16),
        scratch_shapes=[pltpu.VMEM((HEADS_PER_BLOCK, bq, bk), F32), pltpu.VMEM((HEADS_PER_BLOCK, bq, bk), F32)],
        compiler_params=_params("arbitrary", "arbitrary", "arbitrary"),
        name="attention",
    )(q, k, v, c)


def _out_route_kernel(x_ref, yr_ref, ya_ref, ga_ref, wo_ref, gf_ref, wrh_ref, wrl_ref, br_ref, ltri_ref,
                      x1_ref, xn_ref, route_ref, meta_ref, cnt_ref, carry_ref, *, d_rnn):
    i = pl.program_id(0)
    tm = x_ref.shape[0]

    @pl.when(i == 0)
    def _():
        carry_ref[...] = jnp.zeros_like(carry_ref)

    ya = _rms(ya_ref[...].astype(F32)) * ga_ref[...]
    x1 = (x_ref[...]
          + jnp.dot(yr_ref[...], wo_ref[:d_rnn, :], preferred_element_type=F32)
          + jnp.dot(ya.astype(BF16), wo_ref[d_rnn:, :], preferred_element_type=F32))
    x1_ref[...] = x1
    xn = _rms(x1) * gf_ref[...]
    xn_ref[...] = xn

    nh, nl = _split_bf16(xn)
    logits = (jnp.dot(nh, wrh_ref[...], preferred_element_type=F32)
              + jnp.dot(nl, wrh_ref[...], preferred_element_type=F32)
              + jnp.dot(nh, wrl_ref[...], preferred_element_type=F32)) + br_ref[...]

    lane = lax.broadcasted_iota(jnp.int32, (tm, LANES), 1)
    lane_f = lane.astype(F32)
    big = float(LANES)

    def top1(vals):
        m = jnp.max(vals, axis=-1, keepdims=True)
        idx = jnp.min(jnp.where(vals == m, lane_f, big), axis=-1, keepdims=True)
        return m, idx

    is_group = (lane >= N_EXPERTS) & (lane < N_EXPERTS + N_GROUPS)
    gl = jnp.where(is_group, logits, -jnp.inf)
    g_max, g_lane = top1(gl)
    g_p = 1.0 / jnp.sum(jnp.exp(gl - g_max), axis=-1, keepdims=True)
    g_idx = g_lane - float(N_EXPERTS)

    in_group = (lane < N_EXPERTS) & ((lane // EXPERTS_PER_GROUP).astype(F32) == g_idx)
    el = jnp.where(in_group, logits, -jnp.inf)
    m1, i1 = top1(el)
    el2 = jnp.where(lane_f == i1, -jnp.inf, el)
    m2, i2 = top1(el2)
    e2 = jnp.exp(m2 - m1)
    w1 = g_p / (1.0 + e2)
    w2 = g_p * e2 / (1.0 + e2)

    onehot = jnp.where((lane_f == i1) | (lane_f == i2), 1.0, 0.0)
    before = jnp.dot(ltri_ref[...], onehot.astype(BF16), preferred_element_type=F32) + carry_ref[0:1, :]
    rank1 = jnp.sum(jnp.where(lane_f == i1, before, 0.0), axis=-1, keepdims=True)
    rank2 = jnp.sum(jnp.where(lane_f == i2, before, 0.0), axis=-1, keepdims=True)
    carry = carry_ref[0:1, :] + jnp.sum(onehot, axis=0, keepdims=True)
    carry_ref[...] = jnp.broadcast_to(carry, carry_ref.shape)
    cnt_ref[...] = jnp.broadcast_to(carry, cnt_ref.shape)

    rec = jnp.zeros((tm, LANES), F32)
    for ln, val in ((R_ID0, i1), (R_ID1, i2), (R_W0, w1), (R_W1, w2), (R_RANK0, rank1), (R_RANK1, rank2)):
        rec = jnp.where(lane == ln, val, rec)
    route_ref[...] = rec
    meta_ref[...] = rec.T[:SUBLANES, :]


def _out_route(x2, yr, ya, g_att, w_out, g_ffn, wr_hi, wr_lo, b_rt, ltri, *, d_rnn):
    t, d = x2.shape
    tm = SEQ_BLOCK
    d_att = ya.shape[1]
    row = lambda i: (i, 0)
    const = lambda i: (0, 0)
    return pl.pallas_call(
        functools.partial(_out_route_kernel, d_rnn=d_rnn),
        grid=(t // tm,),
        in_specs=[
            pl.BlockSpec((tm, d), row),
            pl.BlockSpec((tm, d_rnn), row),
            pl.BlockSpec((tm, d_att), row),
            pl.BlockSpec((1, d_att), const),
            pl.BlockSpec((d_rnn + d_att, d), const),
            pl.BlockSpec((1, d), const),
            pl.BlockSpec((d, LANES), const),
            pl.BlockSpec((d, LANES), const),
            pl.BlockSpec((1, LANES), const),
            pl.BlockSpec((tm, tm), const),
        ],
        out_specs=(
            pl.BlockSpec((tm, d), row),
            pl.BlockSpec((tm, d), row),
            pl.BlockSpec((tm, LANES), row),
            pl.BlockSpec((SUBLANES, tm), lambda i: (0, i)),
            pl.BlockSpec((SUBLANES, LANES), const),
        ),
        out_shape=(
            jax.ShapeDtypeStruct((t, d), F32),
            jax.ShapeDtypeStruct((t, d), F32),
            jax.ShapeDtypeStruct((t, LANES), F32),
            jax.ShapeDtypeStruct((SUBLANES, t), F32),
            jax.ShapeDtypeStruct((SUBLANES, LANES), F32),
        ),
        scratch_shapes=[pltpu.VMEM((SUBLANES, LANES), F32)],
        compiler_params=_params("arbitrary"),
        name="out_route",
    )(x2, yr, ya, g_att, w_out, g_ffn, wr_hi, wr_lo, b_rt, ltri)


def _positions_kernel(off_ref, meta_ref, pos_ref):
    m = meta_ref[...]
    ids = m[R_ID0:R_ID1 + 1, :]
    acc = m[R_RANK0:R_RANK1 + 1, :]
    for e in range(N_EXPERTS):
        acc = acc + jnp.where(ids == float(e), off_ref[e].astype(F32), 0.0)
    pos_ref[...] = acc.astype(jnp.int32)


def _positions(off, meta):
    t = meta.shape[1]
    return pl.pallas_call(
        _positions_kernel,
        grid_spec=pltpu.PrefetchScalarGridSpec(
            num_scalar_prefetch=1,
            grid=(1,),
            in_specs=[pl.BlockSpec((SUBLANES, t), lambda i, off_ref: (0, 0))],
            out_specs=pl.BlockSpec((TOP_K, t), lambda i, off_ref: (0, 0)),
        ),
        out_shape=jax.ShapeDtypeStruct((TOP_K, t), jnp.int32),
        compiler_params=_params("arbitrary"),
        name="positions",
    )(off, meta)


def _row_copy(src_ref, src_row, dst_ref, dst_row, sem):
    return pltpu.make_async_copy(src_ref.at[pl.ds(src_row, 1), :], dst_ref.at[pl.ds(dst_row, 1), :], sem)


def _dispatch_kernel(pos_ref, xn_ref, xs_ref, sem, *, n_tok):
    tm = xn_ref.shape[0]
    base = pl.program_id(0) * tm

    def issue(r, carry):
        for k in range(TOP_K):
            _row_copy(xn_ref, r, xs_ref, pos_ref[k * n_tok + base + r], sem).start()
        return carry

    def drain(r, carry):
        for k in range(TOP_K):
            _row_copy(xn_ref, 0, xs_ref, 0, sem).wait()
        return carry

    lax.fori_loop(0, tm, issue, 0, unroll=DMA_UNROLL)
    lax.fori_loop(0, tm, drain, 0, unroll=DMA_UNROLL)


def _dispatch(pos, xn):
    t, d = xn.shape
    tm = DISPATCH_BLOCK
    return pl.pallas_call(
        functools.partial(_dispatch_kernel, n_tok=t),
        grid_spec=pltpu.PrefetchScalarGridSpec(
            num_scalar_prefetch=1,
            grid=(t // tm,),
            in_specs=[pl.BlockSpec((tm, d), lambda i, pos_ref: (i, 0))],
            out_specs=pl.BlockSpec(memory_space=pl.ANY),
            scratch_shapes=[pltpu.SemaphoreType.DMA(())],
        ),
        out_shape=jax.ShapeDtypeStruct((TOP_K * t, d), F32),
        compiler_params=_params("arbitrary"),
        name="dispatch",
    )(pos, xn)


def _ffn_kernel(tile_ref, exp_ref, lo_ref, hi_ref, first_ref, init_ref,
                xs_ref, wg_ref, wu_ref, wd_ref, ys_ref, wg_s, wu_s, wd_s):
    g = pl.program_id(0)
    tm = xs_ref.shape[0]

    @pl.when(first_ref[g] == 1)
    def _():
        wg_s[...] = wg_ref[...].astype(BF16)
        wu_s[...] = wu_ref[...].astype(BF16)
        wd_s[...] = wd_ref[...].astype(BF16)

    lo = lo_ref[g]
    hi = hi_ref[g]

    @pl.when(hi > lo)
    def _():
        x = xs_ref[...].astype(BF16)
        hg = jnp.dot(x, wg_s[...], preferred_element_type=F32)
        hu = jnp.dot(x, wu_s[...], preferred_element_type=F32)
        h = (hg * _sigmoid(hg)) * hu
        y = jnp.dot(h.astype(BF16), wd_s[...], preferred_element_type=F32)
        row = lax.broadcasted_iota(jnp.int32, y.shape, 0)
        mine = (row >= lo) & (row < hi)

        @pl.when(init_ref[g] == 1)
        def _():
            ys_ref[...] = jnp.where(mine, y, 0.0)

        @pl.when(init_ref[g] == 0)
        def _():
            ys_ref[...] = jnp.where(mine, y, ys_ref[...])


def _expert_ffn(meta, xs, w_gate, w_up, w_down, layer):
    rows, d = xs.shape
    d_e = w_gate.shape[-1]
    tm = FFN_TILE
    steps = meta[0].shape[0]
    x_map = lambda g, tile, exp, lo, hi, first, init: (tile[g], 0)
    w_map = lambda g, tile, exp, lo, hi, first, init: (layer, exp[g], 0, 0)
    return pl.pallas_call(
        _ffn_kernel,
        grid_spec=pltpu.PrefetchScalarGridSpec(
            num_scalar_prefetch=6,
            grid=(steps,),
            in_specs=[
                pl.BlockSpec((tm, d), x_map),
                pl.BlockSpec((None, None, d, d_e), w_map),
                pl.BlockSpec((None, None, d, d_e), w_map),
                pl.BlockSpec((None, None, d_e, d), w_map),
            ],
            out_specs=pl.BlockSpec((tm, d), x_map),
            scratch_shapes=[
                pltpu.VMEM((d, d_e), BF16),
                pltpu.VMEM((d, d_e), BF16),
                pltpu.VMEM((d_e, d), BF16),
            ],
        ),
        out_shape=jax.ShapeDtypeStruct((rows, d), F32),
        compiler_params=_params("arbitrary"),
        name="expert_ffn",
    )(*meta, xs, w_gate, w_up, w_down)


def _ffn_schedule(cnt, off, rows):
    tm = FFN_TILE
    n_tiles = rows // tm
    steps = n_tiles + N_EXPERTS - 1
    end = off + cnt
    first_tile = off // tm
    last_tile = jnp.where(cnt > 0, (end - 1) // tm, first_tile - 1)
    visits = last_tile - first_tile + 1
    vis_end = jnp.cumsum(visits)
    vis_start = vis_end - visits
    total = vis_end[-1]
    g = jnp.arange(steps, dtype=jnp.int32)
    gc = jnp.minimum(g, total - 1)
    exp = jnp.searchsorted(vis_end, gc, side="right").astype(jnp.int32)
    tile = first_tile[exp] + (gc - vis_start[exp])
    valid = g < total
    lo = jnp.where(valid, jnp.maximum(off[exp], tile * tm) - tile * tm, 0)
    hi = jnp.where(valid, jnp.minimum(end[exp], (tile + 1) * tm) - tile * tm, 0)
    prev_exp = jnp.concatenate([jnp.full((1,), -1, jnp.int32), exp[:-1]])
    prev_tile = jnp.concatenate([jnp.full((1,), -1, jnp.int32), tile[:-1]])
    first = (valid & (exp != prev_exp)).astype(jnp.int32)
    init = (valid & (tile != prev_tile)).astype(jnp.int32)
    i32 = lambda a: a.astype(jnp.int32)
    return i32(tile), i32(exp), i32(lo), i32(hi), first, init


def _combine_kernel(pos_ref, x1_ref, route_ref, p_ref, ys_ref, gp_ref, wg_ref, bg_ref, wp_ref,
                    o_ref, buf, sem, *, n_tok):
    tm = x1_ref.shape[0]
    i = pl.program_id(0)
    last = pl.num_programs(0) - 1
    slot = i % 2

    def start_row(block, slot_, r):
        for k in range(TOP_K):
            src = pos_ref[k * n_tok + block * tm + r]
            _row_copy(ys_ref, src, buf.at[slot_, k], r, sem.at[slot_]).start()

    def drain(slot_):
        def body(r, carry):
            for k in range(TOP_K):
                _row_copy(ys_ref, 0, buf.at[slot_, k], 0, sem.at[slot_]).wait()
            return carry
        lax.fori_loop(0, tm, body, 0, unroll=DMA_UNROLL)

    @pl.when(i == 0)
    def _():
        def body(r, carry):
            start_row(0, 0, r)
            return carry
        lax.fori_loop(0, tm, body, 0, unroll=DMA_UNROLL)

    drain(slot)
    for r in range(tm):
        start_row(jnp.minimum(i + 1, last), 1 - slot, r)

    rec = route_ref[...]
    w0 = rec[:, R_W0:R_W0 + 1]
    w1 = rec[:, R_W1:R_W1 + 1]
    x2 = x1_ref[...] + (buf[slot, 0] * w0 + buf[slot, 1] * w1)
    n = _rms(x2) * gp_ref[...]
    gate = _sigmoid(jnp.dot(n.astype(BF16), wg_ref[...], preferred_element_type=F32) + bg_ref[...])
    pe = jnp.dot(p_ref[...].astype(BF16), wp_ref[...], preferred_element_type=F32)
    o_ref[...] = x2 + pe * gate

    @pl.when(i == last)
    def _():
        drain(1 - slot)


def _combine_ple(pos, x1, route, p3, ys, g_ple, w_gate, b_gate, w_proj, layer):
    t, d = x1.shape
    dp = p3.shape[-1]
    tm = COMBINE_BLOCK
    row = lambda i, pos_ref: (i, 0)
    const = lambda i, pos_ref: (0, 0)
    return pl.pallas_call(
        functools.partial(_combine_kernel, n_tok=t),
        grid_spec=pltpu.PrefetchScalarGridSpec(
            num_scalar_prefetch=1,
            grid=(t // tm,),
            in_specs=[
                pl.BlockSpec((tm, d), row),
                pl.BlockSpec((tm, LANES), row),
                pl.BlockSpec((None, tm, dp), lambda i, pos_ref: (layer, i, 0)),
                pl.BlockSpec(memory_space=pl.ANY),
                pl.BlockSpec((1, d), const),
                pl.BlockSpec((d, d), const),
                pl.BlockSpec((1, d), const),
                pl.BlockSpec((dp, d), const),
            ],
            out_specs=pl.BlockSpec((tm, d), row),
            scratch_shapes=[
                pltpu.VMEM((2, TOP_K, tm, d), F32),
                pltpu.SemaphoreType.DMA((2,)),
            ],
        ),
        out_shape=jax.ShapeDtypeStruct((t, d), F32),
        compiler_params=_params("arbitrary"),
        name="combine_ple",
    )(pos, x1, route, p3, ys, g_ple, w_gate, b_gate, w_proj)


def _block_diag(w):
    nb, n, _ = w.shape
    eye = jnp.eye(nb, dtype=w.dtype)
    return jnp.einsum("hij,hg->higj", w, eye).reshape(nb * n, nb * n)


def kernel(x, p, mix_norm, w_in, b_forget, conv_w, conv_b, w_rgate, b_rgate, w_igate, b_igate, lru_lambda,
           q_norm, k_norm, lru_out_norm, att_out_norm, w_out, ffn_norm, w_group, b_group, w_router, b_router,
           w_exp_gate, w_exp_up, w_exp_down, ple_norm, w_ple_gate, b_ple_gate, w_ple_proj):
    batch, seq, d = x.shape
    depth = w_in.shape[0]
    d_rnn = conv_w.shape[-1]
    n_heads = b_forget.shape[-1]
    d_att = n_heads * HEAD_DIM
    t = batch * seq
    assert n_heads == SUBLANES and d_att % LANES == 0 and seq % SEQ_BLOCK == 0 and seq % Q_BLOCK == 0
    assert HEADS_PER_BLOCK == 2
    assert Q_BLOCK == SEQ_BLOCK and (TOP_K * t) % FFN_TILE == 0
    assert t % DISPATCH_BLOCK == 0 and t % COMBINE_BLOCK == 0
    assert w_router.shape[-1] == N_EXPERTS and w_group.shape[-1] == N_GROUPS

    idx = jnp.arange(SEQ_BLOCK)
    tri_incl = (idx[:, None] <= idx[None, :]).astype(BF16)
    tri_strict = (idx[None, :] < idx[:, None]).astype(BF16)
    hid = jnp.arange(d_att) // HEAD_DIM
    gsum = (hid[:, None] == hid[None, :]).astype(BF16)
    row2 = lambda a: a.reshape(1, -1).astype(F32)
    n_in_pad = 2 * d_rnn + 3 * d_att + LANES
    p3 = p.reshape(depth, t, p.shape[-1])

    x2 = x.reshape(t, d)
    for i in range(depth):
        w_pad = jnp.pad(w_in[i], ((0, 0), (0, n_in_pad - w_in.shape[-1]))).astype(BF16)
        qg = row2(jnp.tile(q_norm[i], n_heads) * (LOG2E / math.sqrt(HEAD_DIM)))
        kg = row2(jnp.tile(k_norm[i], n_heads))
        zx, zg, q, k, v, c = _in_proj(
            x2, row2(mix_norm[i]), w_pad, gsum, qg, kg, b_forget[i].reshape(n_heads, 1).astype(F32), tri_incl,
            batch=batch, seq=seq, d_rnn=d_rnn, d_att=d_att)

        w_gates = jnp.concatenate([_block_diag(w_rgate[i]), _block_diag(w_igate[i])], axis=1).astype(BF16)
        yr = _rglru(zx.reshape(batch, seq, d_rnn), zg.reshape(batch, seq, d_rnn), conv_w[i].astype(F32),
                    row2(conv_b[i]), w_gates, row2(b_rgate[i]), row2(b_igate[i]), row2(lru_lambda[i]),
                    row2(lru_out_norm[i]))
        ya = _attention(q.reshape(batch, seq, d_att), k.reshape(batch, seq, d_att),
                        v.reshape(batch, seq, HEADS_PER_BLOCK * d_att), c)

        w_rt = jnp.pad(jnp.concatenate([w_router[i], w_group[i]], axis=1),
                       ((0, 0), (0, LANES - N_EXPERTS - N_GROUPS))).astype(F32)
        wr_hi, wr_lo = _split_bf16(w_rt)
        b_rt = jnp.pad(jnp.concatenate([b_router[i], b_group[i]]), (0, LANES - N_EXPERTS - N_GROUPS))
        x1, xn, route, meta, cnt = _out_route(
            x2, yr.reshape(t, d_rnn), ya.reshape(t, d_att), row2(att_out_norm[i]), w_out[i].astype(BF16),
            row2(ffn_norm[i]), wr_hi, wr_lo, row2(b_rt), tri_strict, d_rnn=d_rnn)

        cnt_e = cnt[0, :N_EXPERTS].astype(jnp.int32)
        off_e = jnp.cumsum(cnt_e) - cnt_e
        pos = _positions(off_e, meta).reshape(-1)

        xs = _dispatch(pos, xn)
        ys = _expert_ffn(_ffn_schedule(cnt_e, off_e, TOP_K * t), xs, w_exp_gate, w_exp_up, w_exp_down, i)
        x2 = _combine_ple(pos, x1, route, p3, ys, row2(ple_norm[i]), w_ple_gate[i].astype(BF16),
                          row2(b_ple_gate[i]), w_ple_proj[i].astype(BF16), i)
    return x2.reshape(batch, seq, d)
```

```python
import functools
import math

import jax
import jax.numpy as jnp
from jax import lax
from jax.experimental import pallas as pl
from jax.experimental.pallas import tpu as pltpu

F32 = jnp.float32
BF16 = jnp.bfloat16

EPS = 1e-6
LRU_C = 8.0
HEAD_DIM = 64
CONV_WIDTH = 4
N_GROUPS = 4
EXPERTS_PER_GROUP = 8
N_EXPERTS = N_GROUPS * EXPERTS_PER_GROUP
TOP_K = 2

LOG2E = math.log2(math.e)

LANES = 128
SUBLANES = 8
VMEM_LIMIT = 56 * 1024 * 1024
HEADS_PER_BLOCK = LANES // HEAD_DIM

SEQ_BLOCK = 512
Q_BLOCK = 512
FFN_TILE = 256
DISPATCH_BLOCK = 1024
COMBINE_BLOCK = 256
DMA_UNROLL = 8

R_ID0, R_ID1, R_W0, R_W1, R_RANK0, R_RANK1 = 0, 1, 2, 3, 4, 5


def _sigmoid(x):
    return 1.0 / (1.0 + jnp.exp(-x))


def _softplus(x):
    return jnp.maximum(x, 0.0) + jnp.log1p(jnp.exp(-jnp.abs(x)))


def _gelu_tanh(x):
    c = math.sqrt(2.0 / math.pi)
    return x * (0.5 * (1.0 + jnp.tanh(c * (x + 0.044715 * (x * x * x)))))


def _rms(x):
    return x * lax.rsqrt(jnp.mean(x * x, axis=-1, keepdims=True) + EPS)


def _split_bf16(x):
    hi = x.astype(BF16)
    lo = (x - hi.astype(F32)).astype(BF16)
    return hi, lo


def _store_token_tiles(ref, x):
    n, d = x.shape
    chunks = d // LANES
    for j in range(chunks):
        ref[pl.ds(j, n, stride=chunks), :] = x[:, j * LANES:(j + 1) * LANES]


def _load_token_tiles(ref, n, chunks):
    return jnp.concatenate([ref[pl.ds(j, n, stride=chunks), :] for j in range(chunks)], axis=1)


def _token_copy(src_ref, src_tok, dst_ref, dst_tok, sem, chunks):
    def rows(tok):
        start = tok * chunks
        return pl.ds(start if isinstance(tok, int) else pl.multiple_of(start, chunks), chunks)
    return pltpu.make_async_copy(src_ref.at[rows(src_tok), :], dst_ref.at[rows(dst_tok), :], sem)


def _params(*sem):
    return pltpu.CompilerParams(dimension_semantics=sem, vmem_limit_bytes=VMEM_LIMIT)


def _in_proj_kernel(x_ref, g_ref, w_ref, gsum_ref, qg_ref, kg_ref, bf_ref, tri_ref,
                    zx_ref, zg_ref, q_ref, k_ref, v_ref, c_ref, carry_ref, *, blocks_per_seq, d_rnn, d_att):
    i = pl.program_id(0)
    n = _rms(x_ref[...]) * g_ref[...]
    z = jnp.dot(n.astype(BF16), w_ref[...], preferred_element_type=F32)
    o = 0
    zx_ref[...] = z[:, o:o + d_rnn]; o += d_rnn
    zg_ref[...] = z[:, o:o + d_rnn]; o += d_rnn
    zq = z[:, o:o + d_att]; o += d_att
    zk = z[:, o:o + d_att]; o += d_att
    zv = z[:, o:o + d_att]; o += d_att
    zf = z[:, o:o + LANES]

    lane = lax.broadcasted_iota(jnp.int32, (zv.shape[0], LANES), 1)
    v_parts = []
    for blk in range(d_att // LANES):
        vb = zv[:, blk * LANES:(blk + 1) * LANES]
        v_parts += [jnp.where(lane // HEAD_DIM == j, vb, 1.0) for j in range(HEADS_PER_BLOCK)]
    v_ref[...] = jnp.concatenate(v_parts, axis=1).astype(BF16)

    def head_norm(t, gain):
        ss = jnp.dot((t * t).astype(BF16), gsum_ref[...], preferred_element_type=F32)
        return t * lax.rsqrt(ss * (1.0 / HEAD_DIM) + EPS) * gain

    q_ref[...] = head_norm(zq, qg_ref[...]).astype(BF16)
    k_ref[...] = head_norm(zk, kg_ref[...]).astype(BF16)

    zft = zf.T[:SUBLANES, :] + bf_ref[...]
    logf = jnp.minimum(zft, 0.0) - jnp.log1p(jnp.exp(-jnp.abs(zft)))
    hi, lo = _split_bf16(logf)
    cs = (jnp.dot(hi, tri_ref[...], preferred_element_type=F32)
          + jnp.dot(lo, tri_ref[...], preferred_element_type=F32))

    @pl.when(i % blocks_per_seq == 0)
    def _():
        carry_ref[...] = jnp.zeros_like(carry_ref)

    c = cs + carry_ref[:, 0:1]
    c_ref[0, 0] = c * LOG2E
    carry_ref[...] = jnp.broadcast_to(c[:, SEQ_BLOCK - 1:SEQ_BLOCK], carry_ref.shape)


def _in_proj(x2, g_mix, w_pad, gsum, qg, kg, bf, tri, *, batch, seq, d_rnn, d_att):
    t, d = x2.shape
    tm = SEQ_BLOCK
    bps = seq // tm
    n_pad = w_pad.shape[1]
    row = lambda i: (i, 0)
    const = lambda i: (0, 0)
    out_shape = (
        jax.ShapeDtypeStruct((t, d_rnn), F32),
        jax.ShapeDtypeStruct((t, d_rnn), F32),
        jax.ShapeDtypeStruct((t, d_att), BF16),
        jax.ShapeDtypeStruct((t, d_att), BF16),
        jax.ShapeDtypeStruct((t, HEADS_PER_BLOCK * d_att), BF16),
        jax.ShapeDtypeStruct((batch, bps, SUBLANES, tm), F32),
    )
    return pl.pallas_call(
        functools.partial(_in_proj_kernel, blocks_per_seq=bps, d_rnn=d_rnn, d_att=d_att),
        grid=(t // tm,),
        in_specs=[
            pl.BlockSpec((tm, d), row),
            pl.BlockSpec((1, d), const),
            pl.BlockSpec((d, n_pad), const),
            pl.BlockSpec((d_att, d_att), const),
            pl.BlockSpec((1, d_att), const),
            pl.BlockSpec((1, d_att), const),
            pl.BlockSpec((SUBLANES, 1), const),
            pl.BlockSpec((tm, tm), const),
        ],
        out_specs=(
            pl.BlockSpec((tm, d_rnn), row),
            pl.BlockSpec((tm, d_rnn), row),
            pl.BlockSpec((tm, d_att), row),
            pl.BlockSpec((tm, d_att), row),
            pl.BlockSpec((tm, HEADS_PER_BLOCK * d_att), row),
            pl.BlockSpec((1, 1, SUBLANES, tm), lambda i: (i // bps, i % bps, 0, 0)),
        ),
        out_shape=out_shape,
        scratch_shapes=[pltpu.VMEM((SUBLANES, LANES), F32)],
        compiler_params=_params("arbitrary"),
        name="in_proj",
    )(x2, g_mix, w_pad, gsum, qg, kg, bf, tri)


def _rglru_kernel(zx_ref, zg_ref, cw_ref, cb_ref, wg_ref, br_ref, bi_ref, lam_ref, go_ref,
                  y_ref, tail_ref, h_ref, a_s, u_s, hs_s, *, d_rnn):
    j = pl.program_id(1)
    ts = zx_ref.shape[1]

    @pl.when(j == 0)
    def _():
        tail_ref[...] = jnp.zeros_like(tail_ref)
        h_ref[...] = jnp.zeros_like(h_ref)

    x = zx_ref[0]
    tail = tail_ref[...]
    row8 = lax.broadcasted_iota(jnp.int32, (SUBLANES, d_rnn), 0)

    xc = x * cw_ref[CONV_WIDTH - 1:CONV_WIDTH, :] + cb_ref[...]
    for d in range(1, CONV_WIDTH):
        xr = pltpu.roll(x, d, 0)
        top = jnp.where(row8 < d, pltpu.roll(tail, d, 0), xr[:SUBLANES])
        xs = jnp.concatenate([top, xr[SUBLANES:]], axis=0)
        xc = xc + xs * cw_ref[CONV_WIDTH - 1 - d:CONV_WIDTH - d, :]
    tail_ref[...] = x[ts - SUBLANES:, :]

    gl = jnp.dot(xc.astype(BF16), wg_ref[...], preferred_element_type=F32)
    r = _sigmoid(gl[:, :d_rnn] + br_ref[...])
    ig = _sigmoid(gl[:, d_rnn:] + bi_ref[...])
    log_a = (-LRU_C) * r * _softplus(-lam_ref[...])
    a = jnp.exp(log_a)
    a_s[...] = a
    u_s[...] = jnp.sqrt(1.0 - a * a) * (ig * xc)

    def group(gidx, h):
        r0 = pl.multiple_of(gidx * SUBLANES, SUBLANES)
        a8 = a_s[pl.ds(r0, SUBLANES), :]
        b8 = u_s[pl.ds(r0, SUBLANES), :]
        for d in (1, 2, 4):
            keep = row8 >= d
            a_sh = jnp.where(keep, pltpu.roll(a8, d, 0), 1.0)
            b_sh = jnp.where(keep, pltpu.roll(b8, d, 0), 0.0)
            b8 = a8 * b_sh + b8
            a8 = a8 * a_sh
        h8 = a8 * h + b8
        hs_s[pl.ds(r0, SUBLANES), :] = h8
        return h8[SUBLANES - 1:SUBLANES, :]

    h_last = lax.fori_loop(0, ts // SUBLANES, group, h_ref[0:1, :], unroll=4)
    h_ref[...] = jnp.broadcast_to(h_last, h_ref.shape)

    y = hs_s[...] * _gelu_tanh(zg_ref[0])
    y_ref[0] = (_rms(y) * go_ref[...]).astype(BF16)


def _rglru(zx, zg, conv_w, conv_b, w_gates, b_r, b_i, lam, g_out):
    b, s, d_rnn = zx.shape
    ts = SEQ_BLOCK
    blk = lambda bi, j: (bi, j, 0)
    const = lambda bi, j: (0, 0)
    return pl.pallas_call(
        functools.partial(_rglru_kernel, d_rnn=d_rnn),
        grid=(b, s // ts),
        in_specs=[
            pl.BlockSpec((1, ts, d_rnn), blk),
            pl.BlockSpec((1, ts, d_rnn), blk),
            pl.BlockSpec((CONV_WIDTH, d_rnn), const),
            pl.BlockSpec((1, d_rnn), const),
            pl.BlockSpec((d_rnn, 2 * d_rnn), const),
            pl.BlockSpec((1, d_rnn), const),
            pl.BlockSpec((1, d_rnn), const),
            pl.BlockSpec((1, d_rnn), const),
            pl.BlockSpec((1, d_rnn), const),
        ],
        out_specs=pl.BlockSpec((1, ts, d_rnn), blk),
        out_shape=jax.ShapeDtypeStruct((b, s, d_rnn), BF16),
        scratch_shapes=[
            pltpu.VMEM((SUBLANES, d_rnn), F32),
            pltpu.VMEM((SUBLANES, d_rnn), F32),
            pltpu.VMEM((ts, d_rnn), F32),
            pltpu.VMEM((ts, d_rnn), F32),
            pltpu.VMEM((ts, d_rnn), F32),
        ],
        compiler_params=_params("arbitrary", "arbitrary"),
        name="rglru",
    )(zx, zg, conv_w, conv_b, w_gates, b_r, b_i, lam, g_out)


def _attn_kernel(q_ref, k_ref, v_ref, c_ref, o_ref, sa_ref, sb_ref, *, bq, bk):
    hp = pl.program_id(1)
    qi = pl.program_id(2)
    q2 = q_ref[0]
    lane = lax.broadcasted_iota(jnp.int32, (bq, LANES), 1)
    row = qi * bq + lax.broadcasted_iota(jnp.int32, (bq, bk), 0)
    col0 = lax.broadcasted_iota(jnp.int32, (bq, bk), 1)
    n_full = qi * (bq // bk)
    qs = [jnp.where(lane // HEAD_DIM == j, q2, jnp.zeros_like(q2)) for j in range(HEADS_PER_BLOCK)]

    def scores(kb, s_ref):
        k0 = pl.multiple_of(kb * bk, bk)
        kblk = k_ref[0, pl.ds(k0, bk), :]
        for j in range(HEADS_PER_BLOCK):
            s = lax.dot_general(qs[j], kblk, (((1,), (1,)), ((), ())), preferred_element_type=F32)
            s_ref[j] = s - c_ref[0, kb, pl.ds(hp * HEADS_PER_BLOCK + j, 1), :]

    def update(kb, s_ref, stats, masked):
        k0 = pl.multiple_of(kb * bk, bk)
        new = []
        for j in range(HEADS_PER_BLOCK):
            m, acc = stats[j]
            s = s_ref[j]
            if masked:
                s = jnp.where(col0 + k0 <= row, s, -jnp.inf)
            m_new = jnp.maximum(m, jnp.max(s, axis=-1, keepdims=True))
            p = jnp.exp2(s - m_new)
            vj = v_ref[0, pl.ds(k0, bk), j * LANES:(j + 1) * LANES]
            acc = jnp.exp2(m - m_new) * acc + jnp.dot(p.astype(BF16), vj, preferred_element_type=F32)
            new.append((m_new, acc))
        return tuple(new)

    def finish(stats):
        o = None
        for j in range(HEADS_PER_BLOCK):
            acc = stats[j][1]
            oj = acc / pltpu.roll(acc, HEAD_DIM, 1)
            o = oj if o is None else jnp.where(lane // HEAD_DIM == j, oj, o)
        o_ref[0] = o.astype(o_ref.dtype)

    def pair(i, stats):
        scores(2 * i + 1, sb_ref)
        stats = update(2 * i, sa_ref, stats, False)
        scores(2 * i + 2, sa_ref)
        return update(2 * i + 1, sb_ref, stats, False)

    stats = tuple((jnp.full((bq, 1), -jnp.inf, F32), jnp.zeros((bq, LANES), F32)) for _ in range(HEADS_PER_BLOCK))
    scores(0, sa_ref)
    stats = lax.fori_loop(0, n_full // 2, pair, stats)

    @pl.when(n_full % 2 == 0)
    def _():
        finish(update(n_full, sa_ref, stats, True))

    @pl.when(n_full % 2 == 1)
    def _():
        scores(n_full, sb_ref)
        st = update(n_full - 1, sa_ref, stats, False)
        finish(update(n_full, sb_ref, st, True))


def _attention(q, k, v, c):
    b, s, d_att = q.shape
    bq, bk = Q_BLOCK, SEQ_BLOCK
    return pl.pallas_call(
        functools.partial(_attn_kernel, bq=bq, bk=bk),
        grid=(b, d_att // LANES, s // bq),
        in_specs=[
            pl.BlockSpec((1, bq, LANES), lambda bi, hp, qi: (bi, qi, hp)),
            pl.BlockSpec((1, s, LANES), lambda bi, hp, qi: (bi, 0, hp)),
            pl.BlockSpec((1, s, HEADS_PER_BLOCK * LANES), lambda bi, hp, qi: (bi, 0, hp)),
            pl.BlockSpec((1, s // bk, SUBLANES, bk), lambda bi, hp, qi: (bi, 0, 0, 0)),
        ],
        out_specs=pl.BlockSpec((1, bq, LANES), lambda bi, hp, qi: (bi, qi, hp)),
        out_shape=jax.ShapeDtypeStruct((b, s, d_att), BF16),
        scratch_shapes=[pltpu.VMEM((HEADS_PER_BLOCK, bq, bk), F32), pltpu.VMEM((HEADS_PER_BLOCK, bq, bk), F32)],
        compiler_params=_params("arbitrary", "arbitrary", "arbitrary"),
        name="attention",
    )(q, k, v, c)


def _out_route_kernel(x_ref, yr_ref, ya_ref, ga_ref, wo_ref, gf_ref, wrh_ref, wrl_ref, br_ref, ltri_ref,
                      x1_ref, xn_ref, route_ref, meta_ref, cnt_ref, carry_ref, *, d_rnn):
    i = pl.program_id(0)
    tm = x_ref.shape[0]

    @pl.when(i == 0)
    def _():
        carry_ref[...] = jnp.zeros_like(carry_ref)

    ya = _rms(ya_ref[...].astype(F32)) * ga_ref[...]
    x1 = (x_ref[...]
          + jnp.dot(yr_ref[...], wo_ref[:d_rnn, :], preferred_element_type=F32)
          + jnp.dot(ya.astype(BF16), wo_ref[d_rnn:, :], preferred_element_type=F32))
    x1_ref[...] = x1
    xn = _rms(x1) * gf_ref[...]
    _store_token_tiles(xn_ref, xn)

    nh, nl = _split_bf16(xn)
    logits = (jnp.dot(nh, wrh_ref[...], preferred_element_type=F32)
              + jnp.dot(nl, wrh_ref[...], preferred_element_type=F32)
              + jnp.dot(nh, wrl_ref[...], preferred_element_type=F32)) + br_ref[...]

    lane = lax.broadcasted_iota(jnp.int32, (tm, LANES), 1)
    lane_f = lane.astype(F32)
    big = float(LANES)

    def top1(vals):
        m = jnp.max(vals, axis=-1, keepdims=True)
        idx = jnp.min(jnp.where(vals == m, lane_f, big), axis=-1, keepdims=True)
        return m, idx

    is_group = (lane >= N_EXPERTS) & (lane < N_EXPERTS + N_GROUPS)
    gl = jnp.where(is_group, logits, -jnp.inf)
    g_max, g_lane = top1(gl)
    g_p = 1.0 / jnp.sum(jnp.exp(gl - g_max), axis=-1, keepdims=True)
    g_idx = g_lane - float(N_EXPERTS)

    in_group = (lane < N_EXPERTS) & ((lane // EXPERTS_PER_GROUP).astype(F32) == g_idx)
    el = jnp.where(in_group, logits, -jnp.inf)
    m1, i1 = top1(el)
    el2 = jnp.where(lane_f == i1, -jnp.inf, el)
    m2, i2 = top1(el2)
    e2 = jnp.exp(m2 - m1)
    w1 = g_p / (1.0 + e2)
    w2 = g_p * e2 / (1.0 + e2)

    onehot = jnp.where((lane_f == i1) | (lane_f == i2), 1.0, 0.0)
    before = jnp.dot(ltri_ref[...], onehot.astype(BF16), preferred_element_type=F32) + carry_ref[0:1, :]
    rank1 = jnp.sum(jnp.where(lane_f == i1, before, 0.0), axis=-1, keepdims=True)
    rank2 = jnp.sum(jnp.where(lane_f == i2, before, 0.0), axis=-1, keepdims=True)
    carry = carry_ref[0:1, :] + jnp.sum(onehot, axis=0, keepdims=True)
    carry_ref[...] = jnp.broadcast_to(carry, carry_ref.shape)
    cnt_ref[...] = jnp.broadcast_to(carry, cnt_ref.shape)

    rec = jnp.zeros((tm, LANES), F32)
    for ln, val in ((R_ID0, i1), (R_ID1, i2), (R_W0, w1), (R_W1, w2), (R_RANK0, rank1), (R_RANK1, rank2)):
        rec = jnp.where(lane == ln, val, rec)
    route_ref[...] = rec
    meta_ref[...] = rec.T[:SUBLANES, :]


def _out_route(x2, yr, ya, g_att, w_out, g_ffn, wr_hi, wr_lo, b_rt, ltri, *, d_rnn):
    t, d = x2.shape
    tm = SEQ_BLOCK
    d_att = ya.shape[1]
    row = lambda i: (i, 0)
    const = lambda i: (0, 0)
    return pl.pallas_call(
        functools.partial(_out_route_kernel, d_rnn=d_rnn),
        grid=(t // tm,),
        in_specs=[
            pl.BlockSpec((tm, d), row),
            pl.BlockSpec((tm, d_rnn), row),
            pl.BlockSpec((tm, d_att), row),
            pl.BlockSpec((1, d_att), const),
            pl.BlockSpec((d_rnn + d_att, d), const),
            pl.BlockSpec((1, d), const),
            pl.BlockSpec((d, LANES), const),
            pl.BlockSpec((d, LANES), const),
            pl.BlockSpec((1, LANES), const),
            pl.BlockSpec((tm, tm), const),
        ],
        out_specs=(
            pl.BlockSpec((tm, d), row),
            pl.BlockSpec((tm * (d // LANES), LANES), row),
            pl.BlockSpec((tm, LANES), row),
            pl.BlockSpec((SUBLANES, tm), lambda i: (0, i)),
            pl.BlockSpec((SUBLANES, LANES), const),
        ),
        out_shape=(
            jax.ShapeDtypeStruct((t, d), F32),
            jax.ShapeDtypeStruct((t * (d // LANES), LANES), F32),
            jax.ShapeDtypeStruct((t, LANES), F32),
            jax.ShapeDtypeStruct((SUBLANES, t), F32),
            jax.ShapeDtypeStruct((SUBLANES, LANES), F32),
        ),
        scratch_shapes=[pltpu.VMEM((SUBLANES, LANES), F32)],
        compiler_params=_params("arbitrary"),
        name="out_route",
    )(x2, yr, ya, g_att, w_out, g_ffn, wr_hi, wr_lo, b_rt, ltri)


def _positions_kernel(off_ref, meta_ref, pos_ref):
    m = meta_ref[...]
    ids = m[R_ID0:R_ID1 + 1, :]
    acc = m[R_RANK0:R_RANK1 + 1, :]
    for e in range(N_EXPERTS):
        acc = acc + jnp.where(ids == float(e), off_ref[e].astype(F32), 0.0)
    pos_ref[...] = acc.astype(jnp.int32)


def _positions(off, meta):
    t = meta.shape[1]
    return pl.pallas_call(
        _positions_kernel,
        grid_spec=pltpu.PrefetchScalarGridSpec(
            num_scalar_prefetch=1,
            grid=(1,),
            in_specs=[pl.BlockSpec((SUBLANES, t), lambda i, off_ref: (0, 0))],
            out_specs=pl.BlockSpec((TOP_K, t), lambda i, off_ref: (0, 0)),
        ),
        out_shape=jax.ShapeDtypeStruct((TOP_K, t), jnp.int32),
        compiler_params=_params("arbitrary"),
        name="positions",
    )(off, meta)


def _dispatch_kernel(pos_ref, xn_ref, xs_ref, sem, *, n_tok, tm, chunks):
    base = pl.program_id(0) * tm

    def issue(r, carry):
        for k in range(TOP_K):
            _token_copy(xn_ref, r, xs_ref, pos_ref[k * n_tok + base + r], sem, chunks).start()
        return carry

    def drain(r, carry):
        for k in range(TOP_K):
            _token_copy(xn_ref, 0, xs_ref, 0, sem, chunks).wait()
        return carry

    lax.fori_loop(0, tm, issue, 0, unroll=DMA_UNROLL)
    lax.fori_loop(0, tm, drain, 0, unroll=DMA_UNROLL)


def _dispatch(pos, xn_tiles, t):
    chunks = xn_tiles.shape[0] // t
    tm = DISPATCH_BLOCK
    return pl.pallas_call(
        functools.partial(_dispatch_kernel, n_tok=t, tm=tm, chunks=chunks),
        grid_spec=pltpu.PrefetchScalarGridSpec(
            num_scalar_prefetch=1,
            grid=(t // tm,),
            in_specs=[pl.BlockSpec((tm * chunks, LANES), lambda i, pos_ref: (i, 0))],
            out_specs=pl.BlockSpec(memory_space=pl.ANY),
            scratch_shapes=[pltpu.SemaphoreType.DMA(())],
        ),
        out_shape=jax.ShapeDtypeStruct((TOP_K * t * chunks, LANES), F32),
        compiler_params=_params("arbitrary"),
        name="dispatch",
    )(pos, xn_tiles)


def _ffn_kernel(tile_ref, exp_ref, lo_ref, hi_ref, first_ref, init_ref,
                xs_ref, wg_ref, wu_ref, wd_ref, ys_ref, wg_s, wu_s, wd_s, *, tm):
    g = pl.program_id(0)
    chunks = xs_ref.shape[0] // tm

    @pl.when(first_ref[g] == 1)
    def _():
        wg_s[...] = wg_ref[...].astype(BF16)
        wu_s[...] = wu_ref[...].astype(BF16)
        wd_s[...] = wd_ref[...].astype(BF16)

    lo = lo_ref[g]
    hi = hi_ref[g]

    @pl.when(hi > lo)
    def _():
        x = _load_token_tiles(xs_ref, tm, chunks).astype(BF16)
        hg = jnp.dot(x, wg_s[...], preferred_element_type=F32)
        hu = jnp.dot(x, wu_s[...], preferred_element_type=F32)
        h = (hg * _sigmoid(hg)) * hu
        y = jnp.dot(h.astype(BF16), wd_s[...], preferred_element_type=F32)
        row = lax.broadcasted_iota(jnp.int32, y.shape, 0)
        mine = (row >= lo) & (row < hi)

        @pl.when(init_ref[g] == 1)
        def _():
            _store_token_tiles(ys_ref, jnp.where(mine, y, 0.0))

        @pl.when(init_ref[g] == 0)
        def _():
            _store_token_tiles(ys_ref, jnp.where(mine, y, _load_token_tiles(ys_ref, tm, chunks)))


def _expert_ffn(meta, xs, w_gate, w_up, w_down, layer):
    d, d_e = w_gate.shape[-2:]
    chunks = d // LANES
    tm = FFN_TILE
    steps = meta[0].shape[0]
    x_map = lambda g, tile, exp, lo, hi, first, init: (tile[g], 0)
    w_map = lambda g, tile, exp, lo, hi, first, init: (layer, exp[g], 0, 0)
    return pl.pallas_call(
        functools.partial(_ffn_kernel, tm=tm),
        grid_spec=pltpu.PrefetchScalarGridSpec(
            num_scalar_prefetch=6,
            grid=(steps,),
            in_specs=[
                pl.BlockSpec((tm * chunks, LANES), x_map),
                pl.BlockSpec((None, None, d, d_e), w_map),
                pl.BlockSpec((None, None, d, d_e), w_map),
                pl.BlockSpec((None, None, d_e, d), w_map),
            ],
            out_specs=pl.BlockSpec((tm * chunks, LANES), x_map),
            scratch_shapes=[
                pltpu.VMEM((d, d_e), BF16),
                pltpu.VMEM((d, d_e), BF16),
                pltpu.VMEM((d_e, d), BF16),
            ],
        ),
        out_shape=jax.ShapeDtypeStruct(xs.shape, F32),
        compiler_params=_params("arbitrary"),
        name="expert_ffn",
    )(*meta, xs, w_gate, w_up, w_down)


def _ffn_schedule(cnt, off, rows):
    tm = FFN_TILE
    n_tiles = rows // tm
    steps = n_tiles + N_EXPERTS - 1
    end = off + cnt
    first_tile = off // tm
    last_tile = jnp.where(cnt > 0, (end - 1) // tm, first_tile - 1)
    visits = last_tile - first_tile + 1
    vis_end = jnp.cumsum(visits)
    vis_start = vis_end - visits
    total = vis_end[-1]
    g = jnp.arange(steps, dtype=jnp.int32)
    gc = jnp.minimum(g, total - 1)
    exp = jnp.searchsorted(vis_end, gc, side="right").astype(jnp.int32)
    tile = first_tile[exp] + (gc - vis_start[exp])
    valid = g < total
    lo = jnp.where(valid, jnp.maximum(off[exp], tile * tm) - tile * tm, 0)
    hi = jnp.where(valid, jnp.minimum(end[exp], (tile + 1) * tm) - tile * tm, 0)
    prev_exp = jnp.concatenate([jnp.full((1,), -1, jnp.int32), exp[:-1]])
    prev_tile = jnp.concatenate([jnp.full((1,), -1, jnp.int32), tile[:-1]])
    first = (valid & (exp != prev_exp)).astype(jnp.int32)
    init = (valid & (tile != prev_tile)).astype(jnp.int32)
    i32 = lambda a: a.astype(jnp.int32)
    return i32(tile), i32(exp), i32(lo), i32(hi), first, init


def _combine_kernel(pos_ref, x1_ref, route_ref, p_ref, ys_ref, gp_ref, wg_ref, bg_ref, wp_ref,
                    o_ref, buf, sem, *, n_tok):
    tm, d = x1_ref.shape
    chunks = d // LANES
    i = pl.program_id(0)
    last = pl.num_programs(0) - 1
    slot = i % 2

    def start_row(block, slot_, r):
        for k in range(TOP_K):
            src = pos_ref[k * n_tok + block * tm + r]
            _token_copy(ys_ref, src, buf.at[slot_, k], r, sem.at[slot_], chunks).start()

    def drain(slot_):
        def body(r, carry):
            for k in range(TOP_K):
                _token_copy(ys_ref, 0, buf.at[slot_, k], 0, sem.at[slot_], chunks).wait()
            return carry
        lax.fori_loop(0, tm, body, 0, unroll=DMA_UNROLL)

    @pl.when(i == 0)
    def _():
        def body(r, carry):
            start_row(0, 0, r)
            return carry
        lax.fori_loop(0, tm, body, 0, unroll=DMA_UNROLL)

    drain(slot)
    rec = route_ref[...]
    w0 = rec[:, R_W0:R_W0 + 1]
    w1 = rec[:, R_W1:R_W1 + 1]
    y0 = _load_token_tiles(buf.at[slot, 0], tm, chunks)
    y1 = _load_token_tiles(buf.at[slot, 1], tm, chunks)
    x2 = x1_ref[...] + (y0 * w0 + y1 * w1)

    for r in range(tm):
        start_row(jnp.minimum(i + 1, last), 1 - slot, r)

    n = _rms(x2) * gp_ref[...]
    gate = _sigmoid(jnp.dot(n.astype(BF16), wg_ref[...], preferred_element_type=F32) + bg_ref[...])
    pe = jnp.dot(p_ref[...].astype(BF16), wp_ref[...], preferred_element_type=F32)
    o_ref[...] = x2 + pe * gate

    @pl.when(i == last)
    def _():
        drain(1 - slot)


def _combine_ple(pos, x1, route, p3, ys, g_ple, w_gate, b_gate, w_proj, layer):
    t, d = x1.shape
    dp = p3.shape[-1]
    tm = COMBINE_BLOCK
    row = lambda i, pos_ref: (i, 0)
    const = lambda i, pos_ref: (0, 0)
    return pl.pallas_call(
        functools.partial(_combine_kernel, n_tok=t),
        grid_spec=pltpu.PrefetchScalarGridSpec(
            num_scalar_prefetch=1,
            grid=(t // tm,),
            in_specs=[
                pl.BlockSpec((tm, d), row),
                pl.BlockSpec((tm, LANES), row),
                pl.BlockSpec((None, tm, dp), lambda i, pos_ref: (layer, i, 0)),
                pl.BlockSpec(memory_space=pl.ANY),
                pl.BlockSpec((1, d), const),
                pl.BlockSpec((d, d), const),
                pl.BlockSpec((1, d), const),
                pl.BlockSpec((dp, d), const),
            ],
            out_specs=pl.BlockSpec((tm, d), row),
            scratch_shapes=[
                pltpu.VMEM((2, TOP_K, tm * (d // LANES), LANES), F32),
                pltpu.SemaphoreType.DMA((2,)),
            ],
        ),
        out_shape=jax.ShapeDtypeStruct((t, d), F32),
        compiler_params=_params("arbitrary"),
        name="combine_ple",
    )(pos, x1, route, p3, ys, g_ple, w_gate, b_gate, w_proj)


def _block_diag(w):
    nb, n, _ = w.shape
    eye = jnp.eye(nb, dtype=w.dtype)
    return jnp.einsum("hij,hg->higj", w, eye).reshape(nb * n, nb * n)


def kernel(x, p, mix_norm, w_in, b_forget, conv_w, conv_b, w_rgate, b_rgate, w_igate, b_igate, lru_lambda,
           q_norm, k_norm, lru_out_norm, att_out_norm, w_out, ffn_norm, w_group, b_group, w_router, b_router,
           w_exp_gate, w_exp_up, w_exp_down, ple_norm, w_ple_gate, b_ple_gate, w_ple_proj):
    batch, seq, d = x.shape
    depth = w_in.shape[0]
    d_rnn = conv_w.shape[-1]
    n_heads = b_forget.shape[-1]
    d_att = n_heads * HEAD_DIM
    t = batch * seq
    assert n_heads == SUBLANES and d_att % LANES == 0 and seq % SEQ_BLOCK == 0 and seq % Q_BLOCK == 0
    assert HEADS_PER_BLOCK == 2
    assert Q_BLOCK == SEQ_BLOCK and (TOP_K * t) % FFN_TILE == 0
    assert t % DISPATCH_BLOCK == 0 and t % COMBINE_BLOCK == 0
    assert w_router.shape[-1] == N_EXPERTS and w_group.shape[-1] == N_GROUPS

    idx = jnp.arange(SEQ_BLOCK)
    tri_incl = (idx[:, None] <= idx[None, :]).astype(BF16)
    tri_strict = (idx[None, :] < idx[:, None]).astype(BF16)
    hid = jnp.arange(d_att) // HEAD_DIM
    gsum = (hid[:, None] == hid[None, :]).astype(BF16)
    row2 = lambda a: a.reshape(1, -1).astype(F32)
    n_in_pad = 2 * d_rnn + 3 * d_att + LANES
    p3 = p.reshape(depth, t, p.shape[-1])

    x2 = x.reshape(t, d)
    for i in range(depth):
        w_pad = jnp.pad(w_in[i], ((0, 0), (0, n_in_pad - w_in.shape[-1]))).astype(BF16)
        qg = row2(jnp.tile(q_norm[i], n_heads) * (LOG2E / math.sqrt(HEAD_DIM)))
        kg = row2(jnp.tile(k_norm[i], n_heads))
        zx, zg, q, k, v, c = _in_proj(
            x2, row2(mix_norm[i]), w_pad, gsum, qg, kg, b_forget[i].reshape(n_heads, 1).astype(F32), tri_incl,
            batch=batch, seq=seq, d_rnn=d_rnn, d_att=d_att)

        w_gates = jnp.concatenate([_block_diag(w_rgate[i]), _block_diag(w_igate[i])], axis=1).astype(BF16)
        yr = _rglru(zx.reshape(batch, seq, d_rnn), zg.reshape(batch, seq, d_rnn), conv_w[i].astype(F32),
                    row2(conv_b[i]), w_gates, row2(b_rgate[i]), row2(b_igate[i]), row2(lru_lambda[i]),
                    row2(lru_out_norm[i]))
        ya = _attention(q.reshape(batch, seq, d_att), k.reshape(batch, seq, d_att),
                        v.reshape(batch, seq, HEADS_PER_BLOCK * d_att), c)

        w_rt = jnp.pad(jnp.concatenate([w_router[i], w_group[i]], axis=1),
                       ((0, 0), (0, LANES - N_EXPERTS - N_GROUPS))).astype(F32)
        wr_hi, wr_lo = _split_bf16(w_rt)
        b_rt = jnp.pad(jnp.concatenate([b_router[i], b_group[i]]), (0, LANES - N_EXPERTS - N_GROUPS))
        x1, xn, route, meta, cnt = _out_route(
            x2, yr.reshape(t, d_rnn), ya.reshape(t, d_att), row2(att_out_norm[i]), w_out[i].astype(BF16),
            row2(ffn_norm[i]), wr_hi, wr_lo, row2(b_rt), tri_strict, d_rnn=d_rnn)

        cnt_e = cnt[0, :N_EXPERTS].astype(jnp.int32)
        off_e = jnp.cumsum(cnt_e) - cnt_e
        pos = _positions(off_e, meta).reshape(-1)

        xs = _dispatch(pos, xn, t)
        ys = _expert_ffn(_ffn_schedule(cnt_e, off_e, TOP_K * t), xs, w_exp_gate, w_exp_up, w_exp_down, i)
        x2 = _combine_ple(pos, x1, route, p3, ys, row2(ple_norm[i]), w_ple_gate[i].astype(BF16),
                          row2(b_ple_gate[i]), w_ple_proj[i].astype(BF16), i)
    return x2.reshape(batch, seq, d)
```

```python
import functools
import math

import jax
import jax.numpy as jnp
from jax import lax
from jax.experimental import pallas as pl
from jax.experimental.pallas import tpu as pltpu

F32 = jnp.float32
BF16 = jnp.bfloat16

EPS = 1e-6
LRU_C = 8.0
HEAD_DIM = 64
CONV_WIDTH = 4
N_GROUPS = 4
EXPERTS_PER_GROUP = 8
N_EXPERTS = N_GROUPS * EXPERTS_PER_GROUP
TOP_K = 2

LOG2E = math.log2(math.e)

LANES = 128
SUBLANES = 8
VMEM_LIMIT = 56 * 1024 * 1024
HEADS_PER_BLOCK = LANES // HEAD_DIM

SEQ_BLOCK = 512
Q_BLOCK = 512
FFN_TILE = 256
DISPATCH_BLOCK = 1024
COMBINE_BLOCK = 256
DMA_UNROLL = 8

R_ID0, R_ID1, R_W0, R_W1, R_RANK0, R_RANK1 = 0, 1, 2, 3, 4, 5


def _sigmoid(x):
    return 1.0 / (1.0 + jnp.exp(-x))


def _softplus(x):
    return jnp.maximum(x, 0.0) + jnp.log1p(jnp.exp(-jnp.abs(x)))


def _gelu_tanh(x):
    c = math.sqrt(2.0 / math.pi)
    return x * (0.5 * (1.0 + jnp.tanh(c * (x + 0.044715 * (x * x * x)))))


def _rms(x):
    return x * lax.rsqrt(jnp.mean(x * x, axis=-1, keepdims=True) + EPS)


def _split_bf16(x):
    hi = x.astype(BF16)
    lo = (x - hi.astype(F32)).astype(BF16)
    return hi, lo


def _store_token_tiles(ref, x):
    n, d = x.shape
    chunks = d // LANES
    for j in range(chunks):
        ref[pl.ds(j, n, stride=chunks), :] = x[:, j * LANES:(j + 1) * LANES]


def _load_token_tiles(ref, n, chunks):
    return jnp.concatenate([ref[pl.ds(j, n, stride=chunks), :] for j in range(chunks)], axis=1)


def _token_copy(src_ref, src_tok, dst_ref, dst_tok, sem, chunks):
    def rows(tok):
        start = tok * chunks
        return pl.ds(start if isinstance(tok, int) else pl.multiple_of(start, chunks), chunks)
    return pltpu.make_async_copy(src_ref.at[rows(src_tok), :], dst_ref.at[rows(dst_tok), :], sem)


def _params(*sem):
    return pltpu.CompilerParams(dimension_semantics=sem, vmem_limit_bytes=VMEM_LIMIT)


def _in_proj_kernel(x_ref, g_ref, w_ref, gsum_ref, qg_ref, kg_ref, bf_ref, tri_ref,
                    zx_ref, zg_ref, q_ref, k_ref, v_ref, c_ref, carry_ref, *, blocks_per_seq, d_rnn, d_att):
    i = pl.program_id(0)
    n = _rms(x_ref[...]) * g_ref[...]
    z = jnp.dot(n.astype(BF16), w_ref[...], preferred_element_type=F32)
    o = 0
    zx_ref[...] = z[:, o:o + d_rnn]; o += d_rnn
    zg_ref[...] = z[:, o:o + d_rnn]; o += d_rnn
    zq = z[:, o:o + d_att]; o += d_att
    zk = z[:, o:o + d_att]; o += d_att
    zv = z[:, o:o + d_att]; o += d_att
    zf = z[:, o:o + LANES]

    lane = lax.broadcasted_iota(jnp.int32, (zv.shape[0], LANES), 1)
    v_parts = []
    for blk in range(d_att // LANES):
        vb = zv[:, blk * LANES:(blk + 1) * LANES]
        v_parts += [jnp.where(lane // HEAD_DIM == j, vb, 1.0) for j in range(HEADS_PER_BLOCK)]
    v_ref[...] = jnp.concatenate(v_parts, axis=1).astype(BF16)

    def head_norm(t, gain):
        ss = jnp.dot((t * t).astype(BF16), gsum_ref[...], preferred_element_type=F32)
        return t * lax.rsqrt(ss * (1.0 / HEAD_DIM) + EPS) * gain

    q_ref[...] = head_norm(zq, qg_ref[...]).astype(BF16)
    k_ref[...] = head_norm(zk, kg_ref[...]).astype(BF16)

    zft = zf.T[:SUBLANES, :] + bf_ref[...]
    logf = jnp.minimum(zft, 0.0) - jnp.log1p(jnp.exp(-jnp.abs(zft)))
    hi, lo = _split_bf16(logf)
    cs = (jnp.dot(hi, tri_ref[...], preferred_element_type=F32)
          + jnp.dot(lo, tri_ref[...], preferred_element_type=F32))

    @pl.when(i % blocks_per_seq == 0)
    def _():
        carry_ref[...] = jnp.zeros_like(carry_ref)

    c = cs + carry_ref[:, 0:1]
    c_ref[0, 0] = c * LOG2E
    carry_ref[...] = jnp.broadcast_to(c[:, SEQ_BLOCK - 1:SEQ_BLOCK], carry_ref.shape)


def _in_proj(x2, g_mix, w_pad, gsum, qg, kg, bf, tri, *, batch, seq, d_rnn, d_att):
    t, d = x2.shape
    tm = SEQ_BLOCK
    bps = seq // tm
    n_pad = w_pad.shape[1]
    row = lambda i: (i, 0)
    const = lambda i: (0, 0)
    out_shape = (
        jax.ShapeDtypeStruct((t, d_rnn), F32),
        jax.ShapeDtypeStruct((t, d_rnn), F32),
        jax.ShapeDtypeStruct((t, d_att), BF16),
        jax.ShapeDtypeStruct((t, d_att), BF16),
        jax.ShapeDtypeStruct((t, HEADS_PER_BLOCK * d_att), BF16),
        jax.ShapeDtypeStruct((batch, bps, SUBLANES, tm), F32),
    )
    return pl.pallas_call(
        functools.partial(_in_proj_kernel, blocks_per_seq=bps, d_rnn=d_rnn, d_att=d_att),
        grid=(t // tm,),
        in_specs=[
            pl.BlockSpec((tm, d), row),
            pl.BlockSpec((1, d), const),
            pl.BlockSpec((d, n_pad), const),
            pl.BlockSpec((d_att, d_att), const),
            pl.BlockSpec((1, d_att), const),
            pl.BlockSpec((1, d_att), const),
            pl.BlockSpec((SUBLANES, 1), const),
            pl.BlockSpec((tm, tm), const),
        ],
        out_specs=(
            pl.BlockSpec((tm, d_rnn), row),
            pl.BlockSpec((tm, d_rnn), row),
            pl.BlockSpec((tm, d_att), row),
            pl.BlockSpec((tm, d_att), row),
            pl.BlockSpec((tm, HEADS_PER_BLOCK * d_att), row),
            pl.BlockSpec((1, 1, SUBLANES, tm), lambda i: (i // bps, i % bps, 0, 0)),
        ),
        out_shape=out_shape,
        scratch_shapes=[pltpu.VMEM((SUBLANES, LANES), F32)],
        compiler_params=_params("arbitrary"),
        name="in_proj",
    )(x2, g_mix, w_pad, gsum, qg, kg, bf, tri)


def _rglru_kernel(zx_ref, zg_ref, cw_ref, cb_ref, wg_ref, br_ref, bi_ref, lam_ref, go_ref,
                  y_ref, tail_ref, h_ref, a_s, u_s, hs_s, *, d_rnn):
    j = pl.program_id(1)
    ts = zx_ref.shape[1]

    @pl.when(j == 0)
    def _():
        tail_ref[...] = jnp.zeros_like(tail_ref)
        h_ref[...] = jnp.zeros_like(h_ref)

    x = zx_ref[0]
    tail = tail_ref[...]
    row8 = lax.broadcasted_iota(jnp.int32, (SUBLANES, d_rnn), 0)

    xc = x * cw_ref[CONV_WIDTH - 1:CONV_WIDTH, :] + cb_ref[...]
    for d in range(1, CONV_WIDTH):
        xr = pltpu.roll(x, d, 0)
        top = jnp.where(row8 < d, pltpu.roll(tail, d, 0), xr[:SUBLANES])
        xs = jnp.concatenate([top, xr[SUBLANES:]], axis=0)
        xc = xc + xs * cw_ref[CONV_WIDTH - 1 - d:CONV_WIDTH - d, :]
    tail_ref[...] = x[ts - SUBLANES:, :]

    gl = jnp.dot(xc.astype(BF16), wg_ref[...], preferred_element_type=F32)
    r = _sigmoid(gl[:, :d_rnn] + br_ref[...])
    ig = _sigmoid(gl[:, d_rnn:] + bi_ref[...])
    log_a = (-LRU_C) * r * _softplus(-lam_ref[...])
    a = jnp.exp(log_a)
    a_s[...] = a
    u_s[...] = jnp.sqrt(1.0 - a * a) * (ig * xc)

    def group(gidx, h):
        r0 = pl.multiple_of(gidx * SUBLANES, SUBLANES)
        a8 = a_s[pl.ds(r0, SUBLANES), :]
        b8 = u_s[pl.ds(r0, SUBLANES), :]
        for d in (1, 2, 4):
            keep = row8 >= d
            a_sh = jnp.where(keep, pltpu.roll(a8, d, 0), 1.0)
            b_sh = jnp.where(keep, pltpu.roll(b8, d, 0), 0.0)
            b8 = a8 * b_sh + b8
            a8 = a8 * a_sh
        h8 = a8 * h + b8
        hs_s[pl.ds(r0, SUBLANES), :] = h8
        return h8[SUBLANES - 1:SUBLANES, :]

    h_last = lax.fori_loop(0, ts // SUBLANES, group, h_ref[0:1, :], unroll=4)
    h_ref[...] = jnp.broadcast_to(h_last, h_ref.shape)

    y = hs_s[...] * _gelu_tanh(zg_ref[0])
    y_ref[0] = (_rms(y) * go_ref[...]).astype(BF16)


def _rglru(zx, zg, conv_w, conv_b, w_gates, b_r, b_i, lam, g_out):
    b, s, d_rnn = zx.shape
    ts = SEQ_BLOCK
    blk = lambda bi, j: (bi, j, 0)
    const = lambda bi, j: (0, 0)
    return pl.pallas_call(
        functools.partial(_rglru_kernel, d_rnn=d_rnn),
        grid=(b, s // ts),
        in_specs=[
            pl.BlockSpec((1, ts, d_rnn), blk),
            pl.BlockSpec((1, ts, d_rnn), blk),
            pl.BlockSpec((CONV_WIDTH, d_rnn), const),
            pl.BlockSpec((1, d_rnn), const),
            pl.BlockSpec((d_rnn, 2 * d_rnn), const),
            pl.BlockSpec((1, d_rnn), const),
            pl.BlockSpec((1, d_rnn), const),
            pl.BlockSpec((1, d_rnn), const),
            pl.BlockSpec((1, d_rnn), const),
        ],
        out_specs=pl.BlockSpec((1, ts, d_rnn), blk),
        out_shape=jax.ShapeDtypeStruct((b, s, d_rnn), BF16),
        scratch_shapes=[
            pltpu.VMEM((SUBLANES, d_rnn), F32),
            pltpu.VMEM((SUBLANES, d_rnn), F32),
            pltpu.VMEM((ts, d_rnn), F32),
            pltpu.VMEM((ts, d_rnn), F32),
            pltpu.VMEM((ts, d_rnn), F32),
        ],
        compiler_params=_params("arbitrary", "arbitrary"),
        name="rglru",
    )(zx, zg, conv_w, conv_b, w_gates, b_r, b_i, lam, g_out)


def _attn_kernel(q_ref, k_ref, v_ref, c_ref, o_ref, sa_ref, sb_ref, *, bq, bk):
    hp = pl.program_id(1)
    qi = pl.program_id(2)
    q2 = q_ref[0]
    lane = lax.broadcasted_iota(jnp.int32, (bq, LANES), 1)
    row = qi * bq + lax.broadcasted_iota(jnp.int32, (bq, bk), 0)
    col0 = lax.broadcasted_iota(jnp.int32, (bq, bk), 1)
    n_full = qi * (bq // bk)
    qs = [jnp.where(lane // HEAD_DIM == j, q2, jnp.zeros_like(q2)) for j in range(HEADS_PER_BLOCK)]

    def scores(kb, s_ref):
        k0 = pl.multiple_of(kb * bk, bk)
        kblk = k_ref[0, pl.ds(k0, bk), :]
        for j in range(HEADS_PER_BLOCK):
            s = lax.dot_general(qs[j], kblk, (((1,), (1,)), ((), ())), preferred_element_type=F32)
            s_ref[j] = s - c_ref[0, kb, pl.ds(hp * HEADS_PER_BLOCK + j, 1), :]

    def update(kb, s_ref, stats, masked):
        k0 = pl.multiple_of(kb * bk, bk)
        new = []
        for j in range(HEADS_PER_BLOCK):
            m, acc = stats[j]
            s = s_ref[j]
            if masked:
                s = jnp.where(col0 + k0 <= row, s, -jnp.inf)
            m_new = jnp.maximum(m, jnp.max(s, axis=-1, keepdims=True))
            p = jnp.exp2(s - m_new)
            vj = v_ref[0, pl.ds(k0, bk), j * LANES:(j + 1) * LANES]
            acc = jnp.exp2(m - m_new) * acc + jnp.dot(p.astype(BF16), vj, preferred_element_type=F32)
            new.append((m_new, acc))
        return tuple(new)

    def finish(stats):
        o = None
        for j in range(HEADS_PER_BLOCK):
            acc = stats[j][1]
            oj = acc / pltpu.roll(acc, HEAD_DIM, 1)
            o = oj if o is None else jnp.where(lane // HEAD_DIM == j, oj, o)
        o_ref[0] = o.astype(o_ref.dtype)

    def pair(i, stats):
        scores(2 * i + 1, sb_ref)
        stats = update(2 * i, sa_ref, stats, False)
        scores(2 * i + 2, sa_ref)
        return update(2 * i + 1, sb_ref, stats, False)

    stats = tuple((jnp.full((bq, 1), -jnp.inf, F32), jnp.zeros((bq, LANES), F32)) for _ in range(HEADS_PER_BLOCK))
    scores(0, sa_ref)
    stats = lax.fori_loop(0, n_full // 2, pair, stats)

    @pl.when(n_full % 2 == 0)
    def _():
        finish(update(n_full, sa_ref, stats, True))

    @pl.when(n_full % 2 == 1)
    def _():
        scores(n_full, sb_ref)
        st = update(n_full - 1, sa_ref, stats, False)
        finish(update(n_full, sb_ref, st, True))


def _attention(q, k, v, c):
    b, s, d_att = q.shape
    bq, bk = Q_BLOCK, SEQ_BLOCK
    return pl.pallas_call(
        functools.partial(_attn_kernel, bq=bq, bk=bk),
        grid=(b, d_att // LANES, s // bq),
        in_specs=[
            pl.BlockSpec((1, bq, LANES), lambda bi, hp, qi: (bi, qi, hp)),
            pl.BlockSpec((1, s, LANES), lambda bi, hp, qi: (bi, 0, hp)),
            pl.BlockSpec((1, s, HEADS_PER_BLOCK * LANES), lambda bi, hp, qi: (bi, 0, hp)),
            pl.BlockSpec((1, s // bk, SUBLANES, bk), lambda bi, hp, qi: (bi, 0, 0, 0)),
        ],
        out_specs=pl.BlockSpec((1, bq, LANES), lambda bi, hp, qi: (bi, qi, hp)),
        out_shape=jax.ShapeDtypeStruct((b, s, d_att), BF16),
        scratch_shapes=[pltpu.VMEM((HEADS_PER_BLOCK, bq, bk), F32), pltpu.VMEM((HEADS_PER_BLOCK, bq, bk), F32)],
        compiler_params=_params("arbitrary", "arbitrary", "arbitrary"),
        name="attention",
    )(q, k, v, c)


def _out_route_kernel(x_ref, yr_ref, ya_ref, ga_ref, wo_ref, gf_ref, wrh_ref, wrl_ref, br_ref, ltri_ref,
                      x1_ref, xn_ref, route_ref, meta_ref, cnt_ref, carry_ref, *, d_rnn):
    i = pl.program_id(0)
    tm = x_ref.shape[0]

    @pl.when(i == 0)
    def _():
        carry_ref[...] = jnp.zeros_like(carry_ref)

    ya = _rms(ya_ref[...].astype(F32)) * ga_ref[...]
    x1 = (x_ref[...]
          + jnp.dot(yr_ref[...], wo_ref[:d_rnn, :], preferred_element_type=F32)
          + jnp.dot(ya.astype(BF16), wo_ref[d_rnn:, :], preferred_element_type=F32))
    x1_ref[...] = x1
    xn = _rms(x1) * gf_ref[...]
    _store_token_tiles(xn_ref, xn)

    nh, nl = _split_bf16(xn)
    logits = (jnp.dot(nh, wrh_ref[...], preferred_element_type=F32)
              + jnp.dot(nl, wrh_ref[...], preferred_element_type=F32)
              + jnp.dot(nh, wrl_ref[...], preferred_element_type=F32)) + br_ref[...]

    lane = lax.broadcasted_iota(jnp.int32, (tm, LANES), 1)
    lane_f = lane.astype(F32)
    big = float(LANES)

    def top1(vals):
        m = jnp.max(vals, axis=-1, keepdims=True)
        idx = jnp.min(jnp.where(vals == m, lane_f, big), axis=-1, keepdims=True)
        return m, idx

    is_group = (lane >= N_EXPERTS) & (lane < N_EXPERTS + N_GROUPS)
    gl = jnp.where(is_group, logits, -jnp.inf)
    g_max, g_lane = top1(gl)
    g_p = 1.0 / jnp.sum(jnp.exp(gl - g_max), axis=-1, keepdims=True)
    g_idx = g_lane - float(N_EXPERTS)

    in_group = (lane < N_EXPERTS) & ((lane // EXPERTS_PER_GROUP).astype(F32) == g_idx)
    el = jnp.where(in_group, logits, -jnp.inf)
    m1, i1 = top1(el)
    el2 = jnp.where(lane_f == i1, -jnp.inf, el)
    m2, i2 = top1(el2)
    e2 = jnp.exp(m2 - m1)
    w1 = g_p / (1.0 + e2)
    w2 = g_p * e2 / (1.0 + e2)

    onehot = jnp.where((lane_f == i1) | (lane_f == i2), 1.0, 0.0)
    before = jnp.dot(ltri_ref[...], onehot.astype(BF16), preferred_element_type=F32) + carry_ref[0:1, :]
    rank1 = jnp.sum(jnp.where(lane_f == i1, before, 0.0), axis=-1, keepdims=True)
    rank2 = jnp.sum(jnp.where(lane_f == i2, before, 0.0), axis=-1, keepdims=True)
    carry = carry_ref[0:1, :] + jnp.sum(onehot, axis=0, keepdims=True)
    carry_ref[...] = jnp.broadcast_to(carry, carry_ref.shape)
    cnt_ref[...] = jnp.broadcast_to(carry, cnt_ref.shape)

    rec = jnp.zeros((tm, LANES), F32)
    for ln, val in ((R_ID0, i1), (R_ID1, i2), (R_W0, w1), (R_W1, w2), (R_RANK0, rank1), (R_RANK1, rank2)):
        rec = jnp.where(lane == ln, val, rec)
    route_ref[...] = rec
    meta_ref[...] = rec.T[:SUBLANES, :]


def _out_route(x2, yr, ya, g_att, w_out, g_ffn, wr_hi, wr_lo, b_rt, ltri, *, d_rnn):
    t, d = x2.shape
    tm = SEQ_BLOCK
    d_att = ya.shape[1]
    row = lambda i: (i, 0)
    const = lambda i: (0, 0)
    return pl.pallas_call(
        functools.partial(_out_route_kernel, d_rnn=d_rnn),
        grid=(t // tm,),
        in_specs=[
            pl.BlockSpec((tm, d), row),
            pl.BlockSpec((tm, d_rnn), row),
            pl.BlockSpec((tm, d_att), row),
            pl.BlockSpec((1, d_att), const),
            pl.BlockSpec((d_rnn + d_att, d), const),
            pl.BlockSpec((1, d), const),
            pl.BlockSpec((d, LANES), const),
            pl.BlockSpec((d, LANES), const),
            pl.BlockSpec((1, LANES), const),
            pl.BlockSpec((tm, tm), const),
        ],
        out_specs=(
            pl.BlockSpec((tm, d), row),
            pl.BlockSpec((tm * (d // LANES), LANES), row),
            pl.BlockSpec((tm, LANES), row),
            pl.BlockSpec((SUBLANES, tm), lambda i: (0, i)),
            pl.BlockSpec((SUBLANES, LANES), const),
        ),
        out_shape=(
            jax.ShapeDtypeStruct((t, d), F32),
            jax.ShapeDtypeStruct((t * (d // LANES), LANES), F32),
            jax.ShapeDtypeStruct((t, LANES), F32),
            jax.ShapeDtypeStruct((SUBLANES, t), F32),
            jax.ShapeDtypeStruct((SUBLANES, LANES), F32),
        ),
        scratch_shapes=[pltpu.VMEM((SUBLANES, LANES), F32)],
        compiler_params=_params("arbitrary"),
        name="out_route",
    )(x2, yr, ya, g_att, w_out, g_ffn, wr_hi, wr_lo, b_rt, ltri)


def _positions_kernel(off_ref, meta_ref, pos_ref):
    m = meta_ref[...]
    ids = m[R_ID0:R_ID1 + 1, :]
    acc = m[R_RANK0:R_RANK1 + 1, :]
    for e in range(N_EXPERTS):
        acc = acc + jnp.where(ids == float(e), off_ref[e].astype(F32), 0.0)
    pos_ref[...] = acc.astype(jnp.int32)


def _positions(off, meta):
    t = meta.shape[1]
    return pl.pallas_call(
        _positions_kernel,
        grid_spec=pltpu.PrefetchScalarGridSpec(
            num_scalar_prefetch=1,
            grid=(1,),
            in_specs=[pl.BlockSpec((SUBLANES, t), lambda i, off_ref: (0, 0))],
            out_specs=pl.BlockSpec((TOP_K, t), lambda i, off_ref: (0, 0)),
        ),
        out_shape=jax.ShapeDtypeStruct((TOP_K, t), jnp.int32),
        compiler_params=_params("arbitrary"),
        name="positions",
    )(off, meta)


def _dispatch_kernel(pos_ref, xn_ref, xs_ref, sem, *, n_tok, tm, chunks):
    base = pl.program_id(0) * tm

    def issue(r, carry):
        for k in range(TOP_K):
            _token_copy(xn_ref, r, xs_ref, pos_ref[k * n_tok + base + r], sem, chunks).start(priority=k % 2)
        return carry

    def drain(r, carry):
        for k in range(TOP_K):
            _token_copy(xn_ref, 0, xs_ref, 0, sem, chunks).wait()
        return carry

    lax.fori_loop(0, tm, issue, 0, unroll=DMA_UNROLL)
    lax.fori_loop(0, tm, drain, 0, unroll=DMA_UNROLL)


def _dispatch(pos, xn_tiles, t):
    chunks = xn_tiles.shape[0] // t
    tm = DISPATCH_BLOCK
    return pl.pallas_call(
        functools.partial(_dispatch_kernel, n_tok=t, tm=tm, chunks=chunks),
        grid_spec=pltpu.PrefetchScalarGridSpec(
            num_scalar_prefetch=1,
            grid=(t // tm,),
            in_specs=[pl.BlockSpec((tm * chunks, LANES), lambda i, pos_ref: (i, 0))],
            out_specs=pl.BlockSpec(memory_space=pl.ANY),
            scratch_shapes=[pltpu.SemaphoreType.DMA(())],
        ),
        out_shape=jax.ShapeDtypeStruct((TOP_K * t * chunks, LANES), F32),
        compiler_params=_params("arbitrary"),
        name="dispatch",
    )(pos, xn_tiles)


def _ffn_kernel(tile_ref, exp_ref, lo_ref, hi_ref, first_ref, init_ref,
                xs_ref, wg_ref, wu_ref, wd_ref, ys_ref, wg_s, wu_s, wd_s, *, tm):
    g = pl.program_id(0)
    chunks = xs_ref.shape[0] // tm

    @pl.when(first_ref[g] == 1)
    def _():
        wg_s[...] = wg_ref[...].astype(BF16)
        wu_s[...] = wu_ref[...].astype(BF16)
        wd_s[...] = wd_ref[...].astype(BF16)

    lo = lo_ref[g]
    hi = hi_ref[g]

    @pl.when(hi > lo)
    def _():
        x = _load_token_tiles(xs_ref, tm, chunks).astype(BF16)
        hg = jnp.dot(x, wg_s[...], preferred_element_type=F32)
        hu = jnp.dot(x, wu_s[...], preferred_element_type=F32)
        h = (hg * _sigmoid(hg)) * hu
        y = jnp.dot(h.astype(BF16), wd_s[...], preferred_element_type=F32)
        row = lax.broadcasted_iota(jnp.int32, y.shape, 0)
        mine = (row >= lo) & (row < hi)

        @pl.when(init_ref[g] == 1)
        def _():
            _store_token_tiles(ys_ref, jnp.where(mine, y, 0.0))

        @pl.when(init_ref[g] == 0)
        def _():
            _store_token_tiles(ys_ref, jnp.where(mine, y, _load_token_tiles(ys_ref, tm, chunks)))


def _expert_ffn(meta, xs, w_gate, w_up, w_down, layer):
    d, d_e = w_gate.shape[-2:]
    chunks = d // LANES
    tm = FFN_TILE
    steps = meta[0].shape[0]
    x_map = lambda g, tile, exp, lo, hi, first, init: (tile[g], 0)
    w_map = lambda g, tile, exp, lo, hi, first, init: (layer, exp[g], 0, 0)
    return pl.pallas_call(
        functools.partial(_ffn_kernel, tm=tm),
        grid_spec=pltpu.PrefetchScalarGridSpec(
            num_scalar_prefetch=6,
            grid=(steps,),
            in_specs=[
                pl.BlockSpec((tm * chunks, LANES), x_map),
                pl.BlockSpec((None, None, d, d_e), w_map),
                pl.BlockSpec((None, None, d, d_e), w_map),
                pl.BlockSpec((None, None, d_e, d), w_map),
            ],
            out_specs=pl.BlockSpec((tm * chunks, LANES), x_map),
            scratch_shapes=[
                pltpu.VMEM((d, d_e), BF16),
                pltpu.VMEM((d, d_e), BF16),
                pltpu.VMEM((d_e, d), BF16),
            ],
        ),
        out_shape=jax.ShapeDtypeStruct(xs.shape, F32),
        compiler_params=_params("arbitrary"),
        name="expert_ffn",
    )(*meta, xs, w_gate, w_up, w_down)


def _ffn_schedule(cnt, off, rows):
    tm = FFN_TILE
    n_tiles = rows // tm
    steps = n_tiles + N_EXPERTS - 1
    end = off + cnt
    first_tile = off // tm
    last_tile = jnp.where(cnt > 0, (end - 1) // tm, first_tile - 1)
    visits = last_tile - first_tile + 1
    vis_end = jnp.cumsum(visits)
    vis_start = vis_end - visits
    total = vis_end[-1]
    g = jnp.arange(steps, dtype=jnp.int32)
    gc = jnp.minimum(g, total - 1)
    exp = jnp.searchsorted(vis_end, gc, side="right").astype(jnp.int32)
    tile = first_tile[exp] + (gc - vis_start[exp])
    valid = g < total
    lo = jnp.where(valid, jnp.maximum(off[exp], tile * tm) - tile * tm, 0)
    hi = jnp.where(valid, jnp.minimum(end[exp], (tile + 1) * tm) - tile * tm, 0)
    prev_exp = jnp.concatenate([jnp.full((1,), -1, jnp.int32), exp[:-1]])
    prev_tile = jnp.concatenate([jnp.full((1,), -1, jnp.int32), tile[:-1]])
    first = (valid & (exp != prev_exp)).astype(jnp.int32)
    init = (valid & (tile != prev_tile)).astype(jnp.int32)
    i32 = lambda a: a.astype(jnp.int32)
    return i32(tile), i32(exp), i32(lo), i32(hi), first, init


def _combine_kernel(pos_ref, x1_ref, route_ref, p_ref, ys_ref, gp_ref, wg_ref, bg_ref, wp_ref,
                    o_ref, buf_a, buf_b, sem, *, n_tok):
    tm, d = x1_ref.shape
    chunks = d // LANES
    i = pl.program_id(0)
    last = pl.num_programs(0) - 1
    bufs = (buf_a, buf_b)

    def start_row(block, slot, r):
        for k in range(TOP_K):
            src = pos_ref[k * n_tok + block * tm + r]
            _token_copy(ys_ref, src, bufs[slot].at[k], r, sem.at[slot], chunks).start(priority=k % 2)

    def drain(slot):
        def body(r, carry):
            for k in range(TOP_K):
                _token_copy(ys_ref, 0, bufs[slot].at[k], 0, sem.at[slot], chunks).wait()
            return carry
        lax.fori_loop(0, tm, body, 0, unroll=DMA_UNROLL)

    @pl.when(i == 0)
    def _():
        def body(r, carry):
            start_row(0, 0, r)
            return carry
        lax.fori_loop(0, tm, body, 0, unroll=DMA_UNROLL)

    def step(slot):
        drain(slot)
        for r in range(tm):
            start_row(jnp.minimum(i + 1, last), 1 - slot, r)
        rec = route_ref[...]
        w0 = rec[:, R_W0:R_W0 + 1]
        w1 = rec[:, R_W1:R_W1 + 1]
        y0 = _load_token_tiles(bufs[slot].at[0], tm, chunks)
        y1 = _load_token_tiles(bufs[slot].at[1], tm, chunks)
        x2 = x1_ref[...] + (y0 * w0 + y1 * w1)
        n = _rms(x2) * gp_ref[...]
        gate = _sigmoid(jnp.dot(n.astype(BF16), wg_ref[...], preferred_element_type=F32) + bg_ref[...])
        pe = jnp.dot(p_ref[...].astype(BF16), wp_ref[...], preferred_element_type=F32)
        o_ref[...] = x2 + pe * gate

        @pl.when(i == last)
        def _():
            drain(1 - slot)

    for slot in range(2):
        pl.when(i % 2 == slot)(functools.partial(step, slot))


def _combine_ple(pos, x1, route, p3, ys, g_ple, w_gate, b_gate, w_proj, layer):
    t, d = x1.shape
    dp = p3.shape[-1]
    tm = COMBINE_BLOCK
    row = lambda i, pos_ref: (i, 0)
    const = lambda i, pos_ref: (0, 0)
    return pl.pallas_call(
        functools.partial(_combine_kernel, n_tok=t),
        grid_spec=pltpu.PrefetchScalarGridSpec(
            num_scalar_prefetch=1,
            grid=(t // tm,),
            in_specs=[
                pl.BlockSpec((tm, d), row),
                pl.BlockSpec((tm, LANES), row),
                pl.BlockSpec((None, tm, dp), lambda i, pos_ref: (layer, i, 0)),
                pl.BlockSpec(memory_space=pl.ANY),
                pl.BlockSpec((1, d), const),
                pl.BlockSpec((d, d), const),
                pl.BlockSpec((1, d), const),
                pl.BlockSpec((dp, d), const),
            ],
            out_specs=pl.BlockSpec((tm, d), row),
            scratch_shapes=[
                pltpu.VMEM((TOP_K, tm * (d // LANES), LANES), F32),
                pltpu.VMEM((TOP_K, tm * (d // LANES), LANES), F32),
                pltpu.SemaphoreType.DMA((2,)),
            ],
        ),
        out_shape=jax.ShapeDtypeStruct((t, d), F32),
        compiler_params=_params("arbitrary"),
        name="combine_ple",
    )(pos, x1, route, p3, ys, g_ple, w_gate, b_gate, w_proj)


def _block_diag(w):
    nb, n, _ = w.shape
    eye = jnp.eye(nb, dtype=w.dtype)
    return jnp.einsum("hij,hg->higj", w, eye).reshape(nb * n, nb * n)


def kernel(x, p, mix_norm, w_in, b_forget, conv_w, conv_b, w_rgate, b_rgate, w_igate, b_igate, lru_lambda,
           q_norm, k_norm, lru_out_norm, att_out_norm, w_out, ffn_norm, w_group, b_group, w_router, b_router,
           w_exp_gate, w_exp_up, w_exp_down, ple_norm, w_ple_gate, b_ple_gate, w_ple_proj):
    batch, seq, d = x.shape
    depth = w_in.shape[0]
    d_rnn = conv_w.shape[-1]
    n_heads = b_forget.shape[-1]
    d_att = n_heads * HEAD_DIM
    t = batch * seq
    assert n_heads == SUBLANES and d_att % LANES == 0 and seq % SEQ_BLOCK == 0 and seq % Q_BLOCK == 0
    assert HEADS_PER_BLOCK == 2
    assert Q_BLOCK == SEQ_BLOCK and (TOP_K * t) % FFN_TILE == 0
    assert t % DISPATCH_BLOCK == 0 and t % COMBINE_BLOCK == 0
    assert w_router.shape[-1] == N_EXPERTS and w_group.shape[-1] == N_GROUPS

    idx = jnp.arange(SEQ_BLOCK)
    tri_incl = (idx[:, None] <= idx[None, :]).astype(BF16)
    tri_strict = (idx[None, :] < idx[:, None]).astype(BF16)
    hid = jnp.arange(d_att) // HEAD_DIM
    gsum = (hid[:, None] == hid[None, :]).astype(BF16)
    row2 = lambda a: a.reshape(1, -1).astype(F32)
    n_in_pad = 2 * d_rnn + 3 * d_att + LANES
    p3 = p.reshape(depth, t, p.shape[-1])

    x2 = x.reshape(t, d)
    for i in range(depth):
        w_pad = jnp.pad(w_in[i], ((0, 0), (0, n_in_pad - w_in.shape[-1]))).astype(BF16)
        qg = row2(jnp.tile(q_norm[i], n_heads) * (LOG2E / math.sqrt(HEAD_DIM)))
        kg = row2(jnp.tile(k_norm[i], n_heads))
        zx, zg, q, k, v, c = _in_proj(
            x2, row2(mix_norm[i]), w_pad, gsum, qg, kg, b_forget[i].reshape(n_heads, 1).astype(F32), tri_incl,
            batch=batch, seq=seq, d_rnn=d_rnn, d_att=d_att)

        w_gates = jnp.concatenate([_block_diag(w_rgate[i]), _block_diag(w_igate[i])], axis=1).astype(BF16)
        yr = _rglru(zx.reshape(batch, seq, d_rnn), zg.reshape(batch, seq, d_rnn), conv_w[i].astype(F32),
                    row2(conv_b[i]), w_gates, row2(b_rgate[i]), row2(b_igate[i]), row2(lru_lambda[i]),
                    row2(lru_out_norm[i]))
        ya = _attention(q.reshape(batch, seq, d_att), k.reshape(batch, seq, d_att),
                        v.reshape(batch, seq, HEADS_PER_BLOCK * d_att), c)

        w_rt = jnp.pad(jnp.concatenate([w_router[i], w_group[i]], axis=1),
                       ((0, 0), (0, LANES - N_EXPERTS - N_GROUPS))).astype(F32)
        wr_hi, wr_lo = _split_bf16(w_rt)
        b_rt = jnp.pad(jnp.concatenate([b_router[i], b_group[i]]), (0, LANES - N_EXPERTS - N_GROUPS))
        x1, xn, route, meta, cnt = _out_route(
            x2, yr.reshape(t, d_rnn), ya.reshape(t, d_att), row2(att_out_norm[i]), w_out[i].astype(BF16),
            row2(ffn_norm[i]), wr_hi, wr_lo, row2(b_rt), tri_strict, d_rnn=d_rnn)

        cnt_e = cnt[0, :N_EXPERTS].astype(jnp.int32)
        off_e = jnp.cumsum(cnt_e) - cnt_e
        pos = _positions(off_e, meta).reshape(-1)

        xs = _dispatch(pos, xn, t)
        ys = _expert_ffn(_ffn_schedule(cnt_e, off_e, TOP_K * t), xs, w_exp_gate, w_exp_up, w_exp_down, i)
        x2 = _combine_ple(pos, x1, route, p3, ys, row2(ple_norm[i]), w_ple_gate[i].astype(BF16),
                          row2(b_ple_gate[i]), w_ple_proj[i].astype(BF16), i)
    return x2.reshape(batch, seq, d)
```

```python
import functools
import math

import jax
import jax.numpy as jnp
from jax import lax
from jax.experimental import pallas as pl
from jax.experimental.pallas import tpu as pltpu

F32 = jnp.float32
BF16 = jnp.bfloat16

EPS = 1e-6
LRU_C = 8.0
HEAD_DIM = 64
CONV_WIDTH = 4
N_GROUPS = 4
EXPERTS_PER_GROUP = 8
N_EXPERTS = N_GROUPS * EXPERTS_PER_GROUP
TOP_K = 2

LOG2E = math.log2(math.e)

LANES = 128
SUBLANES = 8
VMEM_LIMIT = 56 * 1024 * 1024
HEADS_PER_BLOCK = LANES // HEAD_DIM
VT_ROWS = HEAD_DIM + 16
C_PIECES = 3

SEQ_BLOCK = 512
Q_BLOCK = 512
FFN_TILE = 256
DISPATCH_BLOCK = 1024
COMBINE_BLOCK = 256
DMA_UNROLL = 8

R_ID0, R_ID1, R_W0, R_W1, R_RANK0, R_RANK1 = 0, 1, 2, 3, 4, 5


def _sigmoid(x):
    return 1.0 / (1.0 + jnp.exp(-x))


def _softplus(x):
    return jnp.maximum(x, 0.0) + jnp.log1p(jnp.exp(-jnp.abs(x)))


def _gelu_tanh(x):
    c = math.sqrt(2.0 / math.pi)
    return x * (0.5 * (1.0 + jnp.tanh(c * (x + 0.044715 * (x * x * x)))))


def _rms(x):
    return x * lax.rsqrt(jnp.mean(x * x, axis=-1, keepdims=True) + EPS)


def _split_bf16(x):
    hi = x.astype(BF16)
    lo = (x - hi.astype(F32)).astype(BF16)
    return hi, lo


def _store_token_tiles(ref, x):
    n, d = x.shape
    chunks = d // LANES
    for j in range(chunks):
        ref[pl.ds(j, n, stride=chunks), :] = x[:, j * LANES:(j + 1) * LANES]


def _load_token_tiles(ref, n, chunks):
    return jnp.concatenate([ref[pl.ds(j, n, stride=chunks), :] for j in range(chunks)], axis=1)


def _token_copy(src_ref, src_tok, dst_ref, dst_tok, sem, chunks):
    def rows(tok):
        start = tok * chunks
        return pl.ds(start if isinstance(tok, int) else pl.multiple_of(start, chunks), chunks)
    return pltpu.make_async_copy(src_ref.at[rows(src_tok), :], dst_ref.at[rows(dst_tok), :], sem)


def _params(*sem):
    return pltpu.CompilerParams(dimension_semantics=sem, vmem_limit_bytes=VMEM_LIMIT)


def _in_proj_kernel(x_ref, g_ref, w_ref, gsum_ref, qg_ref, kg_ref, bf_ref, tri_ref,
                    zx_ref, zg_ref, q_ref, k_ref, vt_ref, cp_ref, carry_ref, *, blocks_per_seq, d_rnn, d_att):
    i = pl.program_id(0)
    tm = x_ref.shape[0]
    n = _rms(x_ref[...]) * g_ref[...]
    z = jnp.dot(n.astype(BF16), w_ref[...], preferred_element_type=F32)
    o = 0
    zx_ref[...] = z[:, o:o + d_rnn]; o += d_rnn
    zg_ref[...] = z[:, o:o + d_rnn]; o += d_rnn
    zq = z[:, o:o + d_att]; o += d_att
    zk = z[:, o:o + d_att]; o += d_att
    zv = z[:, o:o + d_att]; o += d_att
    zf = z[:, o:o + LANES]

    zvt = zv.T
    ones = jnp.ones((VT_ROWS - HEAD_DIM, tm), F32)
    for h in range(d_att // HEAD_DIM):
        vt_ref[0, h, 0] = jnp.concatenate([zvt[h * HEAD_DIM:(h + 1) * HEAD_DIM], ones], axis=0).astype(BF16)

    def head_norm(t, gain):
        ss = jnp.dot((t * t).astype(BF16), gsum_ref[...], preferred_element_type=F32)
        return t * lax.rsqrt(ss * (1.0 / HEAD_DIM) + EPS) * gain

    q_ref[...] = head_norm(zq, qg_ref[...]).astype(BF16)
    k_ref[...] = head_norm(zk, kg_ref[...]).astype(BF16)

    zft = zf.T[:SUBLANES, :] + bf_ref[...]
    logf = jnp.minimum(zft, 0.0) - jnp.log1p(jnp.exp(-jnp.abs(zft)))
    hi, lo = _split_bf16(logf)
    cs = (jnp.dot(hi, tri_ref[...], preferred_element_type=F32)
          + jnp.dot(lo, tri_ref[...], preferred_element_type=F32))

    @pl.when(i % blocks_per_seq == 0)
    def _():
        carry_ref[...] = jnp.zeros_like(carry_ref)

    c = cs + carry_ref[:, 0:1]
    carry_ref[...] = jnp.broadcast_to(c[:, tm - 1:tm], carry_ref.shape)

    cneg = c * (-LOG2E)
    pieces = []
    for _ in range(C_PIECES):
        piece = cneg.astype(BF16).astype(F32)
        pieces.append(piece)
        cneg = cneg - piece
    pieces.append(jnp.zeros((LANES - C_PIECES * SUBLANES, tm), F32))
    cp_ref[...] = jnp.concatenate(pieces, axis=0).T.astype(BF16)


def _in_proj(x2, g_mix, w_pad, gsum, qg, kg, bf, tri, *, batch, seq, d_rnn, d_att):
    t, d = x2.shape
    tm = SEQ_BLOCK
    bps = seq // tm
    n_pad = w_pad.shape[1]
    row = lambda i: (i, 0)
    const = lambda i: (0, 0)
    out_shape = (
        jax.ShapeDtypeStruct((t, d_rnn), F32),
        jax.ShapeDtypeStruct((t, d_rnn), F32),
        jax.ShapeDtypeStruct((t, d_att), BF16),
        jax.ShapeDtypeStruct((t, d_att), BF16),
        jax.ShapeDtypeStruct((batch, d_att // HEAD_DIM, bps, VT_ROWS, tm), BF16),
        jax.ShapeDtypeStruct((t, LANES), BF16),
    )
    return pl.pallas_call(
        functools.partial(_in_proj_kernel, blocks_per_seq=bps, d_rnn=d_rnn, d_att=d_att),
        grid=(t // tm,),
        in_specs=[
            pl.BlockSpec((tm, d), row),
            pl.BlockSpec((1, d), const),
            pl.BlockSpec((d, n_pad), const),
            pl.BlockSpec((d_att, d_att), const),
            pl.BlockSpec((1, d_att), const),
            pl.BlockSpec((1, d_att), const),
            pl.BlockSpec((SUBLANES, 1), const),
            pl.BlockSpec((tm, tm), const),
        ],
        out_specs=(
            pl.BlockSpec((tm, d_rnn), row),
            pl.BlockSpec((tm, d_rnn), row),
            pl.BlockSpec((tm, d_att), row),
            pl.BlockSpec((tm, d_att), row),
            pl.BlockSpec((1, d_att // HEAD_DIM, 1, VT_ROWS, tm), lambda i: (i // bps, 0, i % bps, 0, 0)),
            pl.BlockSpec((tm, LANES), row),
        ),
        out_shape=out_shape,
        scratch_shapes=[pltpu.VMEM((SUBLANES, LANES), F32)],
        compiler_params=_params("arbitrary"),
        name="in_proj",
    )(x2, g_mix, w_pad, gsum, qg, kg, bf, tri)


def _rglru_kernel(zx_ref, zg_ref, cw_ref, cb_ref, wg_ref, br_ref, bi_ref, lam_ref, go_ref,
                  y_ref, tail_ref, h_ref, a_s, u_s, hs_s, *, d_rnn):
    j = pl.program_id(1)
    ts = zx_ref.shape[1]

    @pl.when(j == 0)
    def _():
        tail_ref[...] = jnp.zeros_like(tail_ref)
        h_ref[...] = jnp.zeros_like(h_ref)

    x = zx_ref[0]
    tail = tail_ref[...]
    row8 = lax.broadcasted_iota(jnp.int32, (SUBLANES, d_rnn), 0)

    xc = x * cw_ref[CONV_WIDTH - 1:CONV_WIDTH, :] + cb_ref[...]
    for d in range(1, CONV_WIDTH):
        xr = pltpu.roll(x, d, 0)
        top = jnp.where(row8 < d, pltpu.roll(tail, d, 0), xr[:SUBLANES])
        xs = jnp.concatenate([top, xr[SUBLANES:]], axis=0)
        xc = xc + xs * cw_ref[CONV_WIDTH - 1 - d:CONV_WIDTH - d, :]
    tail_ref[...] = x[ts - SUBLANES:, :]

    gl = jnp.dot(xc.astype(BF16), wg_ref[...], preferred_element_type=F32)
    r = _sigmoid(gl[:, :d_rnn] + br_ref[...])
    ig = _sigmoid(gl[:, d_rnn:] + bi_ref[...])
    log_a = (-LRU_C) * r * _softplus(-lam_ref[...])
    a = jnp.exp(log_a)
    a_s[...] = a
    u_s[...] = jnp.sqrt(1.0 - a * a) * (ig * xc)

    def group(gidx, h):
        r0 = pl.multiple_of(gidx * SUBLANES, SUBLANES)
        a8 = a_s[pl.ds(r0, SUBLANES), :]
        b8 = u_s[pl.ds(r0, SUBLANES), :]
        for d in (1, 2, 4):
            keep = row8 >= d
            a_sh = jnp.where(keep, pltpu.roll(a8, d, 0), 1.0)
            b_sh = jnp.where(keep, pltpu.roll(b8, d, 0), 0.0)
            b8 = a8 * b_sh + b8
            a8 = a8 * a_sh
        h8 = a8 * h + b8
        hs_s[pl.ds(r0, SUBLANES), :] = h8
        return h8[SUBLANES - 1:SUBLANES, :]

    h_last = lax.fori_loop(0, ts // SUBLANES, group, h_ref[0:1, :], unroll=4)
    h_ref[...] = jnp.broadcast_to(h_last, h_ref.shape)

    y = hs_s[...] * _gelu_tanh(zg_ref[0])
    y_ref[0] = (_rms(y) * go_ref[...]).astype(BF16)


def _rglru(zx, zg, conv_w, conv_b, w_gates, b_r, b_i, lam, g_out):
    b, s, d_rnn = zx.shape
    ts = SEQ_BLOCK
    blk = lambda bi, j: (bi, j, 0)
    const = lambda bi, j: (0, 0)
    return pl.pallas_call(
        functools.partial(_rglru_kernel, d_rnn=d_rnn),
        grid=(b, s // ts),
        in_specs=[
            pl.BlockSpec((1, ts, d_rnn), blk),
            pl.BlockSpec((1, ts, d_rnn), blk),
            pl.BlockSpec((CONV_WIDTH, d_rnn), const),
            pl.BlockSpec((1, d_rnn), const),
            pl.BlockSpec((d_rnn, 2 * d_rnn), const),
            pl.BlockSpec((1, d_rnn), const),
            pl.BlockSpec((1, d_rnn), const),
            pl.BlockSpec((1, d_rnn), const),
            pl.BlockSpec((1, d_rnn), const),
        ],
        out_specs=pl.BlockSpec((1, ts, d_rnn), blk),
        out_shape=jax.ShapeDtypeStruct((b, s, d_rnn), BF16),
        scratch_shapes=[
            pltpu.VMEM((SUBLANES, d_rnn), F32),
            pltpu.VMEM((SUBLANES, d_rnn), F32),
            pltpu.VMEM((ts, d_rnn), F32),
            pltpu.VMEM((ts, d_rnn), F32),
            pltpu.VMEM((ts, d_rnn), F32),
        ],
        compiler_params=_params("arbitrary", "arbitrary"),
        name="rglru",
    )(zx, zg, conv_w, conv_b, w_gates, b_r, b_i, lam, g_out)


def _attn_kernel(q_ref, k_ref, cp_ref, vt_ref, o_ref, sa_ref, sb_ref, *, bq, bk):
    hp = pl.program_id(1)
    qi = pl.program_id(2)
    q2 = q_ref[0]
    lane = lax.broadcasted_iota(jnp.int32, (bq, LANES), 1)
    key = lax.broadcasted_iota(jnp.int32, (bk, bq), 0)
    query = qi * bq + lax.broadcasted_iota(jnp.int32, (bk, bq), 1)
    n_full = qi * (bq // bk)
    qx = []
    for j in range(HEADS_PER_BLOCK):
        head = hp * HEADS_PER_BLOCK + j
        qj = jnp.where(lane // HEAD_DIM == j, q2, jnp.zeros_like(q2))
        sel = jnp.where((lane < C_PIECES * SUBLANES) & (lane % SUBLANES == head), 1.0, 0.0).astype(BF16)
        qx.append(jnp.concatenate([qj, sel], axis=1))

    def scores(kb, s_ref):
        k0 = pl.multiple_of(kb * bk, bk)
        kx = jnp.concatenate([k_ref[0, pl.ds(k0, bk), :], cp_ref[0, pl.ds(k0, bk), :]], axis=1)
        for j in range(HEADS_PER_BLOCK):
            s_ref[j] = lax.dot_general(kx, qx[j], (((1,), (1,)), ((), ())), preferred_element_type=F32)

    def update(kb, s_ref, stats, masked):
        k0 = pl.multiple_of(kb * bk, bk)
        new = []
        for j in range(HEADS_PER_BLOCK):
            m, acc = stats[j]
            s = s_ref[j]
            if masked:
                s = jnp.where(key + k0 <= query, s, -jnp.inf)
            m_new = jnp.maximum(m, jnp.max(s, axis=0, keepdims=True))
            p = jnp.exp2(s - m_new)
            acc = jnp.exp2(m - m_new) * acc + jnp.dot(vt_ref[0, j, kb], p.astype(BF16),
                                                      preferred_element_type=F32)
            new.append((m_new, acc))
        return tuple(new)

    def finish(stats):
        halves = [acc[:HEAD_DIM] / acc[HEAD_DIM:HEAD_DIM + 1] for _, acc in stats]
        o_ref[0] = jnp.concatenate(halves, axis=0).T.astype(o_ref.dtype)

    def pair(i, stats):
        scores(2 * i + 1, sb_ref)
        stats = update(2 * i, sa_ref, stats, False)
        scores(2 * i + 2, sa_ref)
        return update(2 * i + 1, sb_ref, stats, False)

    stats = tuple((jnp.full((1, bq), -jnp.inf, F32), jnp.zeros((VT_ROWS, bq), F32)) for _ in range(HEADS_PER_BLOCK))
    scores(0, sa_ref)
    stats = lax.fori_loop(0, n_full // 2, pair, stats)

    @pl.when(n_full % 2 == 0)
    def _():
        finish(update(n_full, sa_ref, stats, True))

    @pl.when(n_full % 2 == 1)
    def _():
        scores(n_full, sb_ref)
        st = update(n_full - 1, sa_ref, stats, False)
        finish(update(n_full, sb_ref, st, True))


def _attention(q, k, cp, vt):
    b, s, d_att = q.shape
    bq, bk = Q_BLOCK, SEQ_BLOCK
    return pl.pallas_call(
        functools.partial(_attn_kernel, bq=bq, bk=bk),
        grid=(b, d_att // LANES, s // bq),
        in_specs=[
            pl.BlockSpec((1, bq, LANES), lambda bi, hp, qi: (bi, qi, hp)),
            pl.BlockSpec((1, s, LANES), lambda bi, hp, qi: (bi, 0, hp)),
            pl.BlockSpec((1, s, LANES), lambda bi, hp, qi: (bi, 0, 0)),
            pl.BlockSpec((1, HEADS_PER_BLOCK, s // bk, VT_ROWS, bk), lambda bi, hp, qi: (bi, hp, 0, 0, 0)),
        ],
        out_specs=pl.BlockSpec((1, bq, LANES), lambda bi, hp, qi: (bi, qi, hp)),
        out_shape=jax.ShapeDtypeStruct((b, s, d_att), BF16),
        scratch_shapes=[pltpu.VMEM((HEADS_PER_BLOCK, bk, bq), F32), pltpu.VMEM((HEADS_PER_BLOCK, bk, bq), F32)],
        compiler_params=_params("arbitrary", "arbitrary", "arbitrary"),
        name="attention",
    )(q, k, cp, vt)


def _out_route_kernel(x_ref, yr_ref, ya_ref, ga_ref, wo_ref, gf_ref, wrh_ref, wrl_ref, br_ref, ltri_ref,
                      x1_ref, xn_ref, route_ref, meta_ref, cnt_ref, carry_ref, *, d_rnn):
    i = pl.program_id(0)
    tm = x_ref.shape[0]

    @pl.when(i == 0)
    def _():
        carry_ref[...] = jnp.zeros_like(carry_ref)

    ya = _rms(ya_ref[...].astype(F32)) * ga_ref[...]
    x1 = (x_ref[...]
          + jnp.dot(yr_ref[...], wo_ref[:d_rnn, :], preferred_element_type=F32)
          + jnp.dot(ya.astype(BF16), wo_ref[d_rnn:, :], preferred_element_type=F32))
    x1_ref[...] = x1
    xn = _rms(x1) * gf_ref[...]
    _store_token_tiles(xn_ref, xn)

    nh, nl = _split_bf16(xn)
    logits = (jnp.dot(nh, wrh_ref[...], preferred_element_type=F32)
              + jnp.dot(nl, wrh_ref[...], preferred_element_type=F32)
              + jnp.dot(nh, wrl_ref[...], preferred_element_type=F32)) + br_ref[...]

    lane = lax.broadcasted_iota(jnp.int32, (tm, LANES), 1)
    lane_f = lane.astype(F32)
    big = float(LANES)

    def top1(vals):
        m = jnp.max(vals, axis=-1, keepdims=True)
        idx = jnp.min(jnp.where(vals == m, lane_f, big), axis=-1, keepdims=True)
        return m, idx

    is_group = (lane >= N_EXPERTS) & (lane < N_EXPERTS + N_GROUPS)
    gl = jnp.where(is_group, logits, -jnp.inf)
    g_max, g_lane = top1(gl)
    g_p = 1.0 / jnp.sum(jnp.exp(gl - g_max), axis=-1, keepdims=True)
    g_idx = g_lane - float(N_EXPERTS)

    in_group = (lane < N_EXPERTS) & ((lane // EXPERTS_PER_GROUP).astype(F32) == g_idx)
    el = jnp.where(in_group, logits, -jnp.inf)
    m1, i1 = top1(el)
    el2 = jnp.where(lane_f == i1, -jnp.inf, el)
    m2, i2 = top1(el2)
    e2 = jnp.exp(m2 - m1)
    w1 = g_p / (1.0 + e2)
    w2 = g_p * e2 / (1.0 + e2)

    onehot = jnp.where((lane_f == i1) | (lane_f == i2), 1.0, 0.0)
    before = jnp.dot(ltri_ref[...], onehot.astype(BF16), preferred_element_type=F32) + carry_ref[0:1, :]
    rank1 = jnp.sum(jnp.where(lane_f == i1, before, 0.0), axis=-1, keepdims=True)
    rank2 = jnp.sum(jnp.where(lane_f == i2, before, 0.0), axis=-1, keepdims=True)
    carry = carry_ref[0:1, :] + jnp.sum(onehot, axis=0, keepdims=True)
    carry_ref[...] = jnp.broadcast_to(carry, carry_ref.shape)
    cnt_ref[...] = jnp.broadcast_to(carry, cnt_ref.shape)

    rec = jnp.zeros((tm, LANES), F32)
    for ln, val in ((R_ID0, i1), (R_ID1, i2), (R_W0, w1), (R_W1, w2), (R_RANK0, rank1), (R_RANK1, rank2)):
        rec = jnp.where(lane == ln, val, rec)
    route_ref[...] = rec
    meta_ref[...] = rec.T[:SUBLANES, :]


def _out_route(x2, yr, ya, g_att, w_out, g_ffn, wr_hi, wr_lo, b_rt, ltri, *, d_rnn):
    t, d = x2.shape
    tm = SEQ_BLOCK
    d_att = ya.shape[1]
    row = lambda i: (i, 0)
    const = lambda i: (0, 0)
    return pl.pallas_call(
        functools.partial(_out_route_kernel, d_rnn=d_rnn),
        grid=(t // tm,),
        in_specs=[
            pl.BlockSpec((tm, d), row),
            pl.BlockSpec((tm, d_rnn), row),
            pl.BlockSpec((tm, d_att), row),
            pl.BlockSpec((1, d_att), const),
            pl.BlockSpec((d_rnn + d_att, d), const),
            pl.BlockSpec((1, d), const),
            pl.BlockSpec((d, LANES), const),
            pl.BlockSpec((d, LANES), const),
            pl.BlockSpec((1, LANES), const),
            pl.BlockSpec((tm, tm), const),
        ],
        out_specs=(
            pl.BlockSpec((tm, d), row),
            pl.BlockSpec((tm * (d // LANES), LANES), row),
            pl.BlockSpec((tm, LANES), row),
            pl.BlockSpec((SUBLANES, tm), lambda i: (0, i)),
            pl.BlockSpec((SUBLANES, LANES), const),
        ),
        out_shape=(
            jax.ShapeDtypeStruct((t, d), F32),
            jax.ShapeDtypeStruct((t * (d // LANES), LANES), F32),
            jax.ShapeDtypeStruct((t, LANES), F32),
            jax.ShapeDtypeStruct((SUBLANES, t), F32),
            jax.ShapeDtypeStruct((SUBLANES, LANES), F32),
        ),
        scratch_shapes=[pltpu.VMEM((SUBLANES, LANES), F32)],
        compiler_params=_params("arbitrary"),
        name="out_route",
    )(x2, yr, ya, g_att, w_out, g_ffn, wr_hi, wr_lo, b_rt, ltri)


def _positions_kernel(off_ref, meta_ref, pos_ref):
    m = meta_ref[...]
    ids = m[R_ID0:R_ID1 + 1, :]
    acc = m[R_RANK0:R_RANK1 + 1, :]
    for e in range(N_EXPERTS):
        acc = acc + jnp.where(ids == float(e), off_ref[e].astype(F32), 0.0)
    pos_ref[...] = acc.astype(jnp.int32)


def _positions(off, meta):
    t = meta.shape[1]
    return pl.pallas_call(
        _positions_kernel,
        grid_spec=pltpu.PrefetchScalarGridSpec(
            num_scalar_prefetch=1,
            grid=(1,),
            in_specs=[pl.BlockSpec((SUBLANES, t), lambda i, off_ref: (0, 0))],
            out_specs=pl.BlockSpec((TOP_K, t), lambda i, off_ref: (0, 0)),
        ),
        out_shape=jax.ShapeDtypeStruct((TOP_K, t), jnp.int32),
        compiler_params=_params("arbitrary"),
        name="positions",
    )(off, meta)


def _dispatch_kernel(pos_ref, xn_ref, xs_ref, sem, *, n_tok, tm, chunks):
    base = pl.program_id(0) * tm

    def issue(r, carry):
        for k in range(TOP_K):
            _token_copy(xn_ref, r, xs_ref, pos_ref[k * n_tok + base + r], sem, chunks).start(priority=k % 2)
        return carry

    def drain(r, carry):
        for k in range(TOP_K):
            _token_copy(xn_ref, 0, xs_ref, 0, sem, chunks).wait()
        return carry

    lax.fori_loop(0, tm, issue, 0, unroll=DMA_UNROLL)
    lax.fori_loop(0, tm, drain, 0, unroll=DMA_UNROLL)


def _dispatch(pos, xn_tiles, t):
    chunks = xn_tiles.shape[0] // t
    tm = DISPATCH_BLOCK
    return pl.pallas_call(
        functools.partial(_dispatch_kernel, n_tok=t, tm=tm, chunks=chunks),
        grid_spec=pltpu.PrefetchScalarGridSpec(
            num_scalar_prefetch=1,
            grid=(t // tm,),
            in_specs=[pl.BlockSpec((tm * chunks, LANES), lambda i, pos_ref: (i, 0))],
            out_specs=pl.BlockSpec(memory_space=pl.ANY),
            scratch_shapes=[pltpu.SemaphoreType.DMA(())],
        ),
        out_shape=jax.ShapeDtypeStruct((TOP_K * t * chunks, LANES), F32),
        compiler_params=_params("arbitrary"),
        name="dispatch",
    )(pos, xn_tiles)


def _ffn_kernel(tile_ref, exp_ref, lo_ref, hi_ref, first_ref, init_ref,
                xs_ref, wg_ref, wu_ref, wd_ref, ys_ref, wg_s, wu_s, wd_s, *, tm):
    g = pl.program_id(0)
    chunks = xs_ref.shape[0] // tm

    @pl.when(first_ref[g] == 1)
    def _():
        wg_s[...] = wg_ref[...].astype(BF16)
        wu_s[...] = wu_ref[...].astype(BF16)
        wd_s[...] = wd_ref[...].astype(BF16)

    lo = lo_ref[g]
    hi = hi_ref[g]

    @pl.when(hi > lo)
    def _():
        x = _load_token_tiles(xs_ref, tm, chunks).astype(BF16)
        hg = jnp.dot(x, wg_s[...], preferred_element_type=F32)
        hu = jnp.dot(x, wu_s[...], preferred_element_type=F32)
        h = (hg * _sigmoid(hg)) * hu
        y = jnp.dot(h.astype(BF16), wd_s[...], preferred_element_type=F32)
        row = lax.broadcasted_iota(jnp.int32, y.shape, 0)
        mine = (row >= lo) & (row < hi)

        @pl.when(init_ref[g] == 1)
        def _():
            _store_token_tiles(ys_ref, jnp.where(mine, y, 0.0))

        @pl.when(init_ref[g] == 0)
        def _():
            _store_token_tiles(ys_ref, jnp.where(mine, y, _load_token_tiles(ys_ref, tm, chunks)))


def _expert_ffn(meta, xs, w_gate, w_up, w_down, layer):
    d, d_e = w_gate.shape[-2:]
    chunks = d // LANES
    tm = FFN_TILE
    steps = meta[0].shape[0]
    x_map = lambda g, tile, exp, lo, hi, first, init: (tile[g], 0)
    w_map = lambda g, tile, exp, lo, hi, first, init: (layer, exp[g], 0, 0)
    return pl.pallas_call(
        functools.partial(_ffn_kernel, tm=tm),
        grid_spec=pltpu.PrefetchScalarGridSpec(
            num_scalar_prefetch=6,
            grid=(steps,),
            in_specs=[
                pl.BlockSpec((tm * chunks, LANES), x_map),
                pl.BlockSpec((None, None, d, d_e), w_map),
                pl.BlockSpec((None, None, d, d_e), w_map),
                pl.BlockSpec((None, None, d_e, d), w_map),
            ],
            out_specs=pl.BlockSpec((tm * chunks, LANES), x_map),
            scratch_shapes=[
                pltpu.VMEM((d, d_e), BF16),
                pltpu.VMEM((d, d_e), BF16),
                pltpu.VMEM((d_e, d), BF16),
            ],
        ),
        out_shape=jax.ShapeDtypeStruct(xs.shape, F32),
        compiler_params=_params("arbitrary"),
        name="expert_ffn",
    )(*meta, xs, w_gate, w_up, w_down)


def _ffn_schedule(cnt, off, rows):
    tm = FFN_TILE
    n_tiles = rows // tm
    steps = n_tiles + N_EXPERTS - 1
    end = off + cnt
    first_tile = off // tm
    last_tile = jnp.where(cnt > 0, (end - 1) // tm, first_tile - 1)
    visits = last_tile - first_tile + 1
    vis_end = jnp.cumsum(visits)
    vis_start = vis_end - visits
    total = vis_end[-1]
    g = jnp.arange(steps, dtype=jnp.int32)
    gc = jnp.minimum(g, total - 1)
    exp = jnp.searchsorted(vis_end, gc, side="right").astype(jnp.int32)
    tile = first_tile[exp] + (gc - vis_start[exp])
    valid = g < total
    lo = jnp.where(valid, jnp.maximum(off[exp], tile * tm) - tile * tm, 0)
    hi = jnp.where(valid, jnp.minimum(end[exp], (tile + 1) * tm) - tile * tm, 0)
    prev_exp = jnp.concatenate([jnp.full((1,), -1, jnp.int32), exp[:-1]])
    prev_tile = jnp.concatenate([jnp.full((1,), -1, jnp.int32), tile[:-1]])
    first = (valid & (exp != prev_exp)).astype(jnp.int32)
    init = (valid & (tile != prev_tile)).astype(jnp.int32)
    i32 = lambda a: a.astype(jnp.int32)
    return i32(tile), i32(exp), i32(lo), i32(hi), first, init


def _combine_kernel(pos_ref, x1_ref, route_ref, p_ref, ys_ref, gp_ref, wg_ref, bg_ref, wp_ref,
                    o_ref, buf_a, buf_b, sem, *, n_tok):
    tm, d = x1_ref.shape
    chunks = d // LANES
    i = pl.program_id(0)
    last = pl.num_programs(0) - 1
    bufs = (buf_a, buf_b)

    def start_row(block, slot, r):
        for k in range(TOP_K):
            src = pos_ref[k * n_tok + block * tm + r]
            _token_copy(ys_ref, src, bufs[slot].at[k], r, sem.at[slot], chunks).start(priority=k % 2)

    def drain(slot):
        def body(r, carry):
            for k in range(TOP_K):
                _token_copy(ys_ref, 0, bufs[slot].at[k], 0, sem.at[slot], chunks).wait()
            return carry
        lax.fori_loop(0, tm, body, 0, unroll=DMA_UNROLL)

    @pl.when(i == 0)
    def _():
        def body(r, carry):
            start_row(0, 0, r)
            return carry
        lax.fori_loop(0, tm, body, 0, unroll=DMA_UNROLL)

    def step(slot):
        drain(slot)
        for r in range(tm):
            start_row(jnp.minimum(i + 1, last), 1 - slot, r)
        rec = route_ref[...]
        w0 = rec[:, R_W0:R_W0 + 1]
        w1 = rec[:, R_W1:R_W1 + 1]
        y0 = _load_token_tiles(bufs[slot].at[0], tm, chunks)
        y1 = _load_token_tiles(bufs[slot].at[1], tm, chunks)
        x2 = x1_ref[...] + (y0 * w0 + y1 * w1)
        n = _rms(x2) * gp_ref[...]
        gate = _sigmoid(jnp.dot(n.astype(BF16), wg_ref[...], preferred_element_type=F32) + bg_ref[...])
        pe = jnp.dot(p_ref[...].astype(BF16), wp_ref[...], preferred_element_type=F32)
        o_ref[...] = x2 + pe * gate

        @pl.when(i == last)
        def _():
            drain(1 - slot)

    for slot in range(2):
        pl.when(i % 2 == slot)(functools.partial(step, slot))


def _combine_ple(pos, x1, route, p3, ys, g_ple, w_gate, b_gate, w_proj, layer):
    t, d = x1.shape
    dp = p3.shape[-1]
    tm = COMBINE_BLOCK
    row = lambda i, pos_ref: (i, 0)
    const = lambda i, pos_ref: (0, 0)
    return pl.pallas_call(
        functools.partial(_combine_kernel, n_tok=t),
        grid_spec=pltpu.PrefetchScalarGridSpec(
            num_scalar_prefetch=1,
            grid=(t // tm,),
            in_specs=[
                pl.BlockSpec((tm, d), row),
                pl.BlockSpec((tm, LANES), row),
                pl.BlockSpec((None, tm, dp), lambda i, pos_ref: (layer, i, 0)),
                pl.BlockSpec(memory_space=pl.ANY),
                pl.BlockSpec((1, d), const),
                pl.BlockSpec((d, d), const),
                pl.BlockSpec((1, d), const),
                pl.BlockSpec((dp, d), const),
            ],
            out_specs=pl.BlockSpec((tm, d), row),
            scratch_shapes=[
                pltpu.VMEM((TOP_K, tm * (d // LANES), LANES), F32),
                pltpu.VMEM((TOP_K, tm * (d // LANES), LANES), F32),
                pltpu.SemaphoreType.DMA((2,)),
            ],
        ),
        out_shape=jax.ShapeDtypeStruct((t, d), F32),
        compiler_params=_params("arbitrary"),
        name="combine_ple",
    )(pos, x1, route, p3, ys, g_ple, w_gate, b_gate, w_proj)


def _block_diag(w):
    nb, n, _ = w.shape
    eye = jnp.eye(nb, dtype=w.dtype)
    return jnp.einsum("hij,hg->higj", w, eye).reshape(nb * n, nb * n)


def kernel(x, p, mix_norm, w_in, b_forget, conv_w, conv_b, w_rgate, b_rgate, w_igate, b_igate, lru_lambda,
           q_norm, k_norm, lru_out_norm, att_out_norm, w_out, ffn_norm, w_group, b_group, w_router, b_router,
           w_exp_gate, w_exp_up, w_exp_down, ple_norm, w_ple_gate, b_ple_gate, w_ple_proj):
    batch, seq, d = x.shape
    depth = w_in.shape[0]
    d_rnn = conv_w.shape[-1]
    n_heads = b_forget.shape[-1]
    d_att = n_heads * HEAD_DIM
    t = batch * seq
    assert n_heads == SUBLANES and d_att % LANES == 0 and seq % SEQ_BLOCK == 0 and seq % Q_BLOCK == 0
    assert C_PIECES * SUBLANES <= LANES
    assert Q_BLOCK == SEQ_BLOCK and (TOP_K * t) % FFN_TILE == 0
    assert t % DISPATCH_BLOCK == 0 and t % COMBINE_BLOCK == 0
    assert w_router.shape[-1] == N_EXPERTS and w_group.shape[-1] == N_GROUPS

    idx = jnp.arange(SEQ_BLOCK)
    tri_incl = (idx[:, None] <= idx[None, :]).astype(BF16)
    tri_strict = (idx[None, :] < idx[:, None]).astype(BF16)
    hid = jnp.arange(d_att) // HEAD_DIM
    gsum = (hid[:, None] == hid[None, :]).astype(BF16)
    row2 = lambda a: a.reshape(1, -1).astype(F32)
    n_in_pad = 2 * d_rnn + 3 * d_att + LANES
    p3 = p.reshape(depth, t, p.shape[-1])

    x2 = x.reshape(t, d)
    for i in range(depth):
        w_pad = jnp.pad(w_in[i], ((0, 0), (0, n_in_pad - w_in.shape[-1]))).astype(BF16)
        qg = row2(jnp.tile(q_norm[i], n_heads) * (LOG2E / math.sqrt(HEAD_DIM)))
        kg = row2(jnp.tile(k_norm[i], n_heads))
        zx, zg, q, k, vt, cp = _in_proj(
            x2, row2(mix_norm[i]), w_pad, gsum, qg, kg, b_forget[i].reshape(n_heads, 1).astype(F32), tri_incl,
            batch=batch, seq=seq, d_rnn=d_rnn, d_att=d_att)

        w_gates = jnp.concatenate([_block_diag(w_rgate[i]), _block_diag(w_igate[i])], axis=1).astype(BF16)
        yr = _rglru(zx.reshape(batch, seq, d_rnn), zg.reshape(batch, seq, d_rnn), conv_w[i].astype(F32),
                    row2(conv_b[i]), w_gates, row2(b_rgate[i]), row2(b_igate[i]), row2(lru_lambda[i]),
                    row2(lru_out_norm[i]))
        ya = _attention(q.reshape(batch, seq, d_att), k.reshape(batch, seq, d_att),
                        cp.reshape(batch, seq, LANES), vt)

        w_rt = jnp.pad(jnp.concatenate([w_router[i], w_group[i]], axis=1),
                       ((0, 0), (0, LANES - N_EXPERTS - N_GROUPS))).astype(F32)
        wr_hi, wr_lo = _split_bf16(w_rt)
        b_rt = jnp.pad(jnp.concatenate([b_router[i], b_group[i]]), (0, LANES - N_EXPERTS - N_GROUPS))
        x1, xn, route, meta, cnt = _out_route(
            x2, yr.reshape(t, d_rnn), ya.reshape(t, d_att), row2(att_out_norm[i]), w_out[i].astype(BF16),
            row2(ffn_norm[i]), wr_hi, wr_lo, row2(b_rt), tri_strict, d_rnn=d_rnn)

        cnt_e = cnt[0, :N_EXPERTS].astype(jnp.int32)
        off_e = jnp.cumsum(cnt_e) - cnt_e
        pos = _positions(off_e, meta).reshape(-1)

        xs = _dispatch(pos, xn, t)
        ys = _expert_ffn(_ffn_schedule(cnt_e, off_e, TOP_K * t), xs, w_exp_gate, w_exp_up, w_exp_down, i)
        x2 = _combine_ple(pos, x1, route, p3, ys, row2(ple_norm[i]), w_ple_gate[i].astype(BF16),
                          row2(b_ple_gate[i]), w_ple_proj[i].astype(BF16), i)
    return x2.reshape(batch, seq, d)
```

```python
import functools
import math

import jax
import jax.numpy as jnp
from jax import lax
from jax.experimental import pallas as pl
from jax.experimental.pallas import tpu as pltpu

F32 = jnp.float32
BF16 = jnp.bfloat16

EPS = 1e-6
LRU_C = 8.0
HEAD_DIM = 64
CONV_WIDTH = 4
N_GROUPS = 4
EXPERTS_PER_GROUP = 8
N_EXPERTS = N_GROUPS * EXPERTS_PER_GROUP
TOP_K = 2

LOG2E = math.log2(math.e)

LANES = 128
SUBLANES = 8
VMEM_LIMIT = 56 * 1024 * 1024
HEADS_PER_BLOCK = LANES // HEAD_DIM
VT_ROWS = HEAD_DIM + 16
C_PIECES = 3

SEQ_BLOCK = 512
Q_BLOCK = 512
FFN_TILE = 256
DISPATCH_BLOCK = 1024
COMBINE_BLOCK = 256
DMA_UNROLL = 8

R_ID0, R_ID1, R_W0, R_W1, R_RANK0, R_RANK1 = 0, 1, 2, 3, 4, 5


def _sigmoid(x):
    return 1.0 / (1.0 + jnp.exp(-x))


def _softplus(x):
    return jnp.maximum(x, 0.0) + jnp.log1p(jnp.exp(-jnp.abs(x)))


def _gelu_tanh(x):
    c = math.sqrt(2.0 / math.pi)
    return x * (0.5 * (1.0 + jnp.tanh(c * (x + 0.044715 * (x * x * x)))))


def _rms(x):
    return x * lax.rsqrt(jnp.mean(x * x, axis=-1, keepdims=True) + EPS)


def _split_bf16(x):
    hi = x.astype(BF16)
    lo = (x - hi.astype(F32)).astype(BF16)
    return hi, lo


def _store_token_tiles(ref, x):
    n, d = x.shape
    chunks = d // LANES
    for j in range(chunks):
        ref[pl.ds(j, n, stride=chunks), :] = x[:, j * LANES:(j + 1) * LANES]


def _load_token_tiles(ref, n, chunks):
    return jnp.concatenate([ref[pl.ds(j, n, stride=chunks), :] for j in range(chunks)], axis=1)


def _token_copy(src_ref, src_tok, dst_ref, dst_tok, sem, chunks):
    def rows(tok):
        start = tok * chunks
        return pl.ds(start if isinstance(tok, int) else pl.multiple_of(start, chunks), chunks)
    return pltpu.make_async_copy(src_ref.at[rows(src_tok), :], dst_ref.at[rows(dst_tok), :], sem)


def _params(*sem):
    return pltpu.CompilerParams(dimension_semantics=sem, vmem_limit_bytes=VMEM_LIMIT)


def _in_proj_kernel(x_ref, g_ref, w_ref, gsum_ref, qg_ref, kg_ref, bf_ref, tri_ref,
                    zx_ref, zg_ref, q_ref, k_ref, vt_ref, cp_ref, carry_ref, *, blocks_per_seq, d_rnn, d_att):
    i = pl.program_id(0)
    tm = x_ref.shape[0]
    n = _rms(x_ref[...]) * g_ref[...]
    z = jnp.dot(n.astype(BF16), w_ref[...], preferred_element_type=F32)
    o = 0
    zx_ref[...] = z[:, o:o + d_rnn]; o += d_rnn
    zg_ref[...] = z[:, o:o + d_rnn]; o += d_rnn
    zq = z[:, o:o + d_att]; o += d_att
    zk = z[:, o:o + d_att]; o += d_att
    zv = z[:, o:o + d_att]; o += d_att
    zf = z[:, o:o + LANES]

    zvt = zv.T
    ones = jnp.ones((VT_ROWS - HEAD_DIM, tm), F32)
    for h in range(d_att // HEAD_DIM):
        vt_ref[0, h, 0] = jnp.concatenate([zvt[h * HEAD_DIM:(h + 1) * HEAD_DIM], ones], axis=0).astype(BF16)

    def head_norm(t, gain):
        ss = jnp.dot((t * t).astype(BF16), gsum_ref[...], preferred_element_type=F32)
        return t * lax.rsqrt(ss * (1.0 / HEAD_DIM) + EPS) * gain

    q_ref[...] = head_norm(zq, qg_ref[...]).astype(BF16)
    k_ref[...] = head_norm(zk, kg_ref[...]).astype(BF16)

    zft = zf.T[:SUBLANES, :] + bf_ref[...]
    logf = jnp.minimum(zft, 0.0) - jnp.log1p(jnp.exp(-jnp.abs(zft)))
    hi, lo = _split_bf16(logf)
    cs = (jnp.dot(hi, tri_ref[...], preferred_element_type=F32)
          + jnp.dot(lo, tri_ref[...], preferred_element_type=F32))

    @pl.when(i % blocks_per_seq == 0)
    def _():
        carry_ref[...] = jnp.zeros_like(carry_ref)

    c = cs + carry_ref[:, 0:1]
    carry_ref[...] = jnp.broadcast_to(c[:, tm - 1:tm], carry_ref.shape)

    cneg = c * (-LOG2E)
    pieces = []
    for _ in range(C_PIECES):
        piece = cneg.astype(BF16).astype(F32)
        pieces.append(piece)
        cneg = cneg - piece
    pieces.append(jnp.zeros((LANES - C_PIECES * SUBLANES, tm), F32))
    cp_ref[...] = jnp.concatenate(pieces, axis=0).T.astype(BF16)


def _in_proj(x2, g_mix, w_pad, gsum, qg, kg, bf, tri, *, batch, seq, d_rnn, d_att):
    t, d = x2.shape
    tm = SEQ_BLOCK
    bps = seq // tm
    n_pad = w_pad.shape[1]
    row = lambda i: (i, 0)
    const = lambda i: (0, 0)
    out_shape = (
        jax.ShapeDtypeStruct((t, d_rnn), F32),
        jax.ShapeDtypeStruct((t, d_rnn), F32),
        jax.ShapeDtypeStruct((t, d_att), BF16),
        jax.ShapeDtypeStruct((t, d_att), BF16),
        jax.ShapeDtypeStruct((batch, d_att // HEAD_DIM, bps, VT_ROWS, tm), BF16),
        jax.ShapeDtypeStruct((t, LANES), BF16),
    )
    return pl.pallas_call(
        functools.partial(_in_proj_kernel, blocks_per_seq=bps, d_rnn=d_rnn, d_att=d_att),
        grid=(t // tm,),
        in_specs=[
            pl.BlockSpec((tm, d), row),
            pl.BlockSpec((1, d), const),
            pl.BlockSpec((d, n_pad), const),
            pl.BlockSpec((d_att, d_att), const),
            pl.BlockSpec((1, d_att), const),
            pl.BlockSpec((1, d_att), const),
            pl.BlockSpec((SUBLANES, 1), const),
            pl.BlockSpec((tm, tm), const),
        ],
        out_specs=(
            pl.BlockSpec((tm, d_rnn), row),
            pl.BlockSpec((tm, d_rnn), row),
            pl.BlockSpec((tm, d_att), row),
            pl.BlockSpec((tm, d_att), row),
            pl.BlockSpec((1, d_att // HEAD_DIM, 1, VT_ROWS, tm), lambda i: (i // bps, 0, i % bps, 0, 0)),
            pl.BlockSpec((tm, LANES), row),
        ),
        out_shape=out_shape,
        scratch_shapes=[pltpu.VMEM((SUBLANES, LANES), F32)],
        compiler_params=_params("arbitrary"),
        name="in_proj",
    )(x2, g_mix, w_pad, gsum, qg, kg, bf, tri)


def _rglru_kernel(zx_ref, zg_ref, cw_ref, cb_ref, wg_ref, br_ref, bi_ref, lam_ref, go_ref,
                  y_ref, tail_ref, h_ref, a_s, u_s, hs_s, *, d_rnn):
    j = pl.program_id(1)
    ts = zx_ref.shape[1]

    @pl.when(j == 0)
    def _():
        tail_ref[...] = jnp.zeros_like(tail_ref)
        h_ref[...] = jnp.zeros_like(h_ref)

    x = zx_ref[0]
    tail = tail_ref[...]
    row8 = lax.broadcasted_iota(jnp.int32, (SUBLANES, d_rnn), 0)

    xc = x * cw_ref[CONV_WIDTH - 1:CONV_WIDTH, :] + cb_ref[...]
    for d in range(1, CONV_WIDTH):
        xr = pltpu.roll(x, d, 0)
        top = jnp.where(row8 < d, pltpu.roll(tail, d, 0), xr[:SUBLANES])
        xs = jnp.concatenate([top, xr[SUBLANES:]], axis=0)
        xc = xc + xs * cw_ref[CONV_WIDTH - 1 - d:CONV_WIDTH - d, :]
    tail_ref[...] = x[ts - SUBLANES:, :]

    gl = jnp.dot(xc.astype(BF16), wg_ref[...], preferred_element_type=F32)
    r = _sigmoid(gl[:, :d_rnn] + br_ref[...])
    ig = _sigmoid(gl[:, d_rnn:] + bi_ref[...])
    log_a = (-LRU_C) * r * _softplus(-lam_ref[...])
    a = jnp.exp(log_a)
    a_s[...] = a
    u_s[...] = jnp.sqrt(1.0 - a * a) * (ig * xc)

    def group(gidx, h):
        r0 = pl.multiple_of(gidx * SUBLANES, SUBLANES)
        a8 = a_s[pl.ds(r0, SUBLANES), :]
        b8 = u_s[pl.ds(r0, SUBLANES), :]
        for d in (1, 2, 4):
            keep = row8 >= d
            a_sh = jnp.where(keep, pltpu.roll(a8, d, 0), 1.0)
            b_sh = jnp.where(keep, pltpu.roll(b8, d, 0), 0.0)
            b8 = a8 * b_sh + b8
            a8 = a8 * a_sh
        h8 = a8 * h + b8
        hs_s[pl.ds(r0, SUBLANES), :] = h8
        return h8[SUBLANES - 1:SUBLANES, :]

    h_last = lax.fori_loop(0, ts // SUBLANES, group, h_ref[0:1, :], unroll=4)
    h_ref[...] = jnp.broadcast_to(h_last, h_ref.shape)

    y = hs_s[...] * _gelu_tanh(zg_ref[0])
    y_ref[0] = (_rms(y) * go_ref[...]).astype(BF16)


def _rglru(zx, zg, conv_w, conv_b, w_gates, b_r, b_i, lam, g_out):
    b, s, d_rnn = zx.shape
    ts = SEQ_BLOCK
    blk = lambda bi, j: (bi, j, 0)
    const = lambda bi, j: (0, 0)
    return pl.pallas_call(
        functools.partial(_rglru_kernel, d_rnn=d_rnn),
        grid=(b, s // ts),
        in_specs=[
            pl.BlockSpec((1, ts, d_rnn), blk),
            pl.BlockSpec((1, ts, d_rnn), blk),
            pl.BlockSpec((CONV_WIDTH, d_rnn), const),
            pl.BlockSpec((1, d_rnn), const),
            pl.BlockSpec((d_rnn, 2 * d_rnn), const),
            pl.BlockSpec((1, d_rnn), const),
            pl.BlockSpec((1, d_rnn), const),
            pl.BlockSpec((1, d_rnn), const),
            pl.BlockSpec((1, d_rnn), const),
        ],
        out_specs=pl.BlockSpec((1, ts, d_rnn), blk),
        out_shape=jax.ShapeDtypeStruct((b, s, d_rnn), BF16),
        scratch_shapes=[
            pltpu.VMEM((SUBLANES, d_rnn), F32),
            pltpu.VMEM((SUBLANES, d_rnn), F32),
            pltpu.VMEM((ts, d_rnn), F32),
            pltpu.VMEM((ts, d_rnn), F32),
            pltpu.VMEM((ts, d_rnn), F32),
        ],
        compiler_params=_params("arbitrary", "arbitrary"),
        name="rglru",
    )(zx, zg, conv_w, conv_b, w_gates, b_r, b_i, lam, g_out)


def _attn_kernel(q_ref, k_ref, cp_ref, vt_ref, o_ref, sa_ref, sb_ref, *, bq, bk):
    hp = pl.program_id(1)

    def q_block(qi, carry):
        _attn_q_block(qi, hp, q_ref, k_ref, cp_ref, vt_ref, o_ref, sa_ref, sb_ref, bq=bq, bk=bk)
        return carry

    lax.fori_loop(0, q_ref.shape[1] // bq, q_block, 0)


def _attn_q_block(qi, hp, q_ref, k_ref, cp_ref, vt_ref, o_ref, sa_ref, sb_ref, *, bq, bk):
    q0 = pl.multiple_of(qi * bq, bq)
    q2 = q_ref[0, pl.ds(q0, bq), :]
    lane = lax.broadcasted_iota(jnp.int32, (bq, LANES), 1)
    key = lax.broadcasted_iota(jnp.int32, (bk, bq), 0)
    query = qi * bq + lax.broadcasted_iota(jnp.int32, (bk, bq), 1)
    n_full = qi * (bq // bk)
    qx = []
    for j in range(HEADS_PER_BLOCK):
        head = hp * HEADS_PER_BLOCK + j
        qj = jnp.where(lane // HEAD_DIM == j, q2, jnp.zeros_like(q2))
        sel = jnp.where((lane < C_PIECES * SUBLANES) & (lane % SUBLANES == head), 1.0, 0.0).astype(BF16)
        qx.append(jnp.concatenate([qj, sel], axis=1))

    def scores(kb, s_ref):
        k0 = pl.multiple_of(kb * bk, bk)
        kx = jnp.concatenate([k_ref[0, pl.ds(k0, bk), :], cp_ref[0, pl.ds(k0, bk), :]], axis=1)
        for j in range(HEADS_PER_BLOCK):
            s_ref[j] = lax.dot_general(kx, qx[j], (((1,), (1,)), ((), ())), preferred_element_type=F32)

    def update(kb, s_ref, stats, masked):
        k0 = pl.multiple_of(kb * bk, bk)
        new = []
        for j in range(HEADS_PER_BLOCK):
            m, acc = stats[j]
            s = s_ref[j]
            if masked:
                s = jnp.where(key + k0 <= query, s, -jnp.inf)
            m_new = jnp.maximum(m, jnp.max(s, axis=0, keepdims=True))
            p = jnp.exp2(s - m_new)
            acc = jnp.exp2(m - m_new) * acc + jnp.dot(vt_ref[0, j, kb], p.astype(BF16),
                                                      preferred_element_type=F32)
            new.append((m_new, acc))
        return tuple(new)

    def finish(stats):
        halves = [acc[:HEAD_DIM] / acc[HEAD_DIM:HEAD_DIM + 1] for _, acc in stats]
        o_ref[0, pl.ds(q0, bq), :] = jnp.concatenate(halves, axis=0).T.astype(o_ref.dtype)

    def pair(i, stats):
        scores(2 * i + 1, sb_ref)
        stats = update(2 * i, sa_ref, stats, False)
        scores(2 * i + 2, sa_ref)
        return update(2 * i + 1, sb_ref, stats, False)

    stats = tuple((jnp.full((1, bq), -jnp.inf, F32), jnp.zeros((VT_ROWS, bq), F32)) for _ in range(HEADS_PER_BLOCK))
    scores(0, sa_ref)
    stats = lax.fori_loop(0, n_full // 2, pair, stats)

    @pl.when(n_full % 2 == 0)
    def _():
        finish(update(n_full, sa_ref, stats, True))

    @pl.when(n_full % 2 == 1)
    def _():
        scores(n_full, sb_ref)
        st = update(n_full - 1, sa_ref, stats, False)
        finish(update(n_full, sb_ref, st, True))


def _attention(q, k, cp, vt):
    b, s, d_att = q.shape
    bq, bk = Q_BLOCK, SEQ_BLOCK
    return pl.pallas_call(
        functools.partial(_attn_kernel, bq=bq, bk=bk),
        grid=(b, d_att // LANES),
        in_specs=[
            pl.BlockSpec((1, s, LANES), lambda bi, hp: (bi, 0, hp)),
            pl.BlockSpec((1, s, LANES), lambda bi, hp: (bi, 0, hp)),
            pl.BlockSpec((1, s, LANES), lambda bi, hp: (bi, 0, 0)),
            pl.BlockSpec((1, HEADS_PER_BLOCK, s // bk, VT_ROWS, bk), lambda bi, hp: (bi, hp, 0, 0, 0)),
        ],
        out_specs=pl.BlockSpec((1, s, LANES), lambda bi, hp: (bi, 0, hp)),
        out_shape=jax.ShapeDtypeStruct((b, s, d_att), BF16),
        scratch_shapes=[pltpu.VMEM((HEADS_PER_BLOCK, bk, bq), F32), pltpu.VMEM((HEADS_PER_BLOCK, bk, bq), F32)],
        compiler_params=_params("arbitrary", "arbitrary"),
        name="attention",
    )(q, k, cp, vt)


def _out_route_kernel(x_ref, yr_ref, ya_ref, ga_ref, wo_ref, gf_ref, wr_ref, br_ref, ltri_ref,
                      x1_ref, xn_ref, route_ref, meta_ref, cnt_ref, carry_ref, *, d_rnn):
    i = pl.program_id(0)
    tm = x_ref.shape[0]

    @pl.when(i == 0)
    def _():
        carry_ref[...] = jnp.zeros_like(carry_ref)

    ya = _rms(ya_ref[...].astype(F32)) * ga_ref[...]
    x1 = (x_ref[...]
          + jnp.dot(yr_ref[...], wo_ref[:d_rnn, :], preferred_element_type=F32)
          + jnp.dot(ya.astype(BF16), wo_ref[d_rnn:, :], preferred_element_type=F32))
    x1_ref[...] = x1
    xn = _rms(x1) * gf_ref[...]
    _store_token_tiles(xn_ref, xn)

    nh, nl = _split_bf16(xn)
    both = jnp.dot(nh, wr_ref[...], preferred_element_type=F32)
    logits = (both[:, :LANES] + both[:, LANES:]
              + jnp.dot(nl, wr_ref[:, :LANES], preferred_element_type=F32)) + br_ref[...]

    lane = lax.broadcasted_iota(jnp.int32, (tm, LANES), 1)
    lane_f = lane.astype(F32)
    big = float(LANES)

    def top1(vals):
        m = jnp.max(vals, axis=-1, keepdims=True)
        idx = jnp.min(jnp.where(vals == m, lane_f, big), axis=-1, keepdims=True)
        return m, idx

    is_group = (lane >= N_EXPERTS) & (lane < N_EXPERTS + N_GROUPS)
    gl = jnp.where(is_group, logits, -jnp.inf)
    g_max, g_lane = top1(gl)
    g_p = 1.0 / jnp.sum(jnp.exp(gl - g_max), axis=-1, keepdims=True)
    g_idx = g_lane - float(N_EXPERTS)

    in_group = (lane < N_EXPERTS) & ((lane // EXPERTS_PER_GROUP).astype(F32) == g_idx)
    el = jnp.where(in_group, logits, -jnp.inf)
    m1, i1 = top1(el)
    el2 = jnp.where(lane_f == i1, -jnp.inf, el)
    m2, i2 = top1(el2)
    e2 = jnp.exp(m2 - m1)
    w1 = g_p / (1.0 + e2)
    w2 = g_p * e2 / (1.0 + e2)

    onehot = jnp.where((lane_f == i1) | (lane_f == i2), 1.0, 0.0)
    before = jnp.dot(ltri_ref[...], onehot.astype(BF16), preferred_element_type=F32) + carry_ref[0:1, :]
    rank1 = jnp.sum(jnp.where(lane_f == i1, before, 0.0), axis=-1, keepdims=True)
    rank2 = jnp.sum(jnp.where(lane_f == i2, before, 0.0), axis=-1, keepdims=True)
    carry = carry_ref[0:1, :] + jnp.sum(onehot, axis=0, keepdims=True)
    carry_ref[...] = jnp.broadcast_to(carry, carry_ref.shape)
    cnt_ref[...] = jnp.broadcast_to(carry, cnt_ref.shape)

    rec = jnp.zeros((tm, LANES), F32)
    for ln, val in ((R_ID0, i1), (R_ID1, i2), (R_W0, w1), (R_W1, w2), (R_RANK0, rank1), (R_RANK1, rank2)):
        rec = jnp.where(lane == ln, val, rec)
    route_ref[...] = rec
    meta_ref[...] = rec.T[:SUBLANES, :]


def _out_route(x2, yr, ya, g_att, w_out, g_ffn, wr, b_rt, ltri, *, d_rnn):
    t, d = x2.shape
    tm = SEQ_BLOCK
    d_att = ya.shape[1]
    row = lambda i: (i, 0)
    const = lambda i: (0, 0)
    return pl.pallas_call(
        functools.partial(_out_route_kernel, d_rnn=d_rnn),
        grid=(t // tm,),
        in_specs=[
            pl.BlockSpec((tm, d), row),
            pl.BlockSpec((tm, d_rnn), row),
            pl.BlockSpec((tm, d_att), row),
            pl.BlockSpec((1, d_att), const),
            pl.BlockSpec((d_rnn + d_att, d), const),
            pl.BlockSpec((1, d), const),
            pl.BlockSpec((d, 2 * LANES), const),
            pl.BlockSpec((1, LANES), const),
            pl.BlockSpec((tm, tm), const),
        ],
        out_specs=(
            pl.BlockSpec((tm, d), row),
            pl.BlockSpec((tm * (d // LANES), LANES), row),
            pl.BlockSpec((tm, LANES), row),
            pl.BlockSpec((SUBLANES, tm), lambda i: (0, i)),
            pl.BlockSpec((SUBLANES, LANES), const),
        ),
        out_shape=(
            jax.ShapeDtypeStruct((t, d), F32),
            jax.ShapeDtypeStruct((t * (d // LANES), LANES), F32),
            jax.ShapeDtypeStruct((t, LANES), F32),
            jax.ShapeDtypeStruct((SUBLANES, t), F32),
            jax.ShapeDtypeStruct((SUBLANES, LANES), F32),
        ),
        scratch_shapes=[pltpu.VMEM((SUBLANES, LANES), F32)],
        compiler_params=_params("arbitrary"),
        name="out_route",
    )(x2, yr, ya, g_att, w_out, g_ffn, wr, b_rt, ltri)


def _positions_kernel(off_ref, meta_ref, pos_ref):
    m = meta_ref[...]
    ids = m[R_ID0:R_ID1 + 1, :]
    acc = m[R_RANK0:R_RANK1 + 1, :]
    for e in range(N_EXPERTS):
        acc = acc + jnp.where(ids == float(e), off_ref[e].astype(F32), 0.0)
    pos_ref[...] = acc.astype(jnp.int32)


def _positions(off, meta):
    t = meta.shape[1]
    return pl.pallas_call(
        _positions_kernel,
        grid_spec=pltpu.PrefetchScalarGridSpec(
            num_scalar_prefetch=1,
            grid=(1,),
            in_specs=[pl.BlockSpec((SUBLANES, t), lambda i, off_ref: (0, 0))],
            out_specs=pl.BlockSpec((TOP_K, t), lambda i, off_ref: (0, 0)),
        ),
        out_shape=jax.ShapeDtypeStruct((TOP_K, t), jnp.int32),
        compiler_params=_params("arbitrary"),
        name="positions",
    )(off, meta)


def _dispatch_kernel(pos_ref, xn_ref, xs_ref, sem, *, n_tok, tm, chunks):
    base = pl.program_id(0) * tm

    def issue(r, carry):
        for k in range(TOP_K):
            _token_copy(xn_ref, r, xs_ref, pos_ref[k * n_tok + base + r], sem, chunks).start(priority=k % 2)
        return carry

    def drain(r, carry):
        for k in range(TOP_K):
            _token_copy(xn_ref, 0, xs_ref, 0, sem, chunks).wait()
        return carry

    lax.fori_loop(0, tm, issue, 0, unroll=DMA_UNROLL)
    lax.fori_loop(0, tm, drain, 0, unroll=DMA_UNROLL)


def _dispatch(pos, xn_tiles, t):
    chunks = xn_tiles.shape[0] // t
    tm = DISPATCH_BLOCK
    return pl.pallas_call(
        functools.partial(_dispatch_kernel, n_tok=t, tm=tm, chunks=chunks),
        grid_spec=pltpu.PrefetchScalarGridSpec(
            num_scalar_prefetch=1,
            grid=(t // tm,),
            in_specs=[pl.BlockSpec((tm * chunks, LANES), lambda i, pos_ref: (i, 0))],
            out_specs=pl.BlockSpec(memory_space=pl.ANY),
            scratch_shapes=[pltpu.SemaphoreType.DMA(())],
        ),
        out_shape=jax.ShapeDtypeStruct((TOP_K * t * chunks, LANES), F32),
        compiler_params=_params("arbitrary"),
        name="dispatch",
    )(pos, xn_tiles)


def _ffn_kernel(tile_ref, exp_ref, lo_ref, hi_ref, first_ref, init_ref,
                xs_ref, wg_ref, wu_ref, wd_ref, ys_ref, wg_s, wu_s, wd_s, *, tm):
    g = pl.program_id(0)
    chunks = xs_ref.shape[0] // tm

    @pl.when(first_ref[g] == 1)
    def _():
        wg_s[...] = wg_ref[...].astype(BF16)
        wu_s[...] = wu_ref[...].astype(BF16)
        wd_s[...] = wd_ref[...].astype(BF16)

    lo = lo_ref[g]
    hi = hi_ref[g]

    @pl.when(hi > lo)
    def _():
        x = _load_token_tiles(xs_ref, tm, chunks).astype(BF16)
        hg = jnp.dot(x, wg_s[...], preferred_element_type=F32)
        hu = jnp.dot(x, wu_s[...], preferred_element_type=F32)
        h = (hg * _sigmoid(hg)) * hu
        y = jnp.dot(h.astype(BF16), wd_s[...], preferred_element_type=F32)
        row = lax.broadcasted_iota(jnp.int32, y.shape, 0)
        mine = (row >= lo) & (row < hi)

        @pl.when(init_ref[g] == 1)
        def _():
            _store_token_tiles(ys_ref, jnp.where(mine, y, 0.0))

        @pl.when(init_ref[g] == 0)
        def _():
            _store_token_tiles(ys_ref, jnp.where(mine, y, _load_token_tiles(ys_ref, tm, chunks)))


def _expert_ffn(meta, xs, w_gate, w_up, w_down, layer):
    d, d_e = w_gate.shape[-2:]
    chunks = d // LANES
    tm = FFN_TILE
    steps = meta[0].shape[0]
    x_map = lambda g, tile, exp, lo, hi, first, init: (tile[g], 0)
    w_map = lambda g, tile, exp, lo, hi, first, init: (layer, exp[g], 0, 0)
    return pl.pallas_call(
        functools.partial(_ffn_kernel, tm=tm),
        grid_spec=pltpu.PrefetchScalarGridSpec(
            num_scalar_prefetch=6,
            grid=(steps,),
            in_specs=[
                pl.BlockSpec((tm * chunks, LANES), x_map),
                pl.BlockSpec((None, None, d, d_e), w_map),
                pl.BlockSpec((None, None, d, d_e), w_map),
                pl.BlockSpec((None, None, d_e, d), w_map),
            ],
            out_specs=pl.BlockSpec((tm * chunks, LANES), x_map),
            scratch_shapes=[
                pltpu.VMEM((d, d_e), BF16),
                pltpu.VMEM((d, d_e), BF16),
                pltpu.VMEM((d_e, d), BF16),
            ],
        ),
        out_shape=jax.ShapeDtypeStruct(xs.shape, F32),
        compiler_params=_params("arbitrary"),
        name="expert_ffn",
    )(*meta, xs, w_gate, w_up, w_down)


def _ffn_schedule(cnt, off, rows):
    tm = FFN_TILE
    n_tiles = rows // tm
    steps = n_tiles + N_EXPERTS - 1
    end = off + cnt
    first_tile = off // tm
    last_tile = jnp.where(cnt > 0, (end - 1) // tm, first_tile - 1)
    visits = last_tile - first_tile + 1
    vis_end = jnp.cumsum(visits)
    vis_start = vis_end - visits
    total = vis_end[-1]
    g = jnp.arange(steps, dtype=jnp.int32)
    gc = jnp.minimum(g, total - 1)
    exp = jnp.sum((vis_end[None, :] <= gc[:, None]).astype(jnp.int32), axis=1)
    is_exp = exp[:, None] == jnp.arange(N_EXPERTS, dtype=jnp.int32)[None, :]
    pick = lambda a: jnp.sum(jnp.where(is_exp, a[None, :], 0), axis=1)
    tile = pick(first_tile) + (gc - pick(vis_start))
    valid = g < total
    lo = jnp.where(valid, jnp.maximum(pick(off), tile * tm) - tile * tm, 0)
    hi = jnp.where(valid, jnp.minimum(pick(end), (tile + 1) * tm) - tile * tm, 0)
    prev_exp = jnp.concatenate([jnp.full((1,), -1, jnp.int32), exp[:-1]])
    prev_tile = jnp.concatenate([jnp.full((1,), -1, jnp.int32), tile[:-1]])
    first = (valid & (exp != prev_exp)).astype(jnp.int32)
    init = (valid & (tile != prev_tile)).astype(jnp.int32)
    i32 = lambda a: a.astype(jnp.int32)
    return i32(tile), i32(exp), i32(lo), i32(hi), first, init


def _combine_kernel(pos_ref, x1_ref, route_ref, p_ref, ys_ref, gp_ref, wg_ref, bg_ref, wp_ref,
                    o_ref, buf_a, buf_b, sem, *, n_tok):
    tm, d = x1_ref.shape
    chunks = d // LANES
    i = pl.program_id(0)
    last = pl.num_programs(0) - 1
    bufs = (buf_a, buf_b)

    def start_row(block, slot, r):
        for k in range(TOP_K):
            src = pos_ref[k * n_tok + block * tm + r]
            _token_copy(ys_ref, src, bufs[slot].at[k], r, sem.at[slot], chunks).start(priority=k % 2)

    def drain(slot):
        def body(r, carry):
            for k in range(TOP_K):
                _token_copy(ys_ref, 0, bufs[slot].at[k], 0, sem.at[slot], chunks).wait()
            return carry
        lax.fori_loop(0, tm, body, 0, unroll=DMA_UNROLL)

    @pl.when(i == 0)
    def _():
        def body(r, carry):
            start_row(0, 0, r)
            return carry
        lax.fori_loop(0, tm, body, 0, unroll=DMA_UNROLL)

    def step(slot):
        drain(slot)
        for r in range(tm):
            start_row(jnp.minimum(i + 1, last), 1 - slot, r)
        rec = route_ref[...]
        w0 = rec[:, R_W0:R_W0 + 1]
        w1 = rec[:, R_W1:R_W1 + 1]
        y0 = _load_token_tiles(bufs[slot].at[0], tm, chunks)
        y1 = _load_token_tiles(bufs[slot].at[1], tm, chunks)
        x2 = x1_ref[...] + (y0 * w0 + y1 * w1)
        n = _rms(x2) * gp_ref[...]
        gate = _sigmoid(jnp.dot(n.astype(BF16), wg_ref[...], preferred_element_type=F32) + bg_ref[...])
        pe = jnp.dot(p_ref[...].astype(BF16), wp_ref[...], preferred_element_type=F32)
        o_ref[...] = x2 + pe * gate

        @pl.when(i == last)
        def _():
            drain(1 - slot)

    for slot in range(2):
        pl.when(i % 2 == slot)(functools.partial(step, slot))


def _combine_ple(pos, x1, route, p3, ys, g_ple, w_gate, b_gate, w_proj, layer):
    t, d = x1.shape
    dp = p3.shape[-1]
    tm = COMBINE_BLOCK
    bpb = p3.shape[2] // tm
    row = lambda i, pos_ref: (i, 0)
    const = lambda i, pos_ref: (0, 0)
    return pl.pallas_call(
        functools.partial(_combine_kernel, n_tok=t),
        grid_spec=pltpu.PrefetchScalarGridSpec(
            num_scalar_prefetch=1,
            grid=(t // tm,),
            in_specs=[
                pl.BlockSpec((tm, d), row),
                pl.BlockSpec((tm, LANES), row),
                pl.BlockSpec((None, None, tm, dp), lambda i, pos_ref: (layer, i // bpb, i % bpb, 0)),
                pl.BlockSpec(memory_space=pl.ANY),
                pl.BlockSpec((1, d), const),
                pl.BlockSpec((d, d), const),
                pl.BlockSpec((1, d), const),
                pl.BlockSpec((dp, d), const),
            ],
            out_specs=pl.BlockSpec((tm, d), row),
            scratch_shapes=[
                pltpu.VMEM((TOP_K, tm * (d // LANES), LANES), F32),
                pltpu.VMEM((TOP_K, tm * (d // LANES), LANES), F32),
                pltpu.SemaphoreType.DMA((2,)),
            ],
        ),
        out_shape=jax.ShapeDtypeStruct((t, d), F32),
        compiler_params=_params("arbitrary"),
        name="combine_ple",
    )(pos, x1, route, p3, ys, g_ple, w_gate, b_gate, w_proj)


def _block_diag(w):
    nb, n, _ = w.shape
    eye = jnp.eye(nb, dtype=w.dtype)
    return jnp.einsum("hij,hg->higj", w, eye).reshape(nb * n, nb * n)


def kernel(x, p, mix_norm, w_in, b_forget, conv_w, conv_b, w_rgate, b_rgate, w_igate, b_igate, lru_lambda,
           q_norm, k_norm, lru_out_norm, att_out_norm, w_out, ffn_norm, w_group, b_group, w_router, b_router,
           w_exp_gate, w_exp_up, w_exp_down, ple_norm, w_ple_gate, b_ple_gate, w_ple_proj):
    batch, seq, d = x.shape
    depth = w_in.shape[0]
    d_rnn = conv_w.shape[-1]
    n_heads = b_forget.shape[-1]
    d_att = n_heads * HEAD_DIM
    t = batch * seq
    assert n_heads == SUBLANES and d_att % LANES == 0 and seq % SEQ_BLOCK == 0 and seq % Q_BLOCK == 0
    assert C_PIECES * SUBLANES <= LANES
    assert Q_BLOCK == SEQ_BLOCK and (TOP_K * t) % FFN_TILE == 0
    assert t % DISPATCH_BLOCK == 0 and seq % COMBINE_BLOCK == 0
    assert w_router.shape[-1] == N_EXPERTS and w_group.shape[-1] == N_GROUPS

    idx = jnp.arange(SEQ_BLOCK)
    tri_incl = (idx[:, None] <= idx[None, :]).astype(BF16)
    tri_strict = (idx[None, :] < idx[:, None]).astype(BF16)
    hid = jnp.arange(d_att) // HEAD_DIM
    gsum = (hid[:, None] == hid[None, :]).astype(BF16)
    row2 = lambda a: a.reshape(1, -1).astype(F32)
    n_in_pad = 2 * d_rnn + 3 * d_att + LANES
    p3 = p

    x2 = x.reshape(t, d)
    for i in range(depth):
        w_pad = jnp.pad(w_in[i], ((0, 0), (0, n_in_pad - w_in.shape[-1]))).astype(BF16)
        qg = row2(jnp.tile(q_norm[i], n_heads) * (LOG2E / math.sqrt(HEAD_DIM)))
        kg = row2(jnp.tile(k_norm[i], n_heads))
        zx, zg, q, k, vt, cp = _in_proj(
            x2, row2(mix_norm[i]), w_pad, gsum, qg, kg, b_forget[i].reshape(n_heads, 1).astype(F32), tri_incl,
            batch=batch, seq=seq, d_rnn=d_rnn, d_att=d_att)

        w_gates = jnp.concatenate([_block_diag(w_rgate[i]), _block_diag(w_igate[i])], axis=1).astype(BF16)
        yr = _rglru(zx.reshape(batch, seq, d_rnn), zg.reshape(batch, seq, d_rnn), conv_w[i].astype(F32),
                    row2(conv_b[i]), w_gates, row2(b_rgate[i]), row2(b_igate[i]), row2(lru_lambda[i]),
                    row2(lru_out_norm[i]))
        ya = _attention(q.reshape(batch, seq, d_att), k.reshape(batch, seq, d_att),
                        cp.reshape(batch, seq, LANES), vt)

        w_rt = jnp.pad(jnp.concatenate([w_router[i], w_group[i]], axis=1),
                       ((0, 0), (0, LANES - N_EXPERTS - N_GROUPS))).astype(F32)
        wr_hi, wr_lo = _split_bf16(w_rt)
        b_rt = jnp.pad(jnp.concatenate([b_router[i], b_group[i]]), (0, LANES - N_EXPERTS - N_GROUPS))
        x1, xn, route, meta, cnt = _out_route(
            x2, yr.reshape(t, d_rnn), ya.reshape(t, d_att), row2(att_out_norm[i]), w_out[i].astype(BF16),
            row2(ffn_norm[i]), jnp.concatenate([wr_hi, wr_lo], axis=1), row2(b_rt), tri_strict, d_rnn=d_rnn)

        cnt_e = cnt[0, :N_EXPERTS].astype(jnp.int32)
        off_e = jnp.cumsum(cnt_e) - cnt_e
        pos = _positions(off_e, meta).reshape(-1)

        xs = _dispatch(pos, xn, t)
        ys = _expert_ffn(_ffn_schedule(cnt_e, off_e, TOP_K * t), xs, w_exp_gate, w_exp_up, w_exp_down, i)
        x2 = _combine_ple(pos, x1, route, p3, ys, row2(ple_norm[i]), w_ple_gate[i].astype(BF16),
                          row2(b_ple_gate[i]), w_ple_proj[i].astype(BF16), i)
    return x2.reshape(batch, seq, d)
```

```python
import functools
import math

import jax
import jax.numpy as jnp
from jax import lax
from jax.experimental import pallas as pl
from jax.experimental.pallas import tpu as pltpu

F32 = jnp.float32
BF16 = jnp.bfloat16

EPS = 1e-6
LRU_C = 8.0
HEAD_DIM = 64
CONV_WIDTH = 4
N_GROUPS = 4
EXPERTS_PER_GROUP = 8
N_EXPERTS = N_GROUPS * EXPERTS_PER_GROUP
TOP_K = 2

LOG2E = math.log2(math.e)

LANES = 128
SUBLANES = 8
VMEM_LIMIT = 56 * 1024 * 1024
HEADS_PER_BLOCK = LANES // HEAD_DIM
VT_ROWS = HEAD_DIM + 16
C_PIECES = 3

SEQ_BLOCK = 512
Q_BLOCK = 512
FFN_TILE = 256
DISPATCH_BLOCK = 1024
COMBINE_BLOCK = 512
DMA_UNROLL = 8

R_ID0, R_ID1, R_W0, R_W1, R_RANK0, R_RANK1 = 0, 1, 2, 3, 4, 5


def _sigmoid(x):
    return 1.0 / (1.0 + jnp.exp(-x))


def _softplus(x):
    return jnp.maximum(x, 0.0) + jnp.log1p(jnp.exp(-jnp.abs(x)))


def _gelu_tanh(x):
    c = math.sqrt(2.0 / math.pi)
    return x * (0.5 * (1.0 + jnp.tanh(c * (x + 0.044715 * (x * x * x)))))


def _rms(x):
    return x * lax.rsqrt(jnp.mean(x * x, axis=-1, keepdims=True) + EPS)


def _split_bf16(x):
    hi = x.astype(BF16)
    lo = (x - hi.astype(F32)).astype(BF16)
    return hi, lo


def _store_token_tiles(ref, x):
    n, d = x.shape
    chunks = d // LANES
    for j in range(chunks):
        ref[pl.ds(j, n, stride=chunks), :] = x[:, j * LANES:(j + 1) * LANES]


def _load_token_tiles(ref, n, chunks):
    return jnp.concatenate([ref[pl.ds(j, n, stride=chunks), :] for j in range(chunks)], axis=1)


def _token_copy(src_ref, src_tok, dst_ref, dst_tok, sem, chunks):
    def rows(tok):
        start = tok * chunks
        return pl.ds(start if isinstance(tok, int) else pl.multiple_of(start, chunks), chunks)
    return pltpu.make_async_copy(src_ref.at[rows(src_tok), :], dst_ref.at[rows(dst_tok), :], sem)


def _params(*sem):
    return pltpu.CompilerParams(dimension_semantics=sem, vmem_limit_bytes=VMEM_LIMIT)


def _in_proj_kernel(x_ref, g_ref, w_ref, gsum_ref, qg_ref, kg_ref, bf_ref, tri_ref,
                    zx_ref, zg_ref, q_ref, k_ref, vt_ref, cp_ref, carry_ref, *, blocks_per_seq, d_rnn, d_att):
    i = pl.program_id(0)
    tm = x_ref.shape[0]
    n = _rms(x_ref[...]) * g_ref[...]
    z = jnp.dot(n.astype(BF16), w_ref[...], preferred_element_type=F32)
    o = 0
    zx_ref[...] = z[:, o:o + d_rnn]; o += d_rnn
    zg_ref[...] = z[:, o:o + d_rnn]; o += d_rnn
    zq = z[:, o:o + d_att]; o += d_att
    zk = z[:, o:o + d_att]; o += d_att
    zv = z[:, o:o + d_att]; o += d_att
    zf = z[:, o:o + LANES]

    zvt = zv.T
    ones = jnp.ones((VT_ROWS - HEAD_DIM, tm), F32)
    for h in range(d_att // HEAD_DIM):
        vt_ref[0, h, 0] = jnp.concatenate([zvt[h * HEAD_DIM:(h + 1) * HEAD_DIM], ones], axis=0).astype(BF16)

    def head_norm(t, gain):
        ss = jnp.dot((t * t).astype(BF16), gsum_ref[...], preferred_element_type=F32)
        return t * lax.rsqrt(ss * (1.0 / HEAD_DIM) + EPS) * gain

    q_ref[...] = head_norm(zq, qg_ref[...]).astype(BF16)
    k_ref[...] = head_norm(zk, kg_ref[...]).astype(BF16)

    zft = zf.T[:SUBLANES, :] + bf_ref[...]
    logf = jnp.minimum(zft, 0.0) - jnp.log1p(jnp.exp(-jnp.abs(zft)))
    hi, lo = _split_bf16(logf)
    cs = (jnp.dot(hi, tri_ref[...], preferred_element_type=F32)
          + jnp.dot(lo, tri_ref[...], preferred_element_type=F32))

    @pl.when(i % blocks_per_seq == 0)
    def _():
        carry_ref[...] = jnp.zeros_like(carry_ref)

    c = cs + carry_ref[:, 0:1]
    carry_ref[...] = jnp.broadcast_to(c[:, tm - 1:tm], carry_ref.shape)

    cneg = c * (-LOG2E)
    pieces = []
    for _ in range(C_PIECES):
        piece = cneg.astype(BF16).astype(F32)
        pieces.append(piece)
        cneg = cneg - piece
    pieces.append(jnp.zeros((LANES - C_PIECES * SUBLANES, tm), F32))
    cp_ref[...] = jnp.concatenate(pieces, axis=0).T.astype(BF16)


def _in_proj(x2, g_mix, w_pad, gsum, qg, kg, bf, tri, *, batch, seq, d_rnn, d_att):
    t, d = x2.shape
    tm = SEQ_BLOCK
    bps = seq // tm
    n_pad = w_pad.shape[1]
    row = lambda i: (i, 0)
    const = lambda i: (0, 0)
    out_shape = (
        jax.ShapeDtypeStruct((t, d_rnn), F32),
        jax.ShapeDtypeStruct((t, d_rnn), F32),
        jax.ShapeDtypeStruct((t, d_att), BF16),
        jax.ShapeDtypeStruct((t, d_att), BF16),
        jax.ShapeDtypeStruct((batch, d_att // HEAD_DIM, bps, VT_ROWS, tm), BF16),
        jax.ShapeDtypeStruct((t, LANES), BF16),
    )
    return pl.pallas_call(
        functools.partial(_in_proj_kernel, blocks_per_seq=bps, d_rnn=d_rnn, d_att=d_att),
        grid=(t // tm,),
        in_specs=[
            pl.BlockSpec((tm, d), row),
            pl.BlockSpec((1, d), const),
            pl.BlockSpec((d, n_pad), const),
            pl.BlockSpec((d_att, d_att), const),
            pl.BlockSpec((1, d_att), const),
            pl.BlockSpec((1, d_att), const),
            pl.BlockSpec((SUBLANES, 1), const),
            pl.BlockSpec((tm, tm), const),
        ],
        out_specs=(
            pl.BlockSpec((tm, d_rnn), row),
            pl.BlockSpec((tm, d_rnn), row),
            pl.BlockSpec((tm, d_att), row),
            pl.BlockSpec((tm, d_att), row),
            pl.BlockSpec((1, d_att // HEAD_DIM, 1, VT_ROWS, tm), lambda i: (i // bps, 0, i % bps, 0, 0)),
            pl.BlockSpec((tm, LANES), row),
        ),
        out_shape=out_shape,
        scratch_shapes=[pltpu.VMEM((SUBLANES, LANES), F32)],
        compiler_params=_params("arbitrary"),
        name="in_proj",
    )(x2, g_mix, w_pad, gsum, qg, kg, bf, tri)


def _rglru_kernel(zx_ref, zg_ref, cw_ref, cb_ref, wg_ref, br_ref, bi_ref, lam_ref, go_ref,
                  y_ref, tail_ref, h_ref, a_s, u_s, hs_s, *, d_rnn):
    j = pl.program_id(1)
    ts = zx_ref.shape[1]

    @pl.when(j == 0)
    def _():
        tail_ref[...] = jnp.zeros_like(tail_ref)
        h_ref[...] = jnp.zeros_like(h_ref)

    x = zx_ref[0]
    tail = tail_ref[...]
    row8 = lax.broadcasted_iota(jnp.int32, (SUBLANES, d_rnn), 0)

    xc = x * cw_ref[CONV_WIDTH - 1:CONV_WIDTH, :] + cb_ref[...]
    for d in range(1, CONV_WIDTH):
        xr = pltpu.roll(x, d, 0)
        top = jnp.where(row8 < d, pltpu.roll(tail, d, 0), xr[:SUBLANES])
        xs = jnp.concatenate([top, xr[SUBLANES:]], axis=0)
        xc = xc + xs * cw_ref[CONV_WIDTH - 1 - d:CONV_WIDTH - d, :]
    tail_ref[...] = x[ts - SUBLANES:, :]

    gl = jnp.dot(xc.astype(BF16), wg_ref[...], preferred_element_type=F32)
    r = _sigmoid(gl[:, :d_rnn] + br_ref[...])
    ig = _sigmoid(gl[:, d_rnn:] + bi_ref[...])
    log_a = (-LRU_C) * r * _softplus(-lam_ref[...])
    a = jnp.exp(log_a)
    a_s[...] = a
    u_s[...] = jnp.sqrt(1.0 - a * a) * (ig * xc)

    def group(gidx, h):
        r0 = pl.multiple_of(gidx * SUBLANES, SUBLANES)
        a8 = a_s[pl.ds(r0, SUBLANES), :]
        b8 = u_s[pl.ds(r0, SUBLANES), :]
        for d in (1, 2, 4):
            keep = row8 >= d
            a_sh = jnp.where(keep, pltpu.roll(a8, d, 0), 1.0)
            b_sh = jnp.where(keep, pltpu.roll(b8, d, 0), 0.0)
            b8 = a8 * b_sh + b8
            a8 = a8 * a_sh
        h8 = a8 * h + b8
        hs_s[pl.ds(r0, SUBLANES), :] = h8
        return h8[SUBLANES - 1:SUBLANES, :]

    h_last = lax.fori_loop(0, ts // SUBLANES, group, h_ref[0:1, :], unroll=4)
    h_ref[...] = jnp.broadcast_to(h_last, h_ref.shape)

    y = hs_s[...] * _gelu_tanh(zg_ref[0])
    y_ref[0] = (_rms(y) * go_ref[...]).astype(BF16)


def _rglru(zx, zg, conv_w, conv_b, w_gates, b_r, b_i, lam, g_out):
    b, s, d_rnn = zx.shape
    ts = SEQ_BLOCK
    blk = lambda bi, j: (bi, j, 0)
    const = lambda bi, j: (0, 0)
    return pl.pallas_call(
        functools.partial(_rglru_kernel, d_rnn=d_rnn),
        grid=(b, s // ts),
        in_specs=[
            pl.BlockSpec((1, ts, d_rnn), blk),
            pl.BlockSpec((1, ts, d_rnn), blk),
            pl.BlockSpec((CONV_WIDTH, d_rnn), const),
            pl.BlockSpec((1, d_rnn), const),
            pl.BlockSpec((d_rnn, 2 * d_rnn), const),
            pl.BlockSpec((1, d_rnn), const),
            pl.BlockSpec((1, d_rnn), const),
            pl.BlockSpec((1, d_rnn), const),
            pl.BlockSpec((1, d_rnn), const),
        ],
        out_specs=pl.BlockSpec((1, ts, d_rnn), blk),
        out_shape=jax.ShapeDtypeStruct((b, s, d_rnn), BF16),
        scratch_shapes=[
            pltpu.VMEM((SUBLANES, d_rnn), F32),
            pltpu.VMEM((SUBLANES, d_rnn), F32),
            pltpu.VMEM((ts, d_rnn), F32),
            pltpu.VMEM((ts, d_rnn), F32),
            pltpu.VMEM((ts, d_rnn), F32),
        ],
        compiler_params=_params("arbitrary", "arbitrary"),
        name="rglru",
    )(zx, zg, conv_w, conv_b, w_gates, b_r, b_i, lam, g_out)


def _attn_kernel(q_ref, k_ref, cp_ref, vt_ref, o_ref, sa_ref, sb_ref, *, bq, bk):
    hp = pl.program_id(1)

    def q_block(qi, carry):
        _attn_q_block(qi, hp, q_ref, k_ref, cp_ref, vt_ref, o_ref, sa_ref, sb_ref, bq=bq, bk=bk)
        return carry

    lax.fori_loop(0, q_ref.shape[1] // bq, q_block, 0)


def _attn_q_block(qi, hp, q_ref, k_ref, cp_ref, vt_ref, o_ref, sa_ref, sb_ref, *, bq, bk):
    q0 = pl.multiple_of(qi * bq, bq)
    q2 = q_ref[0, pl.ds(q0, bq), :]
    lane = lax.broadcasted_iota(jnp.int32, (bq, LANES), 1)
    key = lax.broadcasted_iota(jnp.int32, (bk, bq), 0)
    query = qi * bq + lax.broadcasted_iota(jnp.int32, (bk, bq), 1)
    n_full = qi * (bq // bk)
    qx = []
    for j in range(HEADS_PER_BLOCK):
        head = hp * HEADS_PER_BLOCK + j
        qj = jnp.where(lane // HEAD_DIM == j, q2, jnp.zeros_like(q2))
        sel = jnp.where((lane < C_PIECES * SUBLANES) & (lane % SUBLANES == head), 1.0, 0.0).astype(BF16)
        qx.append(jnp.concatenate([qj, sel], axis=1))

    def scores(kb, s_ref):
        k0 = pl.multiple_of(kb * bk, bk)
        kx = jnp.concatenate([k_ref[0, pl.ds(k0, bk), :], cp_ref[0, pl.ds(k0, bk), :]], axis=1)
        for j in range(HEADS_PER_BLOCK):
            s_ref[j] = lax.dot_general(kx, qx[j], (((1,), (1,)), ((), ())), preferred_element_type=F32)

    def update(kb, s_ref, stats, masked):
        k0 = pl.multiple_of(kb * bk, bk)
        new = []
        for j in range(HEADS_PER_BLOCK):
            m, acc = stats[j]
            s = s_ref[j]
            if masked:
                s = jnp.where(key + k0 <= query, s, -jnp.inf)
            m_new = jnp.maximum(m, jnp.max(s, axis=0, keepdims=True))
            p = jnp.exp2(s - m_new)
            acc = jnp.exp2(m - m_new) * acc + jnp.dot(vt_ref[0, j, kb], p.astype(BF16),
                                                      preferred_element_type=F32)
            new.append((m_new, acc))
        return tuple(new)

    def finish(stats):
        halves = [acc[:HEAD_DIM] / acc[HEAD_DIM:HEAD_DIM + 1] for _, acc in stats]
        o_ref[0, pl.ds(q0, bq), :] = jnp.concatenate(halves, axis=0).T.astype(o_ref.dtype)

    def pair(i, stats):
        scores(2 * i + 1, sb_ref)
        stats = update(2 * i, sa_ref, stats, False)
        scores(2 * i + 2, sa_ref)
        return update(2 * i + 1, sb_ref, stats, False)

    stats = tuple((jnp.full((1, bq), -jnp.inf, F32), jnp.zeros((VT_ROWS, bq), F32)) for _ in range(HEADS_PER_BLOCK))
    scores(0, sa_ref)
    stats = lax.fori_loop(0, n_full // 2, pair, stats)

    @pl.when(n_full % 2 == 0)
    def _():
        finish(update(n_full, sa_ref, stats, True))

    @pl.when(n_full % 2 == 1)
    def _():
        scores(n_full, sb_ref)
        st = update(n_full - 1, sa_ref, stats, False)
        finish(update(n_full, sb_ref, st, True))


def _attention(q, k, cp, vt):
    b, s, d_att = q.shape
    bq, bk = Q_BLOCK, SEQ_BLOCK
    return pl.pallas_call(
        functools.partial(_attn_kernel, bq=bq, bk=bk),
        grid=(b, d_att // LANES),
        in_specs=[
            pl.BlockSpec((1, s, LANES), lambda bi, hp: (bi, 0, hp)),
            pl.BlockSpec((1, s, LANES), lambda bi, hp: (bi, 0, hp)),
            pl.BlockSpec((1, s, LANES), lambda bi, hp: (bi, 0, 0)),
            pl.BlockSpec((1, HEADS_PER_BLOCK, s // bk, VT_ROWS, bk), lambda bi, hp: (bi, hp, 0, 0, 0)),
        ],
        out_specs=pl.BlockSpec((1, s, LANES), lambda bi, hp: (bi, 0, hp)),
        out_shape=jax.ShapeDtypeStruct((b, s, d_att), BF16),
        scratch_shapes=[pltpu.VMEM((HEADS_PER_BLOCK, bk, bq), F32), pltpu.VMEM((HEADS_PER_BLOCK, bk, bq), F32)],
        compiler_params=_params("arbitrary", "arbitrary"),
        name="attention",
    )(q, k, cp, vt)


def _out_route_kernel(x_ref, yr_ref, ya_ref, ga_ref, wo_ref, gf_ref, wr_ref, br_ref, ltri_ref,
                      x1_ref, xn_ref, route_ref, meta_ref, cnt_ref, carry_ref, *, d_rnn):
    i = pl.program_id(0)
    tm = x_ref.shape[0]

    @pl.when(i == 0)
    def _():
        carry_ref[...] = jnp.zeros_like(carry_ref)

    ya = _rms(ya_ref[...].astype(F32)) * ga_ref[...]
    x1 = (x_ref[...]
          + jnp.dot(yr_ref[...], wo_ref[:d_rnn, :], preferred_element_type=F32)
          + jnp.dot(ya.astype(BF16), wo_ref[d_rnn:, :], preferred_element_type=F32))
    x1_ref[...] = x1
    xn = _rms(x1) * gf_ref[...]
    _store_token_tiles(xn_ref, xn)

    nh, nl = _split_bf16(xn)
    both = jnp.dot(nh, wr_ref[...], preferred_element_type=F32)
    logits = (both[:, :LANES] + both[:, LANES:]
              + jnp.dot(nl, wr_ref[:, :LANES], preferred_element_type=F32)) + br_ref[...]

    lane = lax.broadcasted_iota(jnp.int32, (tm, LANES), 1)
    lane_f = lane.astype(F32)
    big = float(LANES)

    def top1(vals):
        m = jnp.max(vals, axis=-1, keepdims=True)
        idx = jnp.min(jnp.where(vals == m, lane_f, big), axis=-1, keepdims=True)
        return m, idx

    is_group = (lane >= N_EXPERTS) & (lane < N_EXPERTS + N_GROUPS)
    gl = jnp.where(is_group, logits, -jnp.inf)
    g_max, g_lane = top1(gl)
    g_p = 1.0 / jnp.sum(jnp.exp(gl - g_max), axis=-1, keepdims=True)
    g_idx = g_lane - float(N_EXPERTS)

    in_group = (lane < N_EXPERTS) & ((lane // EXPERTS_PER_GROUP).astype(F32) == g_idx)
    el = jnp.where(in_group, logits, -jnp.inf)
    m1, i1 = top1(el)
    el2 = jnp.where(lane_f == i1, -jnp.inf, el)
    m2, i2 = top1(el2)
    e2 = jnp.exp(m2 - m1)
    w1 = g_p / (1.0 + e2)
    w2 = g_p * e2 / (1.0 + e2)

    onehot = jnp.where((lane_f == i1) | (lane_f == i2), 1.0, 0.0)
    before = jnp.dot(ltri_ref[...], onehot.astype(BF16), preferred_element_type=F32) + carry_ref[0:1, :]
    rank1 = jnp.sum(jnp.where(lane_f == i1, before, 0.0), axis=-1, keepdims=True)
    rank2 = jnp.sum(jnp.where(lane_f == i2, before, 0.0), axis=-1, keepdims=True)
    carry = carry_ref[0:1, :] + jnp.sum(onehot, axis=0, keepdims=True)
    carry_ref[...] = jnp.broadcast_to(carry, carry_ref.shape)
    cnt_ref[...] = jnp.broadcast_to(carry, cnt_ref.shape)

    rec = jnp.zeros((tm, LANES), F32)
    for ln, val in ((R_ID0, i1), (R_ID1, i2), (R_W0, w1), (R_W1, w2), (R_RANK0, rank1), (R_RANK1, rank2)):
        rec = jnp.where(lane == ln, val, rec)
    route_ref[...] = rec
    meta_ref[...] = rec.T[:SUBLANES, :]


def _out_route(x2, yr, ya, g_att, w_out, g_ffn, wr, b_rt, ltri, *, d_rnn):
    t, d = x2.shape
    tm = SEQ_BLOCK
    d_att = ya.shape[1]
    row = lambda i: (i, 0)
    const = lambda i: (0, 0)
    return pl.pallas_call(
        functools.partial(_out_route_kernel, d_rnn=d_rnn),
        grid=(t // tm,),
        in_specs=[
            pl.BlockSpec((tm, d), row),
            pl.BlockSpec((tm, d_rnn), row),
            pl.BlockSpec((tm, d_att), row),
            pl.BlockSpec((1, d_att), const),
            pl.BlockSpec((d_rnn + d_att, d), const),
            pl.BlockSpec((1, d), const),
            pl.BlockSpec((d, 2 * LANES), const),
            pl.BlockSpec((1, LANES), const),
            pl.BlockSpec((tm, tm), const),
        ],
        out_specs=(
            pl.BlockSpec((tm, d), row),
            pl.BlockSpec((tm * (d // LANES), LANES), row),
            pl.BlockSpec((tm, LANES), row),
            pl.BlockSpec((SUBLANES, tm), lambda i: (0, i)),
            pl.BlockSpec((SUBLANES, LANES), const),
        ),
        out_shape=(
            jax.ShapeDtypeStruct((t, d), F32),
            jax.ShapeDtypeStruct((t * (d // LANES), LANES), F32),
            jax.ShapeDtypeStruct((t, LANES), F32),
            jax.ShapeDtypeStruct((SUBLANES, t), F32),
            jax.ShapeDtypeStruct((SUBLANES, LANES), F32),
        ),
        scratch_shapes=[pltpu.VMEM((SUBLANES, LANES), F32)],
        compiler_params=_params("arbitrary"),
        name="out_route",
    )(x2, yr, ya, g_att, w_out, g_ffn, wr, b_rt, ltri)


def _positions_kernel(off_ref, meta_ref, pos_ref):
    m = meta_ref[...]
    ids = m[R_ID0:R_ID1 + 1, :]
    acc = m[R_RANK0:R_RANK1 + 1, :]
    for e in range(N_EXPERTS):
        acc = acc + jnp.where(ids == float(e), off_ref[e].astype(F32), 0.0)
    pos_ref[...] = acc.astype(jnp.int32)


def _positions(off, meta):
    t = meta.shape[1]
    return pl.pallas_call(
        _positions_kernel,
        grid_spec=pltpu.PrefetchScalarGridSpec(
            num_scalar_prefetch=1,
            grid=(1,),
            in_specs=[pl.BlockSpec((SUBLANES, t), lambda i, off_ref: (0, 0))],
            out_specs=pl.BlockSpec((TOP_K, t), lambda i, off_ref: (0, 0)),
        ),
        out_shape=jax.ShapeDtypeStruct((TOP_K, t), jnp.int32),
        compiler_params=_params("arbitrary"),
        name="positions",
    )(off, meta)


def _dispatch_kernel(pos_ref, xn_ref, xs_ref, sem, *, n_tok, tm, chunks):
    base = pl.program_id(0) * tm

    def issue(r, carry):
        for k in range(TOP_K):
            _token_copy(xn_ref, r, xs_ref, pos_ref[k * n_tok + base + r], sem, chunks).start(priority=k % 2)
        return carry

    def drain(r, carry):
        for k in range(TOP_K):
            _token_copy(xn_ref, 0, xs_ref, 0, sem, chunks).wait()
        return carry

    lax.fori_loop(0, tm, issue, 0, unroll=DMA_UNROLL)
    lax.fori_loop(0, tm, drain, 0, unroll=DMA_UNROLL)


def _dispatch(pos, xn_tiles, t):
    chunks = xn_tiles.shape[0] // t
    tm = DISPATCH_BLOCK
    return pl.pallas_call(
        functools.partial(_dispatch_kernel, n_tok=t, tm=tm, chunks=chunks),
        grid_spec=pltpu.PrefetchScalarGridSpec(
            num_scalar_prefetch=1,
            grid=(t // tm,),
            in_specs=[pl.BlockSpec((tm * chunks, LANES), lambda i, pos_ref: (i, 0))],
            out_specs=pl.BlockSpec(memory_space=pl.ANY),
            scratch_shapes=[pltpu.SemaphoreType.DMA(())],
        ),
        out_shape=jax.ShapeDtypeStruct((TOP_K * t * chunks, LANES), F32),
        compiler_params=_params("arbitrary"),
        name="dispatch",
    )(pos, xn_tiles)


def _ffn_kernel(tile_ref, exp_ref, lo_ref, hi_ref, first_ref, init_ref, next_ref, slot_ref,
                xs_ref, wg_ref, wu_ref, wd_ref, ys_ref, wg_f, wu_f, wd_f, wg_s, wu_s, wd_s, sem, *, tm, layer):
    g = pl.program_id(0)
    chunks = xs_ref.shape[0] // tm

    def weight_copies(expert, slot):
        return [pltpu.make_async_copy(w.at[layer, expert], buf.at[slot], sem.at[slot])
                for w, buf in ((wg_ref, wg_f), (wu_ref, wu_f), (wd_ref, wd_f))]

    @pl.when(g == 0)
    def _():
        for cp in weight_copies(exp_ref[0], slot_ref[0]):
            cp.start()

    @pl.when(first_ref[g] == 1)
    def _():
        slot = slot_ref[g]
        for cp in weight_copies(exp_ref[g], slot):
            cp.wait()
        wg_s[...] = wg_f[slot].astype(BF16)
        wu_s[...] = wu_f[slot].astype(BF16)
        wd_s[...] = wd_f[slot].astype(BF16)

        @pl.when(next_ref[g] >= 0)
        def _():
            for cp in weight_copies(next_ref[g], 1 - slot):
                cp.start()

    lo = lo_ref[g]
    hi = hi_ref[g]

    @pl.when(hi > lo)
    def _():
        x = _load_token_tiles(xs_ref, tm, chunks).astype(BF16)
        hg = jnp.dot(x, wg_s[...], preferred_element_type=F32)
        hu = jnp.dot(x, wu_s[...], preferred_element_type=F32)
        h = (hg * _sigmoid(hg)) * hu
        y = jnp.dot(h.astype(BF16), wd_s[...], preferred_element_type=F32)
        row = lax.broadcasted_iota(jnp.int32, y.shape, 0)
        mine = (row >= lo) & (row < hi)

        @pl.when(init_ref[g] == 1)
        def _():
            _store_token_tiles(ys_ref, jnp.where(mine, y, 0.0))

        @pl.when(init_ref[g] == 0)
        def _():
            _store_token_tiles(ys_ref, jnp.where(mine, y, _load_token_tiles(ys_ref, tm, chunks)))


def _expert_ffn(meta, xs, w_gate, w_up, w_down, layer):
    d, d_e = w_gate.shape[-2:]
    chunks = d // LANES
    tm = FFN_TILE
    steps = meta[0].shape[0]
    x_map = lambda g, tile, *_: (tile[g], 0)
    return pl.pallas_call(
        functools.partial(_ffn_kernel, tm=tm, layer=layer),
        grid_spec=pltpu.PrefetchScalarGridSpec(
            num_scalar_prefetch=len(meta),
            grid=(steps,),
            in_specs=[
                pl.BlockSpec((tm * chunks, LANES), x_map),
                pl.BlockSpec(memory_space=pl.ANY),
                pl.BlockSpec(memory_space=pl.ANY),
                pl.BlockSpec(memory_space=pl.ANY),
            ],
            out_specs=pl.BlockSpec((tm * chunks, LANES), x_map),
            scratch_shapes=[
                pltpu.VMEM((2, d, d_e), F32),
                pltpu.VMEM((2, d, d_e), F32),
                pltpu.VMEM((2, d_e, d), F32),
                pltpu.VMEM((d, d_e), BF16),
                pltpu.VMEM((d, d_e), BF16),
                pltpu.VMEM((d_e, d), BF16),
                pltpu.SemaphoreType.DMA((2,)),
            ],
        ),
        out_shape=jax.ShapeDtypeStruct(xs.shape, F32),
        compiler_params=_params("arbitrary"),
        name="expert_ffn",
    )(*meta, xs, w_gate, w_up, w_down)


def _ffn_schedule(cnt, off, rows):
    tm = FFN_TILE
    n_tiles = rows // tm
    steps = n_tiles + N_EXPERTS - 1
    end = off + cnt
    first_tile = off // tm
    last_tile = jnp.where(cnt > 0, (end - 1) // tm, first_tile - 1)
    visits = last_tile - first_tile + 1
    vis_end = jnp.cumsum(visits)
    vis_start = vis_end - visits
    total = vis_end[-1]
    g = jnp.arange(steps, dtype=jnp.int32)
    gc = jnp.minimum(g, total - 1)
    exp = jnp.sum((vis_end[None, :] <= gc[:, None]).astype(jnp.int32), axis=1)
    is_exp = exp[:, None] == jnp.arange(N_EXPERTS, dtype=jnp.int32)[None, :]
    pick = lambda a: jnp.sum(jnp.where(is_exp, a[None, :], 0), axis=1)
    tile = pick(first_tile) + (gc - pick(vis_start))
    valid = g < total
    lo = jnp.where(valid, jnp.maximum(pick(off), tile * tm) - tile * tm, 0)
    hi = jnp.where(valid, jnp.minimum(pick(end), (tile + 1) * tm) - tile * tm, 0)
    prev_exp = jnp.concatenate([jnp.full((1,), -1, jnp.int32), exp[:-1]])
    prev_tile = jnp.concatenate([jnp.full((1,), -1, jnp.int32), tile[:-1]])
    first = (valid & (exp != prev_exp)).astype(jnp.int32)
    init = (valid & (tile != prev_tile)).astype(jnp.int32)
    ids = jnp.arange(N_EXPERTS, dtype=jnp.int32)
    later = (visits > 0)[None, :] & (ids[None, :] > ids[:, None])
    next_e = jnp.min(jnp.where(later, ids[None, :], N_EXPERTS), axis=1)
    next_e = jnp.where(next_e < N_EXPERTS, next_e, -1)
    slot = (jnp.cumsum(first) - 1) % 2
    i32 = lambda a: a.astype(jnp.int32)
    return i32(tile), i32(exp), i32(lo), i32(hi), first, init, i32(pick(next_e)), i32(slot)


def _combine_kernel(pos_ref, x1_ref, route_ref, p_ref, ys_ref, gp_ref, wg_ref, bg_ref, wp_ref,
                    o_ref, buf_a, buf_b, sem, *, n_tok):
    tm, d = x1_ref.shape
    chunks = d // LANES
    i = pl.program_id(0)
    last = pl.num_programs(0) - 1
    bufs = (buf_a, buf_b)

    def start_row(block, slot, r):
        for k in range(TOP_K):
            src = pos_ref[k * n_tok + block * tm + r]
            _token_copy(ys_ref, src, bufs[slot].at[k], r, sem.at[slot], chunks).start(priority=k % 2)

    def drain(slot):
        def body(r, carry):
            for k in range(TOP_K):
                _token_copy(ys_ref, 0, bufs[slot].at[k], 0, sem.at[slot], chunks).wait()
            return carry
        lax.fori_loop(0, tm, body, 0, unroll=DMA_UNROLL)

    @pl.when(i == 0)
    def _():
        def body(r, carry):
            start_row(0, 0, r)
            return carry
        lax.fori_loop(0, tm, body, 0, unroll=DMA_UNROLL)

    def step(slot):
        drain(slot)
        for r in range(tm):
            start_row(jnp.minimum(i + 1, last), 1 - slot, r)
        rec = route_ref[...]
        w0 = rec[:, R_W0:R_W0 + 1]
        w1 = rec[:, R_W1:R_W1 + 1]
        y0 = _load_token_tiles(bufs[slot].at[0], tm, chunks)
        y1 = _load_token_tiles(bufs[slot].at[1], tm, chunks)
        x2 = x1_ref[...] + (y0 * w0 + y1 * w1)
        n = _rms(x2) * gp_ref[...]
        gate = _sigmoid(jnp.dot(n.astype(BF16), wg_ref[...], preferred_element_type=F32) + bg_ref[...])
        pe = jnp.dot(p_ref[...].astype(BF16), wp_ref[...], preferred_element_type=F32)
        o_ref[...] = x2 + pe * gate

        @pl.when(i == last)
        def _():
            drain(1 - slot)

    for slot in range(2):
        pl.when(i % 2 == slot)(functools.partial(step, slot))


def _combine_ple(pos, x1, route, p3, ys, g_ple, w_gate, b_gate, w_proj, layer):
    t, d = x1.shape
    dp = p3.shape[-1]
    tm = COMBINE_BLOCK
    bpb = p3.shape[2] // tm
    row = lambda i, pos_ref: (i, 0)
    const = lambda i, pos_ref: (0, 0)
    return pl.pallas_call(
        functools.partial(_combine_kernel, n_tok=t),
        grid_spec=pltpu.PrefetchScalarGridSpec(
            num_scalar_prefetch=1,
            grid=(t // tm,),
            in_specs=[
                pl.BlockSpec((tm, d), row),
                pl.BlockSpec((tm, LANES), row),
                pl.BlockSpec((None, None, tm, dp), lambda i, pos_ref: (layer, i // bpb, i % bpb, 0)),
                pl.BlockSpec(memory_space=pl.ANY),
                pl.BlockSpec((1, d), const),
                pl.BlockSpec((d, d), const),
                pl.BlockSpec((1, d), const),
                pl.BlockSpec((dp, d), const),
            ],
            out_specs=pl.BlockSpec((tm, d), row),
            scratch_shapes=[
                pltpu.VMEM((TOP_K, tm * (d // LANES), LANES), F32),
                pltpu.VMEM((TOP_K, tm * (d // LANES), LANES), F32),
                pltpu.SemaphoreType.DMA((2,)),
            ],
        ),
        out_shape=jax.ShapeDtypeStruct((t, d), F32),
        compiler_params=_params("arbitrary"),
        name="combine_ple",
    )(pos, x1, route, p3, ys, g_ple, w_gate, b_gate, w_proj)


def _block_diag(w):
    nb, n, _ = w.shape
    eye = jnp.eye(nb, dtype=w.dtype)
    return jnp.einsum("hij,hg->higj", w, eye).reshape(nb * n, nb * n)


def kernel(x, p, mix_norm, w_in, b_forget, conv_w, conv_b, w_rgate, b_rgate, w_igate, b_igate, lru_lambda,
           q_norm, k_norm, lru_out_norm, att_out_norm, w_out, ffn_norm, w_group, b_group, w_router, b_router,
           w_exp_gate, w_exp_up, w_exp_down, ple_norm, w_ple_gate, b_ple_gate, w_ple_proj):
    batch, seq, d = x.shape
    depth = w_in.shape[0]
    d_rnn = conv_w.shape[-1]
    n_heads = b_forget.shape[-1]
    d_att = n_heads * HEAD_DIM
    t = batch * seq
    assert n_heads == SUBLANES and d_att % LANES == 0 and seq % SEQ_BLOCK == 0 and seq % Q_BLOCK == 0
    assert C_PIECES * SUBLANES <= LANES
    assert Q_BLOCK == SEQ_BLOCK and (TOP_K * t) % FFN_TILE == 0
    assert t % DISPATCH_BLOCK == 0 and seq % COMBINE_BLOCK == 0
    assert w_router.shape[-1] == N_EXPERTS and w_group.shape[-1] == N_GROUPS

    idx = jnp.arange(SEQ_BLOCK)
    tri_incl = (idx[:, None] <= idx[None, :]).astype(BF16)
    tri_strict = (idx[None, :] < idx[:, None]).astype(BF16)
    hid = jnp.arange(d_att) // HEAD_DIM
    gsum = (hid[:, None] == hid[None, :]).astype(BF16)
    row2 = lambda a: a.reshape(1, -1).astype(F32)
    n_in_pad = 2 * d_rnn + 3 * d_att + LANES
    p3 = p

    x2 = x.reshape(t, d)
    for i in range(depth):
        w_pad = jnp.pad(w_in[i], ((0, 0), (0, n_in_pad - w_in.shape[-1]))).astype(BF16)
        qg = row2(jnp.tile(q_norm[i], n_heads) * (LOG2E / math.sqrt(HEAD_DIM)))
        kg = row2(jnp.tile(k_norm[i], n_heads))
        zx, zg, q, k, vt, cp = _in_proj(
            x2, row2(mix_norm[i]), w_pad, gsum, qg, kg, b_forget[i].reshape(n_heads, 1).astype(F32), tri_incl,
            batch=batch, seq=seq, d_rnn=d_rnn, d_att=d_att)

        w_gates = jnp.concatenate([_block_diag(w_rgate[i]), _block_diag(w_igate[i])], axis=1).astype(BF16)
        yr = _rglru(zx.reshape(batch, seq, d_rnn), zg.reshape(batch, seq, d_rnn), conv_w[i].astype(F32),
                    row2(conv_b[i]), w_gates, row2(b_rgate[i]), row2(b_igate[i]), row2(lru_lambda[i]),
                    row2(lru_out_norm[i]))
        ya = _attention(q.reshape(batch, seq, d_att), k.reshape(batch, seq, d_att),
                        cp.reshape(batch, seq, LANES), vt)

        w_rt = jnp.pad(jnp.concatenate([w_router[i], w_group[i]], axis=1),
                       ((0, 0), (0, LANES - N_EXPERTS - N_GROUPS))).astype(F32)
        wr_hi, wr_lo = _split_bf16(w_rt)
        b_rt = jnp.pad(jnp.concatenate([b_router[i], b_group[i]]), (0, LANES - N_EXPERTS - N_GROUPS))
        x1, xn, route, meta, cnt = _out_route(
            x2, yr.reshape(t, d_rnn), ya.reshape(t, d_att), row2(att_out_norm[i]), w_out[i].astype(BF16),
            row2(ffn_norm[i]), jnp.concatenate([wr_hi, wr_lo], axis=1), row2(b_rt), tri_strict, d_rnn=d_rnn)

        cnt_e = cnt[0, :N_EXPERTS].astype(jnp.int32)
        off_e = jnp.cumsum(cnt_e) - cnt_e
        pos = _positions(off_e, meta).reshape(-1)

        xs = _dispatch(pos, xn, t)
        ys = _expert_ffn(_ffn_schedule(cnt_e, off_e, TOP_K * t), xs, w_exp_gate, w_exp_up, w_exp_down, i)
        x2 = _combine_ple(pos, x1, route, p3, ys, row2(ple_norm[i]), w_ple_gate[i].astype(BF16),
                          row2(b_ple_gate[i]), w_ple_proj[i].astype(BF16), i)
    return x2.reshape(batch, seq, d)
```

```python
import functools
import math

import jax
import jax.numpy as jnp
from jax import lax
from jax.experimental import pallas as pl
from jax.experimental.pallas import tpu as pltpu

F32 = jnp.float32
BF16 = jnp.bfloat16

EPS = 1e-6
LRU_C = 8.0
HEAD_DIM = 64
CONV_WIDTH = 4
N_GROUPS = 4
EXPERTS_PER_GROUP = 8
N_EXPERTS = N_GROUPS * EXPERTS_PER_GROUP
TOP_K = 2

LOG2E = math.log2(math.e)

LANES = 128
SUBLANES = 8
VMEM_LIMIT = 56 * 1024 * 1024
HEADS_PER_BLOCK = LANES // HEAD_DIM
VT_ROWS = HEAD_DIM + 16
C_PIECES = 3

SEQ_BLOCK = 512
Q_BLOCK = 512
FFN_TILE = 256
DISPATCH_BLOCK = 1024
COMBINE_BLOCK = 512
DMA_UNROLL = 8

R_ID0, R_ID1, R_W0, R_W1, R_RANK0, R_RANK1 = 0, 1, 2, 3, 4, 5


def _sigmoid(x):
    return 0.5 * jnp.tanh(0.5 * x) + 0.5


def _sqrt_unit(y):
    return y * lax.rsqrt(jnp.maximum(y, 1e-30))


def _softplus(x):
    return jnp.maximum(x, 0.0) + jnp.log1p(jnp.exp(-jnp.abs(x)))


def _gelu_tanh(x):
    c = math.sqrt(2.0 / math.pi)
    return x * (0.5 * (1.0 + jnp.tanh(c * (x + 0.044715 * (x * x * x)))))


def _rms(x):
    return x * lax.rsqrt(jnp.mean(x * x, axis=-1, keepdims=True) + EPS)


def _split_bf16(x):
    hi = x.astype(BF16)
    lo = (x - hi.astype(F32)).astype(BF16)
    return hi, lo


def _store_token_tiles(ref, x):
    n, d = x.shape
    chunks = d // LANES
    for j in range(chunks):
        ref[pl.ds(j, n, stride=chunks), :] = x[:, j * LANES:(j + 1) * LANES]


def _load_token_tiles(ref, n, chunks):
    return jnp.concatenate([ref[pl.ds(j, n, stride=chunks), :] for j in range(chunks)], axis=1)


def _token_copy(src_ref, src_tok, dst_ref, dst_tok, sem, chunks):
    def rows(tok):
        start = tok * chunks
        return pl.ds(start if isinstance(tok, int) else pl.multiple_of(start, chunks), chunks)
    return pltpu.make_async_copy(src_ref.at[rows(src_tok), :], dst_ref.at[rows(dst_tok), :], sem)


def _params(*sem):
    return pltpu.CompilerParams(dimension_semantics=sem, vmem_limit_bytes=VMEM_LIMIT)


def _in_proj_kernel(x_ref, g_ref, w_ref, gsum_ref, qg_ref, kg_ref, bf_ref, tri_ref,
                    zx_ref, zg_ref, q_ref, k_ref, vt_ref, cp_ref, carry_ref, *, blocks_per_seq, d_rnn, d_att):
    i = pl.program_id(0)
    tm = x_ref.shape[0]
    n = _rms(x_ref[...]) * g_ref[...]
    z = jnp.dot(n.astype(BF16), w_ref[...], preferred_element_type=F32)
    o = 0
    zx_ref[...] = z[:, o:o + d_rnn]; o += d_rnn
    zg_ref[...] = z[:, o:o + d_rnn]; o += d_rnn
    zq = z[:, o:o + d_att]; o += d_att
    zk = z[:, o:o + d_att]; o += d_att
    zv = z[:, o:o + d_att]; o += d_att
    zf = z[:, o:o + LANES]

    zvt = zv.T
    ones = jnp.ones((VT_ROWS - HEAD_DIM, tm), F32)
    for h in range(d_att // HEAD_DIM):
        vt_ref[0, h, 0] = jnp.concatenate([zvt[h * HEAD_DIM:(h + 1) * HEAD_DIM], ones], axis=0).astype(BF16)

    def head_norm(t, gain):
        ss = jnp.dot((t * t).astype(BF16), gsum_ref[...], preferred_element_type=F32)
        return t * lax.rsqrt(ss * (1.0 / HEAD_DIM) + EPS) * gain

    q_ref[...] = head_norm(zq, qg_ref[...]).astype(BF16)
    k_ref[...] = head_norm(zk, kg_ref[...]).astype(BF16)

    zft = zf.T[:SUBLANES, :] + bf_ref[...]
    logf = jnp.minimum(zft, 0.0) - jnp.log1p(jnp.exp(-jnp.abs(zft)))
    hi, lo = _split_bf16(logf)
    cs = (jnp.dot(hi, tri_ref[...], preferred_element_type=F32)
          + jnp.dot(lo, tri_ref[...], preferred_element_type=F32))

    @pl.when(i % blocks_per_seq == 0)
    def _():
        carry_ref[...] = jnp.zeros_like(carry_ref)

    c = cs + carry_ref[:, 0:1]
    carry_ref[...] = jnp.broadcast_to(c[:, tm - 1:tm], carry_ref.shape)

    cneg = c * (-LOG2E)
    pieces = []
    for _ in range(C_PIECES):
        piece = cneg.astype(BF16).astype(F32)
        pieces.append(piece)
        cneg = cneg - piece
    pieces.append(jnp.zeros((LANES - C_PIECES * SUBLANES, tm), F32))
    cp_ref[...] = jnp.concatenate(pieces, axis=0).T.astype(BF16)


def _in_proj(x2, g_mix, w_pad, gsum, qg, kg, bf, tri, *, batch, seq, d_rnn, d_att):
    t, d = x2.shape
    tm = SEQ_BLOCK
    bps = seq // tm
    n_pad = w_pad.shape[1]
    row = lambda i: (i, 0)
    const = lambda i: (0, 0)
    out_shape = (
        jax.ShapeDtypeStruct((t, d_rnn), F32),
        jax.ShapeDtypeStruct((t, d_rnn), F32),
        jax.ShapeDtypeStruct((t, d_att), BF16),
        jax.ShapeDtypeStruct((t, d_att), BF16),
        jax.ShapeDtypeStruct((batch, d_att // HEAD_DIM, bps, VT_ROWS, tm), BF16),
        jax.ShapeDtypeStruct((t, LANES), BF16),
    )
    return pl.pallas_call(
        functools.partial(_in_proj_kernel, blocks_per_seq=bps, d_rnn=d_rnn, d_att=d_att),
        grid=(t // tm,),
        in_specs=[
            pl.BlockSpec((tm, d), row),
            pl.BlockSpec((1, d), const),
            pl.BlockSpec((d, n_pad), const),
            pl.BlockSpec((d_att, d_att), const),
            pl.BlockSpec((1, d_att), const),
            pl.BlockSpec((1, d_att), const),
            pl.BlockSpec((SUBLANES, 1), const),
            pl.BlockSpec((tm, tm), const),
        ],
        out_specs=(
            pl.BlockSpec((tm, d_rnn), row),
            pl.BlockSpec((tm, d_rnn), row),
            pl.BlockSpec((tm, d_att), row),
            pl.BlockSpec((tm, d_att), row),
            pl.BlockSpec((1, d_att // HEAD_DIM, 1, VT_ROWS, tm), lambda i: (i // bps, 0, i % bps, 0, 0)),
            pl.BlockSpec((tm, LANES), row),
        ),
        out_shape=out_shape,
        scratch_shapes=[pltpu.VMEM((SUBLANES, LANES), F32)],
        compiler_params=_params("arbitrary"),
        name="in_proj",
    )(x2, g_mix, w_pad, gsum, qg, kg, bf, tri)


def _rglru_kernel(zx_ref, zg_ref, cw_ref, cb_ref, wg_ref, br_ref, bi_ref, lam_ref, go_ref,
                  y_ref, tail_ref, h_ref, a_s, u_s, hs_s, *, d_rnn):
    j = pl.program_id(1)
    ts = zx_ref.shape[1]

    @pl.when(j == 0)
    def _():
        tail_ref[...] = jnp.zeros_like(tail_ref)
        h_ref[...] = jnp.zeros_like(h_ref)

    x = zx_ref[0]
    tail = tail_ref[...]
    row8 = lax.broadcasted_iota(jnp.int32, (SUBLANES, d_rnn), 0)

    xc = x * cw_ref[CONV_WIDTH - 1:CONV_WIDTH, :] + cb_ref[...]
    for d in range(1, CONV_WIDTH):
        xr = pltpu.roll(x, d, 0)
        top = jnp.where(row8 < d, pltpu.roll(tail, d, 0), xr[:SUBLANES])
        xs = jnp.concatenate([top, xr[SUBLANES:]], axis=0)
        xc = xc + xs * cw_ref[CONV_WIDTH - 1 - d:CONV_WIDTH - d, :]
    tail_ref[...] = x[ts - SUBLANES:, :]

    gl = jnp.dot(xc.astype(BF16), wg_ref[...], preferred_element_type=F32)
    r = _sigmoid(gl[:, :d_rnn] + br_ref[...])
    ig = _sigmoid(gl[:, d_rnn:] + bi_ref[...])
    log_a = (-LRU_C) * r * _softplus(-lam_ref[...])
    a = jnp.exp(log_a)
    a_s[...] = a
    u_s[...] = _sqrt_unit(1.0 - a * a) * (ig * xc)

    def group(gidx, h):
        r0 = pl.multiple_of(gidx * SUBLANES, SUBLANES)
        a8 = a_s[pl.ds(r0, SUBLANES), :]
        b8 = u_s[pl.ds(r0, SUBLANES), :]
        for d in (1, 2, 4):
            keep = row8 >= d
            a_sh = jnp.where(keep, pltpu.roll(a8, d, 0), 1.0)
            b_sh = jnp.where(keep, pltpu.roll(b8, d, 0), 0.0)
            b8 = a8 * b_sh + b8
            a8 = a8 * a_sh
        h8 = a8 * h + b8
        hs_s[pl.ds(r0, SUBLANES), :] = h8
        return h8[SUBLANES - 1:SUBLANES, :]

    h_last = lax.fori_loop(0, ts // SUBLANES, group, h_ref[0:1, :], unroll=4)
    h_ref[...] = jnp.broadcast_to(h_last, h_ref.shape)

    y = hs_s[...] * _gelu_tanh(zg_ref[0])
    y_ref[0] = (_rms(y) * go_ref[...]).astype(BF16)


def _rglru(zx, zg, conv_w, conv_b, w_gates, b_r, b_i, lam, g_out):
    b, s, d_rnn = zx.shape
    ts = SEQ_BLOCK
    blk = lambda bi, j: (bi, j, 0)
    const = lambda bi, j: (0, 0)
    return pl.pallas_call(
        functools.partial(_rglru_kernel, d_rnn=d_rnn),
        grid=(b, s // ts),
        in_specs=[
            pl.BlockSpec((1, ts, d_rnn), blk),
            pl.BlockSpec((1, ts, d_rnn), blk),
            pl.BlockSpec((CONV_WIDTH, d_rnn), const),
            pl.BlockSpec((1, d_rnn), const),
            pl.BlockSpec((d_rnn, 2 * d_rnn), const),
            pl.BlockSpec((1, d_rnn), const),
            pl.BlockSpec((1, d_rnn), const),
            pl.BlockSpec((1, d_rnn), const),
            pl.BlockSpec((1, d_rnn), const),
        ],
        out_specs=pl.BlockSpec((1, ts, d_rnn), blk),
        out_shape=jax.ShapeDtypeStruct((b, s, d_rnn), BF16),
        scratch_shapes=[
            pltpu.VMEM((SUBLANES, d_rnn), F32),
            pltpu.VMEM((SUBLANES, d_rnn), F32),
            pltpu.VMEM((ts, d_rnn), F32),
            pltpu.VMEM((ts, d_rnn), F32),
            pltpu.VMEM((ts, d_rnn), F32),
        ],
        compiler_params=_params("arbitrary", "arbitrary"),
        name="rglru",
    )(zx, zg, conv_w, conv_b, w_gates, b_r, b_i, lam, g_out)


def _attn_kernel(q_ref, k_ref, cp_ref, vt_ref, o_ref, sa_ref, sb_ref, *, bq, bk):
    hp = pl.program_id(1)

    def q_block(qi, carry):
        _attn_q_block(qi, hp, q_ref, k_ref, cp_ref, vt_ref, o_ref, sa_ref, sb_ref, bq=bq, bk=bk)
        return carry

    lax.fori_loop(0, q_ref.shape[1] // bq, q_block, 0)


def _attn_q_block(qi, hp, q_ref, k_ref, cp_ref, vt_ref, o_ref, sa_ref, sb_ref, *, bq, bk):
    q0 = pl.multiple_of(qi * bq, bq)
    q2 = q_ref[0, pl.ds(q0, bq), :]
    lane = lax.broadcasted_iota(jnp.int32, (bq, LANES), 1)
    key = lax.broadcasted_iota(jnp.int32, (bk, bq), 0)
    query = qi * bq + lax.broadcasted_iota(jnp.int32, (bk, bq), 1)
    n_full = qi * (bq // bk)
    qx = []
    for j in range(HEADS_PER_BLOCK):
        head = hp * HEADS_PER_BLOCK + j
        qj = jnp.where(lane // HEAD_DIM == j, q2, jnp.zeros_like(q2))
        sel = jnp.where((lane < C_PIECES * SUBLANES) & (lane % SUBLANES == head), 1.0, 0.0).astype(BF16)
        qx.append(jnp.concatenate([qj, sel], axis=1))

    def scores(kb, s_ref):
        k0 = pl.multiple_of(kb * bk, bk)
        kx = jnp.concatenate([k_ref[0, pl.ds(k0, bk), :], cp_ref[0, pl.ds(k0, bk), :]], axis=1)
        for j in range(HEADS_PER_BLOCK):
            s_ref[j] = lax.dot_general(kx, qx[j], (((1,), (1,)), ((), ())), preferred_element_type=F32)

    def update(kb, s_ref, stats, masked):
        k0 = pl.multiple_of(kb * bk, bk)
        new = []
        for j in range(HEADS_PER_BLOCK):
            m, acc = stats[j]
            s = s_ref[j]
            if masked:
                s = jnp.where(key + k0 <= query, s, -jnp.inf)
            m_new = jnp.maximum(m, jnp.max(s, axis=0, keepdims=True))
            p = jnp.exp2(s - m_new)
            acc = jnp.exp2(m - m_new) * acc + jnp.dot(vt_ref[0, j, kb], p.astype(BF16),
                                                      preferred_element_type=F32)
            new.append((m_new, acc))
        return tuple(new)

    def finish(stats):
        halves = [acc[:HEAD_DIM] / acc[HEAD_DIM:HEAD_DIM + 1] for _, acc in stats]
        o_ref[0, pl.ds(q0, bq), :] = jnp.concatenate(halves, axis=0).T.astype(o_ref.dtype)

    def pair(i, stats):
        scores(2 * i + 1, sb_ref)
        stats = update(2 * i, sa_ref, stats, False)
        scores(2 * i + 2, sa_ref)
        return update(2 * i + 1, sb_ref, stats, False)

    stats = tuple((jnp.full((1, bq), -jnp.inf, F32), jnp.zeros((VT_ROWS, bq), F32)) for _ in range(HEADS_PER_BLOCK))
    scores(0, sa_ref)
    stats = lax.fori_loop(0, n_full // 2, pair, stats)

    @pl.when(n_full % 2 == 0)
    def _():
        finish(update(n_full, sa_ref, stats, True))

    @pl.when(n_full % 2 == 1)
    def _():
        scores(n_full, sb_ref)
        st = update(n_full - 1, sa_ref, stats, False)
        finish(update(n_full, sb_ref, st, True))


def _attention(q, k, cp, vt):
    b, s, d_att = q.shape
    bq, bk = Q_BLOCK, SEQ_BLOCK
    return pl.pallas_call(
        functools.partial(_attn_kernel, bq=bq, bk=bk),
        grid=(b, d_att // LANES),
        in_specs=[
            pl.BlockSpec((1, s, LANES), lambda bi, hp: (bi, 0, hp)),
            pl.BlockSpec((1, s, LANES), lambda bi, hp: (bi, 0, hp)),
            pl.BlockSpec((1, s, LANES), lambda bi, hp: (bi, 0, 0)),
            pl.BlockSpec((1, HEADS_PER_BLOCK, s // bk, VT_ROWS, bk), lambda bi, hp: (bi, hp, 0, 0, 0)),
        ],
        out_specs=pl.BlockSpec((1, s, LANES), lambda bi, hp: (bi, 0, hp)),
        out_shape=jax.ShapeDtypeStruct((b, s, d_att), BF16),
        scratch_shapes=[pltpu.VMEM((HEADS_PER_BLOCK, bk, bq), F32), pltpu.VMEM((HEADS_PER_BLOCK, bk, bq), F32)],
        compiler_params=_params("arbitrary", "arbitrary"),
        name="attention",
    )(q, k, cp, vt)


def _out_route_kernel(x_ref, yr_ref, ya_ref, ga_ref, wo_ref, gf_ref, wr_ref, br_ref, ltri_ref,
                      x1_ref, xn_ref, route_ref, meta_ref, cnt_ref, carry_ref, *, d_rnn):
    i = pl.program_id(0)
    tm = x_ref.shape[0]

    @pl.when(i == 0)
    def _():
        carry_ref[...] = jnp.zeros_like(carry_ref)

    ya = _rms(ya_ref[...].astype(F32)) * ga_ref[...]
    x1 = (x_ref[...]
          + jnp.dot(yr_ref[...], wo_ref[:d_rnn, :], preferred_element_type=F32)
          + jnp.dot(ya.astype(BF16), wo_ref[d_rnn:, :], preferred_element_type=F32))
    x1_ref[...] = x1
    xn = _rms(x1) * gf_ref[...]
    _store_token_tiles(xn_ref, xn)

    nh, nl = _split_bf16(xn)
    both = jnp.dot(nh, wr_ref[...], preferred_element_type=F32)
    logits = (both[:, :LANES] + both[:, LANES:]
              + jnp.dot(nl, wr_ref[:, :LANES], preferred_element_type=F32)) + br_ref[...]

    lane = lax.broadcasted_iota(jnp.int32, (tm, LANES), 1)
    lane_f = lane.astype(F32)
    big = float(LANES)

    def top1(vals):
        m = jnp.max(vals, axis=-1, keepdims=True)
        idx = jnp.min(jnp.where(vals == m, lane_f, big), axis=-1, keepdims=True)
        return m, idx

    is_group = (lane >= N_EXPERTS) & (lane < N_EXPERTS + N_GROUPS)
    gl = jnp.where(is_group, logits, -jnp.inf)
    g_max, g_lane = top1(gl)
    g_p = 1.0 / jnp.sum(jnp.exp(gl - g_max), axis=-1, keepdims=True)
    g_idx = g_lane - float(N_EXPERTS)

    in_group = (lane < N_EXPERTS) & ((lane // EXPERTS_PER_GROUP).astype(F32) == g_idx)
    el = jnp.where(in_group, logits, -jnp.inf)
    m1, i1 = top1(el)
    el2 = jnp.where(lane_f == i1, -jnp.inf, el)
    m2, i2 = top1(el2)
    e2 = jnp.exp(m2 - m1)
    w1 = g_p / (1.0 + e2)
    w2 = g_p * e2 / (1.0 + e2)

    onehot = jnp.where((lane_f == i1) | (lane_f == i2), 1.0, 0.0)
    before = jnp.dot(ltri_ref[...], onehot.astype(BF16), preferred_element_type=F32) + carry_ref[0:1, :]
    rank1 = jnp.sum(jnp.where(lane_f == i1, before, 0.0), axis=-1, keepdims=True)
    rank2 = jnp.sum(jnp.where(lane_f == i2, before, 0.0), axis=-1, keepdims=True)
    carry = carry_ref[0:1, :] + jnp.sum(onehot, axis=0, keepdims=True)
    carry_ref[...] = jnp.broadcast_to(carry, carry_ref.shape)
    cnt_ref[...] = jnp.broadcast_to(carry, cnt_ref.shape)

    rec = jnp.zeros((tm, LANES), F32)
    for ln, val in ((R_ID0, i1), (R_ID1, i2), (R_W0, w1), (R_W1, w2), (R_RANK0, rank1), (R_RANK1, rank2)):
        rec = jnp.where(lane == ln, val, rec)
    route_ref[...] = rec
    meta_ref[...] = rec.T[:SUBLANES, :]


def _out_route(x2, yr, ya, g_att, w_out, g_ffn, wr, b_rt, ltri, *, d_rnn):
    t, d = x2.shape
    tm = SEQ_BLOCK
    d_att = ya.shape[1]
    row = lambda i: (i, 0)
    const = lambda i: (0, 0)
    return pl.pallas_call(
        functools.partial(_out_route_kernel, d_rnn=d_rnn),
        grid=(t // tm,),
        in_specs=[
            pl.BlockSpec((tm, d), row),
            pl.BlockSpec((tm, d_rnn), row),
            pl.BlockSpec((tm, d_att), row),
            pl.BlockSpec((1, d_att), const),
            pl.BlockSpec((d_rnn + d_att, d), const),
            pl.BlockSpec((1, d), const),
            pl.BlockSpec((d, 2 * LANES), const),
            pl.BlockSpec((1, LANES), const),
            pl.BlockSpec((tm, tm), const),
        ],
        out_specs=(
            pl.BlockSpec((tm, d), row),
            pl.BlockSpec((tm * (d // LANES), LANES), row),
            pl.BlockSpec((tm, LANES), row),
            pl.BlockSpec((SUBLANES, tm), lambda i: (0, i)),
            pl.BlockSpec((SUBLANES, LANES), const),
        ),
        out_shape=(
            jax.ShapeDtypeStruct((t, d), F32),
            jax.ShapeDtypeStruct((t * (d // LANES), LANES), F32),
            jax.ShapeDtypeStruct((t, LANES), F32),
            jax.ShapeDtypeStruct((SUBLANES, t), F32),
            jax.ShapeDtypeStruct((SUBLANES, LANES), F32),
        ),
        scratch_shapes=[pltpu.VMEM((SUBLANES, LANES), F32)],
        compiler_params=_params("arbitrary"),
        name="out_route",
    )(x2, yr, ya, g_att, w_out, g_ffn, wr, b_rt, ltri)


def _positions_kernel(off_ref, meta_ref, pos_ref):
    m = meta_ref[...]
    ids = m[R_ID0:R_ID1 + 1, :]
    acc = m[R_RANK0:R_RANK1 + 1, :]
    for e in range(N_EXPERTS):
        acc = acc + jnp.where(ids == float(e), off_ref[e].astype(F32), 0.0)
    pos_ref[...] = acc.astype(jnp.int32)


def _positions(off, meta):
    t = meta.shape[1]
    return pl.pallas_call(
        _positions_kernel,
        grid_spec=pltpu.PrefetchScalarGridSpec(
            num_scalar_prefetch=1,
            grid=(1,),
            in_specs=[pl.BlockSpec((SUBLANES, t), lambda i, off_ref: (0, 0))],
            out_specs=pl.BlockSpec((TOP_K, t), lambda i, off_ref: (0, 0)),
        ),
        out_shape=jax.ShapeDtypeStruct((TOP_K, t), jnp.int32),
        compiler_params=_params("arbitrary"),
        name="positions",
    )(off, meta)


def _dispatch_kernel(pos_ref, xn_ref, xs_ref, sem, *, n_tok, tm, chunks):
    base = pl.program_id(0) * tm

    def issue(r, carry):
        for k in range(TOP_K):
            _token_copy(xn_ref, r, xs_ref, pos_ref[k * n_tok + base + r], sem, chunks).start(priority=k % 2)
        return carry

    def drain(r, carry):
        for k in range(TOP_K):
            _token_copy(xn_ref, 0, xs_ref, 0, sem, chunks).wait()
        return carry

    lax.fori_loop(0, tm, issue, 0, unroll=DMA_UNROLL)
    lax.fori_loop(0, tm, drain, 0, unroll=DMA_UNROLL)


def _dispatch(pos, xn_tiles, t):
    chunks = xn_tiles.shape[0] // t
    tm = DISPATCH_BLOCK
    return pl.pallas_call(
        functools.partial(_dispatch_kernel, n_tok=t, tm=tm, chunks=chunks),
        grid_spec=pltpu.PrefetchScalarGridSpec(
            num_scalar_prefetch=1,
            grid=(t // tm,),
            in_specs=[pl.BlockSpec((tm * chunks, LANES), lambda i, pos_ref: (i, 0))],
            out_specs=pl.BlockSpec(memory_space=pl.ANY),
            scratch_shapes=[pltpu.SemaphoreType.DMA(())],
        ),
        out_shape=jax.ShapeDtypeStruct((TOP_K * t * chunks, LANES), F32),
        compiler_params=_params("arbitrary"),
        name="dispatch",
    )(pos, xn_tiles)


def _ffn_kernel(tile_ref, exp_ref, lo_ref, hi_ref, first_ref, init_ref, next_ref, slot_ref,
                xs_ref, wg_ref, wu_ref, wd_ref, ys_ref, wg_f, wu_f, wd_f, wg_s, wu_s, wd_s, sem, *, tm, layer):
    g = pl.program_id(0)
    chunks = xs_ref.shape[0] // tm

    def weight_copies(expert, slot):
        return [pltpu.make_async_copy(w.at[layer, expert], buf.at[slot], sem.at[slot])
                for w, buf in ((wg_ref, wg_f), (wu_ref, wu_f), (wd_ref, wd_f))]

    @pl.when(g == 0)
    def _():
        for cp in weight_copies(exp_ref[0], slot_ref[0]):
            cp.start()

    @pl.when(first_ref[g] == 1)
    def _():
        slot = slot_ref[g]
        for cp in weight_copies(exp_ref[g], slot):
            cp.wait()
        wg_s[...] = wg_f[slot].astype(BF16)
        wu_s[...] = wu_f[slot].astype(BF16)
        wd_s[...] = wd_f[slot].astype(BF16)

        @pl.when(next_ref[g] >= 0)
        def _():
            for cp in weight_copies(next_ref[g], 1 - slot):
                cp.start()

    lo = lo_ref[g]
    hi = hi_ref[g]

    @pl.when(hi > lo)
    def _():
        x = _load_token_tiles(xs_ref, tm, chunks).astype(BF16)
        hg = jnp.dot(x, wg_s[...], preferred_element_type=F32)
        hu = jnp.dot(x, wu_s[...], preferred_element_type=F32)
        h = (hg * _sigmoid(hg)) * hu
        y = jnp.dot(h.astype(BF16), wd_s[...], preferred_element_type=F32)
        row = lax.broadcasted_iota(jnp.int32, y.shape, 0)
        mine = (row >= lo) & (row < hi)

        @pl.when(init_ref[g] == 1)
        def _():
            _store_token_tiles(ys_ref, jnp.where(mine, y, 0.0))

        @pl.when(init_ref[g] == 0)
        def _():
            _store_token_tiles(ys_ref, jnp.where(mine, y, _load_token_tiles(ys_ref, tm, chunks)))


def _expert_ffn(meta, xs, w_gate, w_up, w_down, layer):
    d, d_e = w_gate.shape[-2:]
    chunks = d // LANES
    tm = FFN_TILE
    steps = meta[0].shape[0]
    x_map = lambda g, tile, *_: (tile[g], 0)
    return pl.pallas_call(
        functools.partial(_ffn_kernel, tm=tm, layer=layer),
        grid_spec=pltpu.PrefetchScalarGridSpec(
            num_scalar_prefetch=len(meta),
            grid=(steps,),
            in_specs=[
                pl.BlockSpec((tm * chunks, LANES), x_map),
                pl.BlockSpec(memory_space=pl.ANY),
                pl.BlockSpec(memory_space=pl.ANY),
                pl.BlockSpec(memory_space=pl.ANY),
            ],
            out_specs=pl.BlockSpec((tm * chunks, LANES), x_map),
            scratch_shapes=[
                pltpu.VMEM((2, d, d_e), F32),
                pltpu.VMEM((2, d, d_e), F32),
                pltpu.VMEM((2, d_e, d), F32),
                pltpu.VMEM((d, d_e), BF16),
                pltpu.VMEM((d, d_e), BF16),
                pltpu.VMEM((d_e, d), BF16),
                pltpu.SemaphoreType.DMA((2,)),
            ],
        ),
        out_shape=jax.ShapeDtypeStruct(xs.shape, F32),
        compiler_params=_params("arbitrary"),
        name="expert_ffn",
    )(*meta, xs, w_gate, w_up, w_down)


def _ffn_schedule(cnt, off, rows):
    tm = FFN_TILE
    n_tiles = rows // tm
    steps = n_tiles + N_EXPERTS - 1
    end = off + cnt
    first_tile = off // tm
    last_tile = jnp.where(cnt > 0, (end - 1) // tm, first_tile - 1)
    visits = last_tile - first_tile + 1
    vis_end = jnp.cumsum(visits)
    vis_start = vis_end - visits
    total = vis_end[-1]
    g = jnp.arange(steps, dtype=jnp.int32)
    gc = jnp.minimum(g, total - 1)
    exp = jnp.sum((vis_end[None, :] <= gc[:, None]).astype(jnp.int32), axis=1)
    is_exp = exp[:, None] == jnp.arange(N_EXPERTS, dtype=jnp.int32)[None, :]
    pick = lambda a: jnp.sum(jnp.where(is_exp, a[None, :], 0), axis=1)
    tile = pick(first_tile) + (gc - pick(vis_start))
    valid = g < total
    lo = jnp.where(valid, jnp.maximum(pick(off), tile * tm) - tile * tm, 0)
    hi = jnp.where(valid, jnp.minimum(pick(end), (tile + 1) * tm) - tile * tm, 0)
    prev_exp = jnp.concatenate([jnp.full((1,), -1, jnp.int32), exp[:-1]])
    prev_tile = jnp.concatenate([jnp.full((1,), -1, jnp.int32), tile[:-1]])
    first = (valid & (exp != prev_exp)).astype(jnp.int32)
    init = (valid & (tile != prev_tile)).astype(jnp.int32)
    ids = jnp.arange(N_EXPERTS, dtype=jnp.int32)
    later = (visits > 0)[None, :] & (ids[None, :] > ids[:, None])
    next_e = jnp.min(jnp.where(later, ids[None, :], N_EXPERTS), axis=1)
    next_e = jnp.where(next_e < N_EXPERTS, next_e, -1)
    slot = (jnp.cumsum(first) - 1) % 2
    i32 = lambda a: a.astype(jnp.int32)
    return i32(tile), i32(exp), i32(lo), i32(hi), first, init, i32(pick(next_e)), i32(slot)


def _combine_kernel(pos_ref, x1_ref, route_ref, p_ref, ys_ref, gp_ref, wg_ref, bg_ref, wp_ref,
                    o_ref, buf_a, buf_b, sem, *, n_tok):
    tm, d = x1_ref.shape
    chunks = d // LANES
    i = pl.program_id(0)
    last = pl.num_programs(0) - 1
    bufs = (buf_a, buf_b)

    def start_row(block, slot, r):
        for k in range(TOP_K):
            src = pos_ref[k * n_tok + block * tm + r]
            _token_copy(ys_ref, src, bufs[slot].at[k], r, sem.at[slot], chunks).start(priority=k % 2)

    def drain(slot):
        def body(r, carry):
            for k in range(TOP_K):
                _token_copy(ys_ref, 0, bufs[slot].at[k], 0, sem.at[slot], chunks).wait()
            return carry
        lax.fori_loop(0, tm, body, 0, unroll=DMA_UNROLL)

    @pl.when(i == 0)
    def _():
        def body(r, carry):
            start_row(0, 0, r)
            return carry
        lax.fori_loop(0, tm, body, 0, unroll=DMA_UNROLL)

    def step(slot):
        drain(slot)
        for r in range(tm):
            start_row(jnp.minimum(i + 1, last), 1 - slot, r)
        rec = route_ref[...]
        w0 = rec[:, R_W0:R_W0 + 1]
        w1 = rec[:, R_W1:R_W1 + 1]
        y0 = _load_token_tiles(bufs[slot].at[0], tm, chunks)
        y1 = _load_token_tiles(bufs[slot].at[1], tm, chunks)
        x2 = x1_ref[...] + (y0 * w0 + y1 * w1)
        n = _rms(x2) * gp_ref[...]
        gate = _sigmoid(jnp.dot(n.astype(BF16), wg_ref[...], preferred_element_type=F32) + bg_ref[...])
        pe = jnp.dot(p_ref[...].astype(BF16), wp_ref[...], preferred_element_type=F32)
        o_ref[...] = x2 + pe * gate

        @pl.when(i == last)
        def _():
            drain(1 - slot)

    for slot in range(2):
        pl.when(i % 2 == slot)(functools.partial(step, slot))


def _combine_ple(pos, x1, route, p3, ys, g_ple, w_gate, b_gate, w_proj, layer):
    t, d = x1.shape
    dp = p3.shape[-1]
    tm = COMBINE_BLOCK
    bpb = p3.shape[2] // tm
    row = lambda i, pos_ref: (i, 0)
    const = lambda i, pos_ref: (0, 0)
    return pl.pallas_call(
        functools.partial(_combine_kernel, n_tok=t),
        grid_spec=pltpu.PrefetchScalarGridSpec(
            num_scalar_prefetch=1,
            grid=(t // tm,),
            in_specs=[
                pl.BlockSpec((tm, d), row),
                pl.BlockSpec((tm, LANES), row),
                pl.BlockSpec((None, None, tm, dp), lambda i, pos_ref: (layer, i // bpb, i % bpb, 0)),
                pl.BlockSpec(memory_space=pl.ANY),
                pl.BlockSpec((1, d), const),
                pl.BlockSpec((d, d), const),
                pl.BlockSpec((1, d), const),
                pl.BlockSpec((dp, d), const),
            ],
            out_specs=pl.BlockSpec((tm, d), row),
            scratch_shapes=[
                pltpu.VMEM((TOP_K, tm * (d // LANES), LANES), F32),
                pltpu.VMEM((TOP_K, tm * (d // LANES), LANES), F32),
                pltpu.SemaphoreType.DMA((2,)),
            ],
        ),
        out_shape=jax.ShapeDtypeStruct((t, d), F32),
        compiler_params=_params("arbitrary"),
        name="combine_ple",
    )(pos, x1, route, p3, ys, g_ple, w_gate, b_gate, w_proj)


def _block_diag(w):
    nb, n, _ = w.shape
    eye = jnp.eye(nb, dtype=w.dtype)
    return jnp.einsum("hij,hg->higj", w, eye).reshape(nb * n, nb * n)


def kernel(x, p, mix_norm, w_in, b_forget, conv_w, conv_b, w_rgate, b_rgate, w_igate, b_igate, lru_lambda,
           q_norm, k_norm, lru_out_norm, att_out_norm, w_out, ffn_norm, w_group, b_group, w_router, b_router,
           w_exp_gate, w_exp_up, w_exp_down, ple_norm, w_ple_gate, b_ple_gate, w_ple_proj):
    batch, seq, d = x.shape
    depth = w_in.shape[0]
    d_rnn = conv_w.shape[-1]
    n_heads = b_forget.shape[-1]
    d_att = n_heads * HEAD_DIM
    t = batch * seq
    assert n_heads == SUBLANES and d_att % LANES == 0 and seq % SEQ_BLOCK == 0 and seq % Q_BLOCK == 0
    assert C_PIECES * SUBLANES <= LANES
    assert Q_BLOCK == SEQ_BLOCK and (TOP_K * t) % FFN_TILE == 0
    assert t % DISPATCH_BLOCK == 0 and seq % COMBINE_BLOCK == 0
    assert w_router.shape[-1] == N_EXPERTS and w_group.shape[-1] == N_GROUPS

    idx = jnp.arange(SEQ_BLOCK)
    tri_incl = (idx[:, None] <= idx[None, :]).astype(BF16)
    tri_strict = (idx[None, :] < idx[:, None]).astype(BF16)
    hid = jnp.arange(d_att) // HEAD_DIM
    gsum = (hid[:, None] == hid[None, :]).astype(BF16)
    row2 = lambda a: a.reshape(1, -1).astype(F32)
    n_in_pad = 2 * d_rnn + 3 * d_att + LANES
    p3 = p

    x2 = x.reshape(t, d)
    for i in range(depth):
        w_pad = jnp.pad(w_in[i], ((0, 0), (0, n_in_pad - w_in.shape[-1]))).astype(BF16)
        qg = row2(jnp.tile(q_norm[i], n_heads) * (LOG2E / math.sqrt(HEAD_DIM)))
        kg = row2(jnp.tile(k_norm[i], n_heads))
        zx, zg, q, k, vt, cp = _in_proj(
            x2, row2(mix_norm[i]), w_pad, gsum, qg, kg, b_forget[i].reshape(n_heads, 1).astype(F32), tri_incl,
            batch=batch, seq=seq, d_rnn=d_rnn, d_att=d_att)

        w_gates = jnp.concatenate([_block_diag(w_rgate[i]), _block_diag(w_igate[i])], axis=1).astype(BF16)
        yr = _rglru(zx.reshape(batch, seq, d_rnn), zg.reshape(batch, seq, d_rnn), conv_w[i].astype(F32),
                    row2(conv_b[i]), w_gates, row2(b_rgate[i]), row2(b_igate[i]), row2(lru_lambda[i]),
                    row2(lru_out_norm[i]))
        ya = _attention(q.reshape(batch, seq, d_att), k.reshape(batch, seq, d_att),
                        cp.reshape(batch, seq, LANES), vt)

        w_rt = jnp.pad(jnp.concatenate([w_router[i], w_group[i]], axis=1),
                       ((0, 0), (0, LANES - N_EXPERTS - N_GROUPS))).astype(F32)
        wr_hi, wr_lo = _split_bf16(w_rt)
        b_rt = jnp.pad(jnp.concatenate([b_router[i], b_group[i]]), (0, LANES - N_EXPERTS - N_GROUPS))
        x1, xn, route, meta, cnt = _out_route(
            x2, yr.reshape(t, d_rnn), ya.reshape(t, d_att), row2(att_out_norm[i]), w_out[i].astype(BF16),
            row2(ffn_norm[i]), jnp.concatenate([wr_hi, wr_lo], axis=1), row2(b_rt), tri_strict, d_rnn=d_rnn)

        cnt_e = cnt[0, :N_EXPERTS].astype(jnp.int32)
        off_e = jnp.cumsum(cnt_e) - cnt_e
        pos = _positions(off_e, meta).reshape(-1)

        xs = _dispatch(pos, xn, t)
        ys = _expert_ffn(_ffn_schedule(cnt_e, off_e, TOP_K * t), xs, w_exp_gate, w_exp_up, w_exp_down, i)
        x2 = _combine_ple(pos, x1, route, p3, ys, row2(ple_norm[i]), w_ple_gate[i].astype(BF16),
                          row2(b_ple_gate[i]), w_ple_proj[i].astype(BF16), i)
    return x2.reshape(batch, seq, d)
```

```python
import functools
import math

import jax
import jax.numpy as jnp
from jax import lax
from jax.experimental import pallas as pl
from jax.experimental.pallas import tpu as pltpu

F32 = jnp.float32
BF16 = jnp.bfloat16

EPS = 1e-6
LRU_C = 8.0
HEAD_DIM = 64
CONV_WIDTH = 4
N_GROUPS = 4
EXPERTS_PER_GROUP = 8
N_EXPERTS = N_GROUPS * EXPERTS_PER_GROUP
TOP_K = 2

LOG2E = math.log2(math.e)

LANES = 128
SUBLANES = 8
VMEM_LIMIT = 56 * 1024 * 1024
HEADS_PER_BLOCK = LANES // HEAD_DIM
VT_ROWS = HEAD_DIM + 16
C_PIECES = 3
ROUTE_ROWS = 48

SEQ_BLOCK = 512
Q_BLOCK = 512
FFN_TILE = 256
DISPATCH_BLOCK = 1024
COMBINE_BLOCK = 512
DMA_UNROLL = 8

R_ID0, R_ID1, R_W0, R_W1, R_RANK0, R_RANK1 = 0, 1, 2, 3, 4, 5


def _sigmoid(x):
    return 0.5 * jnp.tanh(0.5 * x) + 0.5


def _sqrt_unit(y):
    return y * lax.rsqrt(jnp.maximum(y, 1e-30))


def _softplus(x):
    return jnp.maximum(x, 0.0) + jnp.log1p(jnp.exp(-jnp.abs(x)))


def _gelu_tanh(x):
    c = math.sqrt(2.0 / math.pi)
    return x * (0.5 * (1.0 + jnp.tanh(c * (x + 0.044715 * (x * x * x)))))


def _rms(x):
    return x * lax.rsqrt(jnp.mean(x * x, axis=-1, keepdims=True) + EPS)


def _split_bf16(x):
    hi = x.astype(BF16)
    lo = (x - hi.astype(F32)).astype(BF16)
    return hi, lo


def _store_token_tiles(ref, x):
    n, d = x.shape
    chunks = d // LANES
    for j in range(chunks):
        ref[pl.ds(j, n, stride=chunks), :] = x[:, j * LANES:(j + 1) * LANES]


def _load_token_tiles(ref, n, chunks):
    return jnp.concatenate([ref[pl.ds(j, n, stride=chunks), :] for j in range(chunks)], axis=1)


def _token_copy(src_ref, src_tok, dst_ref, dst_tok, sem, chunks):
    def rows(tok):
        start = tok * chunks
        return pl.ds(start if isinstance(tok, int) else pl.multiple_of(start, chunks), chunks)
    return pltpu.make_async_copy(src_ref.at[rows(src_tok), :], dst_ref.at[rows(dst_tok), :], sem)


def _params(*sem):
    return pltpu.CompilerParams(dimension_semantics=sem, vmem_limit_bytes=VMEM_LIMIT)


def _in_proj_kernel(x_ref, g_ref, w_ref, gsum_ref, qg_ref, kg_ref, bf_ref, tri_ref,
                    zx_ref, zg_ref, q_ref, k_ref, vt_ref, cp_ref, carry_ref, *, blocks_per_seq, d_rnn, d_att):
    i = pl.program_id(0)
    tm = x_ref.shape[0]
    n = _rms(x_ref[...]) * g_ref[...]
    z = jnp.dot(n.astype(BF16), w_ref[...], preferred_element_type=F32)
    o = 0
    zx_ref[...] = z[:, o:o + d_rnn]; o += d_rnn
    zg_ref[...] = z[:, o:o + d_rnn]; o += d_rnn
    zq = z[:, o:o + d_att]; o += d_att
    zk = z[:, o:o + d_att]; o += d_att
    zv = z[:, o:o + d_att]; o += d_att
    zf = z[:, o:o + LANES]

    zvt = zv.T
    ones = jnp.ones((VT_ROWS - HEAD_DIM, tm), F32)
    for h in range(d_att // HEAD_DIM):
        vt_ref[0, h, 0] = jnp.concatenate([zvt[h * HEAD_DIM:(h + 1) * HEAD_DIM], ones], axis=0).astype(BF16)

    def head_norm(t, gain):
        ss = jnp.dot((t * t).astype(BF16), gsum_ref[...], preferred_element_type=F32)
        return t * lax.rsqrt(ss * (1.0 / HEAD_DIM) + EPS) * gain

    q_ref[...] = head_norm(zq, qg_ref[...]).astype(BF16)
    k_ref[...] = head_norm(zk, kg_ref[...]).astype(BF16)

    zft = zf.T[:SUBLANES, :] + bf_ref[...]
    logf = jnp.minimum(zft, 0.0) - jnp.log1p(jnp.exp(-jnp.abs(zft)))
    hi, lo = _split_bf16(logf)
    cs = (jnp.dot(hi, tri_ref[...], preferred_element_type=F32)
          + jnp.dot(lo, tri_ref[...], preferred_element_type=F32))

    @pl.when(i % blocks_per_seq == 0)
    def _():
        carry_ref[...] = jnp.zeros_like(carry_ref)

    c = cs + carry_ref[:, 0:1]
    carry_ref[...] = jnp.broadcast_to(c[:, tm - 1:tm], carry_ref.shape)

    cneg = c * (-LOG2E)
    pieces = []
    for _ in range(C_PIECES):
        piece = cneg.astype(BF16).astype(F32)
        pieces.append(piece)
        cneg = cneg - piece
    pieces.append(jnp.zeros((LANES - C_PIECES * SUBLANES, tm), F32))
    cp_ref[...] = jnp.concatenate(pieces, axis=0).T.astype(BF16)


def _in_proj(x2, g_mix, w_pad, gsum, qg, kg, bf, tri, *, batch, seq, d_rnn, d_att):
    t, d = x2.shape
    tm = SEQ_BLOCK
    bps = seq // tm
    n_pad = w_pad.shape[1]
    row = lambda i: (i, 0)
    const = lambda i: (0, 0)
    out_shape = (
        jax.ShapeDtypeStruct((t, d_rnn), F32),
        jax.ShapeDtypeStruct((t, d_rnn), F32),
        jax.ShapeDtypeStruct((t, d_att), BF16),
        jax.ShapeDtypeStruct((t, d_att), BF16),
        jax.ShapeDtypeStruct((batch, d_att // HEAD_DIM, bps, VT_ROWS, tm), BF16),
        jax.ShapeDtypeStruct((t, LANES), BF16),
    )
    return pl.pallas_call(
        functools.partial(_in_proj_kernel, blocks_per_seq=bps, d_rnn=d_rnn, d_att=d_att),
        grid=(t // tm,),
        in_specs=[
            pl.BlockSpec((tm, d), row),
            pl.BlockSpec((1, d), const),
            pl.BlockSpec((d, n_pad), const),
            pl.BlockSpec((d_att, d_att), const),
            pl.BlockSpec((1, d_att), const),
            pl.BlockSpec((1, d_att), const),
            pl.BlockSpec((SUBLANES, 1), const),
            pl.BlockSpec((tm, tm), const),
        ],
        out_specs=(
            pl.BlockSpec((tm, d_rnn), row),
            pl.BlockSpec((tm, d_rnn), row),
            pl.BlockSpec((tm, d_att), row),
            pl.BlockSpec((tm, d_att), row),
            pl.BlockSpec((1, d_att // HEAD_DIM, 1, VT_ROWS, tm), lambda i: (i // bps, 0, i % bps, 0, 0)),
            pl.BlockSpec((tm, LANES), row),
        ),
        out_shape=out_shape,
        scratch_shapes=[pltpu.VMEM((SUBLANES, LANES), F32)],
        compiler_params=_params("arbitrary"),
        name="in_proj",
    )(x2, g_mix, w_pad, gsum, qg, kg, bf, tri)


def _rglru_kernel(zx_ref, zg_ref, cw_ref, cb_ref, wg_ref, br_ref, bi_ref, lam_ref, go_ref,
                  y_ref, tail_ref, h_ref, a_s, u_s, hs_s, *, d_rnn):
    j = pl.program_id(1)
    ts = zx_ref.shape[1]

    @pl.when(j == 0)
    def _():
        tail_ref[...] = jnp.zeros_like(tail_ref)
        h_ref[...] = jnp.zeros_like(h_ref)

    x = zx_ref[0]
    tail = tail_ref[...]
    row8 = lax.broadcasted_iota(jnp.int32, (SUBLANES, d_rnn), 0)

    xc = x * cw_ref[CONV_WIDTH - 1:CONV_WIDTH, :] + cb_ref[...]
    for d in range(1, CONV_WIDTH):
        xr = pltpu.roll(x, d, 0)
        top = jnp.where(row8 < d, pltpu.roll(tail, d, 0), xr[:SUBLANES])
        xs = jnp.concatenate([top, xr[SUBLANES:]], axis=0)
        xc = xc + xs * cw_ref[CONV_WIDTH - 1 - d:CONV_WIDTH - d, :]
    tail_ref[...] = x[ts - SUBLANES:, :]

    gl = jnp.dot(xc.astype(BF16), wg_ref[...], preferred_element_type=F32)
    r = _sigmoid(gl[:, :d_rnn] + br_ref[...])
    ig = _sigmoid(gl[:, d_rnn:] + bi_ref[...])
    log_a = (-LRU_C) * r * _softplus(-lam_ref[...])
    a = jnp.exp(log_a)
    a_s[...] = a
    u_s[...] = _sqrt_unit(1.0 - a * a) * (ig * xc)

    def group(gidx, h):
        r0 = pl.multiple_of(gidx * SUBLANES, SUBLANES)
        a8 = a_s[pl.ds(r0, SUBLANES), :]
        b8 = u_s[pl.ds(r0, SUBLANES), :]
        for d in (1, 2, 4):
            keep = row8 >= d
            a_sh = jnp.where(keep, pltpu.roll(a8, d, 0), 1.0)
            b_sh = jnp.where(keep, pltpu.roll(b8, d, 0), 0.0)
            b8 = a8 * b_sh + b8
            a8 = a8 * a_sh
        h8 = a8 * h + b8
        hs_s[pl.ds(r0, SUBLANES), :] = h8
        return h8[SUBLANES - 1:SUBLANES, :]

    h_last = lax.fori_loop(0, ts // SUBLANES, group, h_ref[0:1, :], unroll=4)
    h_ref[...] = jnp.broadcast_to(h_last, h_ref.shape)

    y = hs_s[...] * _gelu_tanh(zg_ref[0])
    y_ref[0] = (_rms(y) * go_ref[...]).astype(BF16)


def _rglru(zx, zg, conv_w, conv_b, w_gates, b_r, b_i, lam, g_out):
    b, s, d_rnn = zx.shape
    ts = SEQ_BLOCK
    blk = lambda bi, j: (bi, j, 0)
    const = lambda bi, j: (0, 0)
    return pl.pallas_call(
        functools.partial(_rglru_kernel, d_rnn=d_rnn),
        grid=(b, s // ts),
        in_specs=[
            pl.BlockSpec((1, ts, d_rnn), blk),
            pl.BlockSpec((1, ts, d_rnn), blk),
            pl.BlockSpec((CONV_WIDTH, d_rnn), const),
            pl.BlockSpec((1, d_rnn), const),
            pl.BlockSpec((d_rnn, 2 * d_rnn), const),
            pl.BlockSpec((1, d_rnn), const),
            pl.BlockSpec((1, d_rnn), const),
            pl.BlockSpec((1, d_rnn), const),
            pl.BlockSpec((1, d_rnn), const),
        ],
        out_specs=pl.BlockSpec((1, ts, d_rnn), blk),
        out_shape=jax.ShapeDtypeStruct((b, s, d_rnn), BF16),
        scratch_shapes=[
            pltpu.VMEM((SUBLANES, d_rnn), F32),
            pltpu.VMEM((SUBLANES, d_rnn), F32),
            pltpu.VMEM((ts, d_rnn), F32),
            pltpu.VMEM((ts, d_rnn), F32),
            pltpu.VMEM((ts, d_rnn), F32),
        ],
        compiler_params=_params("arbitrary", "arbitrary"),
        name="rglru",
    )(zx, zg, conv_w, conv_b, w_gates, b_r, b_i, lam, g_out)


def _attn_kernel(q_ref, k_ref, cp_ref, vt_ref, o_ref, sa_ref, sb_ref, *, bq, bk):
    hp = pl.program_id(1)

    def q_block(qi, carry):
        _attn_q_block(qi, hp, q_ref, k_ref, cp_ref, vt_ref, o_ref, sa_ref, sb_ref, bq=bq, bk=bk)
        return carry

    lax.fori_loop(0, q_ref.shape[1] // bq, q_block, 0)


def _attn_q_block(qi, hp, q_ref, k_ref, cp_ref, vt_ref, o_ref, sa_ref, sb_ref, *, bq, bk):
    q0 = pl.multiple_of(qi * bq, bq)
    q2 = q_ref[0, pl.ds(q0, bq), :]
    lane = lax.broadcasted_iota(jnp.int32, (bq, LANES), 1)
    key = lax.broadcasted_iota(jnp.int32, (bk, bq), 0)
    query = qi * bq + lax.broadcasted_iota(jnp.int32, (bk, bq), 1)
    n_full = qi * (bq // bk)
    qx = []
    for j in range(HEADS_PER_BLOCK):
        head = hp * HEADS_PER_BLOCK + j
        qj = jnp.where(lane // HEAD_DIM == j, q2, jnp.zeros_like(q2))
        sel = jnp.where((lane < C_PIECES * SUBLANES) & (lane % SUBLANES == head), 1.0, 0.0).astype(BF16)
        qx.append(jnp.concatenate([qj, sel], axis=1))

    def scores(kb, s_ref):
        k0 = pl.multiple_of(kb * bk, bk)
        kx = jnp.concatenate([k_ref[0, pl.ds(k0, bk), :], cp_ref[0, pl.ds(k0, bk), :]], axis=1)
        for j in range(HEADS_PER_BLOCK):
            s_ref[j] = lax.dot_general(kx, qx[j], (((1,), (1,)), ((), ())), preferred_element_type=F32)

    def update(kb, s_ref, stats, masked):
        k0 = pl.multiple_of(kb * bk, bk)
        new = []
        for j in range(HEADS_PER_BLOCK):
            m, acc = stats[j]
            s = s_ref[j]
            if masked:
                s = jnp.where(key + k0 <= query, s, -jnp.inf)
            m_new = jnp.maximum(m, jnp.max(s, axis=0, keepdims=True))
            p = jnp.exp2(s - m_new)
            acc = jnp.exp2(m - m_new) * acc + jnp.dot(vt_ref[0, j, kb], p.astype(BF16),
                                                      preferred_element_type=F32)
            new.append((m_new, acc))
        return tuple(new)

    def finish(stats):
        halves = [acc[:HEAD_DIM] / acc[HEAD_DIM:HEAD_DIM + 1] for _, acc in stats]
        o_ref[0, pl.ds(q0, bq), :] = jnp.concatenate(halves, axis=0).T.astype(o_ref.dtype)

    def pair(i, stats):
        scores(2 * i + 1, sb_ref)
        stats = update(2 * i, sa_ref, stats, False)
        scores(2 * i + 2, sa_ref)
        return update(2 * i + 1, sb_ref, stats, False)

    stats = tuple((jnp.full((1, bq), -jnp.inf, F32), jnp.zeros((VT_ROWS, bq), F32)) for _ in range(HEADS_PER_BLOCK))
    scores(0, sa_ref)
    stats = lax.fori_loop(0, n_full // 2, pair, stats)

    @pl.when(n_full % 2 == 0)
    def _():
        finish(update(n_full, sa_ref, stats, True))

    @pl.when(n_full % 2 == 1)
    def _():
        scores(n_full, sb_ref)
        st = update(n_full - 1, sa_ref, stats, False)
        finish(update(n_full, sb_ref, st, True))


def _attention(q, k, cp, vt):
    b, s, d_att = q.shape
    bq, bk = Q_BLOCK, SEQ_BLOCK
    return pl.pallas_call(
        functools.partial(_attn_kernel, bq=bq, bk=bk),
        grid=(b, d_att // LANES),
        in_specs=[
            pl.BlockSpec((1, s, LANES), lambda bi, hp: (bi, 0, hp)),
            pl.BlockSpec((1, s, LANES), lambda bi, hp: (bi, 0, hp)),
            pl.BlockSpec((1, s, LANES), lambda bi, hp: (bi, 0, 0)),
            pl.BlockSpec((1, HEADS_PER_BLOCK, s // bk, VT_ROWS, bk), lambda bi, hp: (bi, hp, 0, 0, 0)),
        ],
        out_specs=pl.BlockSpec((1, s, LANES), lambda bi, hp: (bi, 0, hp)),
        out_shape=jax.ShapeDtypeStruct((b, s, d_att), BF16),
        scratch_shapes=[pltpu.VMEM((HEADS_PER_BLOCK, bk, bq), F32), pltpu.VMEM((HEADS_PER_BLOCK, bk, bq), F32)],
        compiler_params=_params("arbitrary", "arbitrary"),
        name="attention",
    )(q, k, cp, vt)


def _out_route_kernel(x_ref, yr_ref, ya_ref, ga_ref, wo_ref, gf_ref, wr_ref, br_ref, utri_ref,
                      x1_ref, xn_ref, route_ref, meta_ref, cnt_ref, carry_ref, *, d_rnn):
    i = pl.program_id(0)
    tm = x_ref.shape[0]

    @pl.when(i == 0)
    def _():
        carry_ref[...] = jnp.zeros_like(carry_ref)

    ya = _rms(ya_ref[...].astype(F32)) * ga_ref[...]
    x1 = (x_ref[...]
          + jnp.dot(yr_ref[...], wo_ref[:d_rnn, :], preferred_element_type=F32)
          + jnp.dot(ya.astype(BF16), wo_ref[d_rnn:, :], preferred_element_type=F32))
    x1_ref[...] = x1
    xn = _rms(x1) * gf_ref[...]
    _store_token_tiles(xn_ref, xn)

    nh, nl = _split_bf16(xn)
    both = jnp.dot(nh, wr_ref[...], preferred_element_type=F32)
    logits = (both[:, :LANES] + both[:, LANES:]
              + jnp.dot(nl, wr_ref[:, :LANES], preferred_element_type=F32)) + br_ref[...]

    lt = logits.T[:ROUTE_ROWS, :]
    rowi = lax.broadcasted_iota(jnp.int32, (ROUTE_ROWS, tm), 0)
    row_f = rowi.astype(F32)
    big = float(ROUTE_ROWS)

    def top1(vals):
        m = jnp.max(vals, axis=0, keepdims=True)
        idx = jnp.min(jnp.where(vals == m, row_f, big), axis=0, keepdims=True)
        return m, idx

    is_group = (rowi >= N_EXPERTS) & (rowi < N_EXPERTS + N_GROUPS)
    gl = jnp.where(is_group, lt, -jnp.inf)
    g_max, g_row = top1(gl)
    g_p = 1.0 / jnp.sum(jnp.exp(gl - g_max), axis=0, keepdims=True)
    g_idx = g_row - float(N_EXPERTS)

    in_group = (rowi < N_EXPERTS) & ((rowi // EXPERTS_PER_GROUP).astype(F32) == g_idx)
    el = jnp.where(in_group, lt, -jnp.inf)
    m1, i1 = top1(el)
    el2 = jnp.where(row_f == i1, -jnp.inf, el)
    m2, i2 = top1(el2)
    e2 = jnp.exp(m2 - m1)
    w1 = g_p / (1.0 + e2)
    w2 = g_p * e2 / (1.0 + e2)

    onehot = jnp.where((row_f == i1) | (row_f == i2), 1.0, 0.0)
    before = jnp.dot(onehot.astype(BF16), utri_ref[...], preferred_element_type=F32) + carry_ref[:, 0:1]
    rank1 = jnp.sum(jnp.where(row_f == i1, before, 0.0), axis=0, keepdims=True)
    rank2 = jnp.sum(jnp.where(row_f == i2, before, 0.0), axis=0, keepdims=True)
    carry = carry_ref[:, 0:1] + jnp.sum(onehot, axis=1, keepdims=True)
    carry_ref[...] = jnp.broadcast_to(carry, carry_ref.shape)
    cnt_ref[...] = jnp.broadcast_to(carry, cnt_ref.shape)

    row8 = lax.broadcasted_iota(jnp.int32, (SUBLANES, tm), 0)
    meta = jnp.zeros((SUBLANES, tm), F32)
    for r, val in ((R_ID0, i1), (R_ID1, i2), (R_W0, w1), (R_W1, w2), (R_RANK0, rank1), (R_RANK1, rank2)):
        meta = jnp.where(row8 == r, val, meta)
    meta_ref[...] = meta
    route_ref[...] = jnp.concatenate([meta, jnp.zeros((LANES - SUBLANES, tm), F32)], axis=0).T


def _out_route(x2, yr, ya, g_att, w_out, g_ffn, wr, b_rt, ltri, *, d_rnn):
    t, d = x2.shape
    tm = SEQ_BLOCK
    d_att = ya.shape[1]
    row = lambda i: (i, 0)
    const = lambda i: (0, 0)
    return pl.pallas_call(
        functools.partial(_out_route_kernel, d_rnn=d_rnn),
        grid=(t // tm,),
        in_specs=[
            pl.BlockSpec((tm, d), row),
            pl.BlockSpec((tm, d_rnn), row),
            pl.BlockSpec((tm, d_att), row),
            pl.BlockSpec((1, d_att), const),
            pl.BlockSpec((d_rnn + d_att, d), const),
            pl.BlockSpec((1, d), const),
            pl.BlockSpec((d, 2 * LANES), const),
            pl.BlockSpec((1, LANES), const),
            pl.BlockSpec((tm, tm), const),
        ],
        out_specs=(
            pl.BlockSpec((tm, d), row),
            pl.BlockSpec((tm * (d // LANES), LANES), row),
            pl.BlockSpec((tm, LANES), row),
            pl.BlockSpec((SUBLANES, tm), lambda i: (0, i)),
            pl.BlockSpec((ROUTE_ROWS, LANES), const),
        ),
        out_shape=(
            jax.ShapeDtypeStruct((t, d), F32),
            jax.ShapeDtypeStruct((t * (d // LANES), LANES), F32),
            jax.ShapeDtypeStruct((t, LANES), F32),
            jax.ShapeDtypeStruct((SUBLANES, t), F32),
            jax.ShapeDtypeStruct((ROUTE_ROWS, LANES), F32),
        ),
        scratch_shapes=[pltpu.VMEM((ROUTE_ROWS, LANES), F32)],
        compiler_params=_params("arbitrary"),
        name="out_route",
    )(x2, yr, ya, g_att, w_out, g_ffn, wr, b_rt, ltri)


def _positions_kernel(off_ref, meta_ref, pos_ref):
    m = meta_ref[...]
    ids = m[R_ID0:R_ID1 + 1, :]
    acc = m[R_RANK0:R_RANK1 + 1, :]
    for e in range(N_EXPERTS):
        acc = acc + jnp.where(ids == float(e), off_ref[e].astype(F32), 0.0)
    pos_ref[...] = acc.astype(jnp.int32)


def _positions(off, meta):
    t = meta.shape[1]
    return pl.pallas_call(
        _positions_kernel,
        grid_spec=pltpu.PrefetchScalarGridSpec(
            num_scalar_prefetch=1,
            grid=(1,),
            in_specs=[pl.BlockSpec((SUBLANES, t), lambda i, off_ref: (0, 0))],
            out_specs=pl.BlockSpec((TOP_K, t), lambda i, off_ref: (0, 0)),
        ),
        out_shape=jax.ShapeDtypeStruct((TOP_K, t), jnp.int32),
        compiler_params=_params("arbitrary"),
        name="positions",
    )(off, meta)


def _dispatch_kernel(pos_ref, xn_ref, xs_ref, sem, *, n_tok, tm, chunks):
    base = pl.program_id(0) * tm

    def issue(r, carry):
        for k in range(TOP_K):
            _token_copy(xn_ref, r, xs_ref, pos_ref[k * n_tok + base + r], sem, chunks).start(priority=k % 2)
        return carry

    def drain(r, carry):
        for k in range(TOP_K):
            _token_copy(xn_ref, 0, xs_ref, 0, sem, chunks).wait()
        return carry

    lax.fori_loop(0, tm, issue, 0, unroll=DMA_UNROLL)
    lax.fori_loop(0, tm, drain, 0, unroll=DMA_UNROLL)


def _dispatch(pos, xn_tiles, t):
    chunks = xn_tiles.shape[0] // t
    tm = DISPATCH_BLOCK
    return pl.pallas_call(
        functools.partial(_dispatch_kernel, n_tok=t, tm=tm, chunks=chunks),
        grid_spec=pltpu.PrefetchScalarGridSpec(
            num_scalar_prefetch=1,
            grid=(t // tm,),
            in_specs=[pl.BlockSpec((tm * chunks, LANES), lambda i, pos_ref: (i, 0))],
            out_specs=pl.BlockSpec(memory_space=pl.ANY),
            scratch_shapes=[pltpu.SemaphoreType.DMA(())],
        ),
        out_shape=jax.ShapeDtypeStruct((TOP_K * t * chunks, LANES), F32),
        compiler_params=_params("arbitrary"),
        name="dispatch",
    )(pos, xn_tiles)


def _ffn_kernel(tile_ref, exp_ref, lo_ref, hi_ref, first_ref, init_ref, next_ref, slot_ref,
                xs_ref, wg_ref, wu_ref, wd_ref, ys_ref, wg_f, wu_f, wd_f, wg_s, wu_s, wd_s, sem, *, tm, layer):
    g = pl.program_id(0)
    chunks = xs_ref.shape[0] // tm

    def weight_copies(expert, slot):
        return [pltpu.make_async_copy(w.at[layer, expert], buf.at[slot], sem.at[slot])
                for w, buf in ((wg_ref, wg_f), (wu_ref, wu_f), (wd_ref, wd_f))]

    @pl.when(g == 0)
    def _():
        for cp in weight_copies(exp_ref[0], slot_ref[0]):
            cp.start()

    @pl.when(first_ref[g] == 1)
    def _():
        slot = slot_ref[g]
        for cp in weight_copies(exp_ref[g], slot):
            cp.wait()
        wg_s[...] = wg_f[slot].astype(BF16)
        wu_s[...] = wu_f[slot].astype(BF16)
        wd_s[...] = wd_f[slot].astype(BF16)

        @pl.when(next_ref[g] >= 0)
        def _():
            for cp in weight_copies(next_ref[g], 1 - slot):
                cp.start()

    lo = lo_ref[g]
    hi = hi_ref[g]

    @pl.when(hi > lo)
    def _():
        x = _load_token_tiles(xs_ref, tm, chunks).astype(BF16)
        hg = jnp.dot(x, wg_s[...], preferred_element_type=F32)
        hu = jnp.dot(x, wu_s[...], preferred_element_type=F32)
        h = (hg * _sigmoid(hg)) * hu
        y = jnp.dot(h.astype(BF16), wd_s[...], preferred_element_type=F32)
        row = lax.broadcasted_iota(jnp.int32, y.shape, 0)
        mine = (row >= lo) & (row < hi)

        @pl.when(init_ref[g] == 1)
        def _():
            _store_token_tiles(ys_ref, jnp.where(mine, y, 0.0))

        @pl.when(init_ref[g] == 0)
        def _():
            _store_token_tiles(ys_ref, jnp.where(mine, y, _load_token_tiles(ys_ref, tm, chunks)))


def _expert_ffn(meta, xs, w_gate, w_up, w_down, layer):
    d, d_e = w_gate.shape[-2:]
    chunks = d // LANES
    tm = FFN_TILE
    steps = meta[0].shape[0]
    x_map = lambda g, tile, *_: (tile[g], 0)
    return pl.pallas_call(
        functools.partial(_ffn_kernel, tm=tm, layer=layer),
        grid_spec=pltpu.PrefetchScalarGridSpec(
            num_scalar_prefetch=len(meta),
            grid=(steps,),
            in_specs=[
                pl.BlockSpec((tm * chunks, LANES), x_map),
                pl.BlockSpec(memory_space=pl.ANY),
                pl.BlockSpec(memory_space=pl.ANY),
                pl.BlockSpec(memory_space=pl.ANY),
            ],
            out_specs=pl.BlockSpec((tm * chunks, LANES), x_map),
            scratch_shapes=[
                pltpu.VMEM((2, d, d_e), F32),
                pltpu.VMEM((2, d, d_e), F32),
                pltpu.VMEM((2, d_e, d), F32),
                pltpu.VMEM((d, d_e), BF16),
                pltpu.VMEM((d, d_e), BF16),
                pltpu.VMEM((d_e, d), BF16),
                pltpu.SemaphoreType.DMA((2,)),
            ],
        ),
        out_shape=jax.ShapeDtypeStruct(xs.shape, F32),
        compiler_params=_params("arbitrary"),
        name="expert_ffn",
    )(*meta, xs, w_gate, w_up, w_down)


def _ffn_schedule(cnt, off, rows):
    tm = FFN_TILE
    n_tiles = rows // tm
    steps = n_tiles + N_EXPERTS - 1
    end = off + cnt
    first_tile = off // tm
    last_tile = jnp.where(cnt > 0, (end - 1) // tm, first_tile - 1)
    visits = last_tile - first_tile + 1
    vis_end = jnp.cumsum(visits)
    vis_start = vis_end - visits
    total = vis_end[-1]
    g = jnp.arange(steps, dtype=jnp.int32)
    gc = jnp.minimum(g, total - 1)
    exp = jnp.sum((vis_end[None, :] <= gc[:, None]).astype(jnp.int32), axis=1)
    is_exp = exp[:, None] == jnp.arange(N_EXPERTS, dtype=jnp.int32)[None, :]
    pick = lambda a: jnp.sum(jnp.where(is_exp, a[None, :], 0), axis=1)
    tile = pick(first_tile) + (gc - pick(vis_start))
    valid = g < total
    lo = jnp.where(valid, jnp.maximum(pick(off), tile * tm) - tile * tm, 0)
    hi = jnp.where(valid, jnp.minimum(pick(end), (tile + 1) * tm) - tile * tm, 0)
    prev_exp = jnp.concatenate([jnp.full((1,), -1, jnp.int32), exp[:-1]])
    prev_tile = jnp.concatenate([jnp.full((1,), -1, jnp.int32), tile[:-1]])
    first = (valid & (exp != prev_exp)).astype(jnp.int32)
    init = (valid & (tile != prev_tile)).astype(jnp.int32)
    ids = jnp.arange(N_EXPERTS, dtype=jnp.int32)
    later = (visits > 0)[None, :] & (ids[None, :] > ids[:, None])
    next_e = jnp.min(jnp.where(later, ids[None, :], N_EXPERTS), axis=1)
    next_e = jnp.where(next_e < N_EXPERTS, next_e, -1)
    slot = (jnp.cumsum(first) - 1) % 2
    i32 = lambda a: a.astype(jnp.int32)
    return i32(tile), i32(exp), i32(lo), i32(hi), first, init, i32(pick(next_e)), i32(slot)


def _combine_kernel(pos_ref, x1_ref, route_ref, p_ref, ys_ref, gp_ref, wg_ref, bg_ref, wp_ref,
                    o_ref, buf_a, buf_b, sem, *, n_tok):
    tm, d = x1_ref.shape
    chunks = d // LANES
    i = pl.program_id(0)
    last = pl.num_programs(0) - 1
    bufs = (buf_a, buf_b)

    def start_row(block, slot, r):
        for k in range(TOP_K):
            src = pos_ref[k * n_tok + block * tm + r]
            _token_copy(ys_ref, src, bufs[slot].at[k], r, sem.at[slot], chunks).start(priority=k % 2)

    def drain(slot):
        def body(r, carry):
            for k in range(TOP_K):
                _token_copy(ys_ref, 0, bufs[slot].at[k], 0, sem.at[slot], chunks).wait()
            return carry
        lax.fori_loop(0, tm, body, 0, unroll=DMA_UNROLL)

    @pl.when(i == 0)
    def _():
        def body(r, carry):
            start_row(0, 0, r)
            return carry
        lax.fori_loop(0, tm, body, 0, unroll=DMA_UNROLL)

    def step(slot):
        drain(slot)
        for r in range(tm):
            start_row(jnp.minimum(i + 1, last), 1 - slot, r)
        rec = route_ref[...]
        w0 = rec[:, R_W0:R_W0 + 1]
        w1 = rec[:, R_W1:R_W1 + 1]
        y0 = _load_token_tiles(bufs[slot].at[0], tm, chunks)
        y1 = _load_token_tiles(bufs[slot].at[1], tm, chunks)
        x2 = x1_ref[...] + (y0 * w0 + y1 * w1)
        n = _rms(x2) * gp_ref[...]
        gate = _sigmoid(jnp.dot(n.astype(BF16), wg_ref[...], preferred_element_type=F32) + bg_ref[...])
        pe = jnp.dot(p_ref[...].astype(BF16), wp_ref[...], preferred_element_type=F32)
        o_ref[...] = x2 + pe * gate

        @pl.when(i == last)
        def _():
            drain(1 - slot)

    for slot in range(2):
        pl.when(i % 2 == slot)(functools.partial(step, slot))


def _combine_ple(pos, x1, route, p3, ys, g_ple, w_gate, b_gate, w_proj, layer):
    t, d = x1.shape
    dp = p3.shape[-1]
    tm = COMBINE_BLOCK
    bpb = p3.shape[2] // tm
    row = lambda i, pos_ref: (i, 0)
    const = lambda i, pos_ref: (0, 0)
    return pl.pallas_call(
        functools.partial(_combine_kernel, n_tok=t),
        grid_spec=pltpu.PrefetchScalarGridSpec(
            num_scalar_prefetch=1,
            grid=(t // tm,),
            in_specs=[
                pl.BlockSpec((tm, d), row),
                pl.BlockSpec((tm, LANES), row),
                pl.BlockSpec((None, None, tm, dp), lambda i, pos_ref: (layer, i // bpb, i % bpb, 0)),
                pl.BlockSpec(memory_space=pl.ANY),
                pl.BlockSpec((1, d), const),
                pl.BlockSpec((d, d), const),
                pl.BlockSpec((1, d), const),
                pl.BlockSpec((dp, d), const),
            ],
            out_specs=pl.BlockSpec((tm, d), row),
            scratch_shapes=[
                pltpu.VMEM((TOP_K, tm * (d // LANES), LANES), F32),
                pltpu.VMEM((TOP_K, tm * (d // LANES), LANES), F32),
                pltpu.SemaphoreType.DMA((2,)),
            ],
        ),
        out_shape=jax.ShapeDtypeStruct((t, d), F32),
        compiler_params=_params("arbitrary"),
        name="combine_ple",
    )(pos, x1, route, p3, ys, g_ple, w_gate, b_gate, w_proj)


def _block_diag(w):
    nb, n, _ = w.shape
    eye = jnp.eye(nb, dtype=w.dtype)
    return jnp.einsum("hij,hg->higj", w, eye).reshape(nb * n, nb * n)


def kernel(x, p, mix_norm, w_in, b_forget, conv_w, conv_b, w_rgate, b_rgate, w_igate, b_igate, lru_lambda,
           q_norm, k_norm, lru_out_norm, att_out_norm, w_out, ffn_norm, w_group, b_group, w_router, b_router,
           w_exp_gate, w_exp_up, w_exp_down, ple_norm, w_ple_gate, b_ple_gate, w_ple_proj):
    batch, seq, d = x.shape
    depth = w_in.shape[0]
    d_rnn = conv_w.shape[-1]
    n_heads = b_forget.shape[-1]
    d_att = n_heads * HEAD_DIM
    t = batch * seq
    assert n_heads == SUBLANES and d_att % LANES == 0 and seq % SEQ_BLOCK == 0 and seq % Q_BLOCK == 0
    assert C_PIECES * SUBLANES <= LANES and N_EXPERTS + N_GROUPS <= ROUTE_ROWS
    assert Q_BLOCK == SEQ_BLOCK and (TOP_K * t) % FFN_TILE == 0
    assert t % DISPATCH_BLOCK == 0 and seq % COMBINE_BLOCK == 0
    assert w_router.shape[-1] == N_EXPERTS and w_group.shape[-1] == N_GROUPS

    idx = jnp.arange(SEQ_BLOCK)
    tri_incl = (idx[:, None] <= idx[None, :]).astype(BF16)
    tri_strict = (idx[:, None] < idx[None, :]).astype(BF16)
    hid = jnp.arange(d_att) // HEAD_DIM
    gsum = (hid[:, None] == hid[None, :]).astype(BF16)
    row2 = lambda a: a.reshape(1, -1).astype(F32)
    n_in_pad = 2 * d_rnn + 3 * d_att + LANES
    p3 = p

    x2 = x.reshape(t, d)
    for i in range(depth):
        w_pad = jnp.pad(w_in[i], ((0, 0), (0, n_in_pad - w_in.shape[-1]))).astype(BF16)
        qg = row2(jnp.tile(q_norm[i], n_heads) * (LOG2E / math.sqrt(HEAD_DIM)))
        kg = row2(jnp.tile(k_norm[i], n_heads))
        zx, zg, q, k, vt, cp = _in_proj(
            x2, row2(mix_norm[i]), w_pad, gsum, qg, kg, b_forget[i].reshape(n_heads, 1).astype(F32), tri_incl,
            batch=batch, seq=seq, d_rnn=d_rnn, d_att=d_att)

        w_gates = jnp.concatenate([_block_diag(w_rgate[i]), _block_diag(w_igate[i])], axis=1).astype(BF16)
        yr = _rglru(zx.reshape(batch, seq, d_rnn), zg.reshape(batch, seq, d_rnn), conv_w[i].astype(F32),
                    row2(conv_b[i]), w_gates, row2(b_rgate[i]), row2(b_igate[i]), row2(lru_lambda[i]),
                    row2(lru_out_norm[i]))
        ya = _attention(q.reshape(batch, seq, d_att), k.reshape(batch, seq, d_att),
                        cp.reshape(batch, seq, LANES), vt)

        w_rt = jnp.pad(jnp.concatenate([w_router[i], w_group[i]], axis=1),
                       ((0, 0), (0, LANES - N_EXPERTS - N_GROUPS))).astype(F32)
        wr_hi, wr_lo = _split_bf16(w_rt)
        b_rt = jnp.pad(jnp.concatenate([b_router[i], b_group[i]]), (0, LANES - N_EXPERTS - N_GROUPS))
        x1, xn, route, meta, cnt = _out_route(
            x2, yr.reshape(t, d_rnn), ya.reshape(t, d_att), row2(att_out_norm[i]), w_out[i].astype(BF16),
            row2(ffn_norm[i]), jnp.concatenate([wr_hi, wr_lo], axis=1), row2(b_rt), tri_strict, d_rnn=d_rnn)

        cnt_e = cnt[:N_EXPERTS, 0].astype(jnp.int32)
        off_e = jnp.cumsum(cnt_e) - cnt_e
        pos = _positions(off_e, meta).reshape(-1)

        xs = _dispatch(pos, xn, t)
        ys = _expert_ffn(_ffn_schedule(cnt_e, off_e, TOP_K * t), xs, w_exp_gate, w_exp_up, w_exp_down, i)
        x2 = _combine_ple(pos, x1, route, p3, ys, row2(ple_norm[i]), w_ple_gate[i].astype(BF16),
                          row2(b_ple_gate[i]), w_ple_proj[i].astype(BF16), i)
    return x2.reshape(batch, seq, d)
```
